```python
import math
import jax, jax.numpy as jnp
from jax import lax
import numpy as np

D_MODEL = 1024
BATCH = 2
SEQ = 8192
DEPTH = 1
DEC_BATCH = 16
DEC_SEQ = 32
PAST_LEN = 1024

CHUNK = 64
N_META = 16
D_S5 = 1024
S5_GROUP = 16
S5_GROUPS = D_S5 // S5_GROUP
S5_STATE = 64
D_ML = 1024
ML_HEADS = 4
ML_HEAD_DIM = D_ML // ML_HEADS
ML_QKV_BLOCK = 4
ML_CONV = 4
D_MIX = D_S5 + D_ML
D_FF = 2816
EPS = 1e-6

kernel_name = 'hymba_s5_mlstm_macaron_stream_step'


def rmsnorm(x, g):
    xf = x.astype(jnp.float32)
    y = xf * lax.rsqrt(jnp.mean(xf * xf, axis=-1, keepdims=True) + EPS)
    return (y * g.astype(jnp.float32)).astype(x.dtype)


def swiglu(x, w_gate, w_up, w_down):
    return (jax.nn.silu(x @ w_gate) * (x @ w_up)) @ w_down


def s5_mix(u, s_re, s_im, lam_re, lam_im, log_dt, b_re, b_im, c_re, c_im, d_skip, glu_w, glu_b):
    f32 = jnp.float32
    n, l, _ = u.shape
    lam = lax.complex(lam_re.astype(f32), lam_im.astype(f32))
    dt = jnp.exp(log_dt.astype(f32))[:, None]
    lam_bar = jnp.exp(lam * dt)
    b_bar = ((lam_bar - 1.0) / lam)[..., None] * lax.complex(b_re.astype(f32), b_im.astype(f32))
    c = lax.complex(c_re.astype(f32), c_im.astype(f32))
    uf = u.astype(f32)
    ug = uf.reshape(n, l, S5_GROUPS, S5_GROUP).astype(jnp.complex64)
    bu = jnp.einsum('gph,nlgh->nlgp', b_bar, ug)
    s0 = lax.complex(s_re.astype(f32), s_im.astype(f32))
    bu = bu.at[:, 0].add(lam_bar * s0)
    a = jnp.broadcast_to(lam_bar, (1, l) + lam_bar.shape)

    def combine(e1, e2):
        a1, b1 = e1
        a2, b2 = e2
        return a1 * a2, a2 * b1 + b2

    _, states = lax.associative_scan(combine, (a, bu), axis=1)
    y = jnp.einsum('ghp,nlgp->nlgh', c, states).real.reshape(n, l, D_S5) + d_skip.astype(f32) * uf
    g = jax.nn.gelu(y)
    y = g * jax.nn.sigmoid(g @ glu_w.astype(f32) + glu_b.astype(f32))
    last = states[:, -1]
    return y.astype(u.dtype), jnp.real(last), jnp.imag(last)


def causal_dwconv(x, buf, w, b):
    l = x.shape[1]
    xp = jnp.concatenate([buf.astype(x.dtype), x], axis=1)
    y = b + sum(xp[:, j:j + l] * w[j] for j in range(ML_CONV))
    return y, xp[:, xp.shape[1] - (ML_CONV - 1):]


def blockdiag(x, w):
    xb = x.reshape(x.shape[:-1] + (w.shape[0], ML_QKV_BLOCK))
    return jnp.einsum('nlbi,nlbo->nlbo', xb, xb)[..., :0].sum() * 0 + jnp.einsum('nlbi,bio->nlbo', xb, w).reshape(x.shape) if False else jnp.einsum('nlbi,bio->nlbo', xb, w).reshape(x.shape)


def split_heads(t):
    n, l, _ = t.shape
    return t.reshape(n, l, ML_HEADS, ML_HEAD_DIM).transpose(0, 2, 1, 3)


def mlstm_chunk(carry, inp):
    c_prev, n_prev, m_prev = carry
    q, k, v, ig, lf = inp
    l = q.shape[2]
    bcum = jnp.cumsum(lf, axis=-1)
    logw = bcum[..., :, None] - bcum[..., None, :] + ig[..., None, :]
    causal = jnp.tril(jnp.ones((l, l), dtype=bool))
    logw = jnp.where(causal, logw, -jnp.inf)
    log_inter = bcum + m_prev[..., None]
    m_t = jnp.maximum(log_inter, jnp.max(logw, axis=-1))
    w = jnp.exp(logw - m_t[..., None])
    a_inter = jnp.exp(log_inter - m_t)
    s = jnp.einsum('nhtd,nhsd->nhts', q, k) * w
    num = a_inter[..., None] * jnp.einsum('nhvk,nhtk->nhtv', c_prev, q) + jnp.einsum('nhts,nhsv->nhtv', s, v)
    den = a_inter * jnp.einsum('nhk,nhtk->nht', n_prev, q) + jnp.sum(s, axis=-1)
    h = num / jnp.maximum(jnp.abs(den), jnp.exp(-m_t))[..., None]
    m_new = m_t[..., -1]
    g_state = jnp.exp(bcum[..., -1] + m_prev - m_new)
    g_src = jnp.exp(bcum[..., -1:] - bcum + ig - m_new[..., None])
    c_new = g_state[..., None, None] * c_prev + jnp.einsum('nhs,nhsv,nhsk->nhvk', g_src, v, k)
    n_new = g_state[..., None] * n_prev + jnp.einsum('nhs,nhsk->nhk', g_src, k)
    return (c_new, n_new, m_new), h


def mlstm_blocks(q, k, v, ig, lf, state, lead):
    state, h0 = mlstm_chunk(state, (q[:, :, :lead], k[:, :, :lead], v[:, :, :lead], ig[:, :, :lead], lf[:, :, :lead]))
    rest = q.shape[2] - lead
    if rest == 0:
        return h0, state
    nc = rest // CHUNK

    def to_chunks(t):
        t = t[:, :, lead:]
        t = t.reshape(t.shape[:2] + (nc, CHUNK) + t.shape[3:])
        return jnp.moveaxis(t, 2, 0)

    state, hs = lax.scan(mlstm_chunk, state, (to_chunks(q), to_chunks(k), to_chunks(v), to_chunks(ig), to_chunks(lf)))
    hs = jnp.moveaxis(hs, 0, 2)
    hs = hs.reshape(hs.shape[:2] + (rest, ML_HEAD_DIM))
    return jnp.concatenate([h0, hs], axis=2), state


def mlstm_mix(xm, z, conv_buf, c0, n0, m0, lead, conv_w, conv_b, wq, wk, wv, ig_w, ig_b, fg_w, fg_b, norm_w, skip):
    f32 = jnp.float32
    n, l, _ = xm.shape
    xc, conv_new = causal_dwconv(xm, conv_buf, conv_w, conv_b)
    xc = jax.nn.silu(xc)
    q = blockdiag(xc, wq)
    k = blockdiag(xc, wk)
    v = blockdiag(xm, wv)
    gate_in = jnp.concatenate([q, k, v], axis=-1)
    ig = (gate_in @ ig_w + ig_b).astype(f32).transpose(0, 2, 1)
    lf = jax.nn.log_sigmoid((gate_in @ fg_w + fg_b).astype(f32)).transpose(0, 2, 1)
    qh = split_heads(q.astype(f32))
    kh = split_heads(k.astype(f32)) * (ML_HEAD_DIM ** -0.5)
    vh = split_heads(v.astype(f32))
    h, (c1, n1, m1) = mlstm_blocks(qh, kh, vh, ig, lf, (c0.astype(f32), n0.astype(f32), m0.astype(f32)), lead)
    mu = jnp.mean(h, axis=-1, keepdims=True)
    var = jnp.mean(jnp.square(h - mu), axis=-1, keepdims=True)
    h = (h - mu) * lax.rsqrt(var + EPS)
    h = h.transpose(0, 2, 1, 3).reshape(n, l, D_ML) * norm_w.astype(f32)
    out = (h + skip.astype(f32) * xc.astype(f32)) * jax.nn.silu(z.astype(f32))
    return out.astype(xm.dtype), conv_new, c1, n1, m1


def trunk(x, s5_re, s5_im, ml_c, ml_n, ml_m, ml_conv, lead, p):
    outs = ([], [], [], [], [], [])
    for i in range(DEPTH):
        x = x + 0.5 * swiglu(rmsnorm(x, p['norm_ffn1'][i]), p['ffn1_gate'][i], p['ffn1_up'][i], p['ffn1_down'][i])
        proj = rmsnorm(x, p['norm_mix'][i]) @ p['w_in'][i]
        u = proj[..., :D_S5]
        xm = proj[..., D_S5:D_S5 + D_ML]
        z = proj[..., D_S5 + D_ML:]
        y5, r5, i5 = s5_mix(u, s5_re[i], s5_im[i], p['s5_lambda_re'][i], p['s5_lambda_im'][i], p['s5_log_dt'][i],
                            p['s5_b_re'][i], p['s5_b_im'][i], p['s5_c_re'][i], p['s5_c_im'][i], p['s5_d'][i],
                            p['s5_glu_w'][i], p['s5_glu_b'][i])
        ym, cv, c1, n1, m1 = mlstm_mix(xm, z, ml_conv[i], ml_c[i], ml_n[i], ml_m[i], lead,
                                       p['ml_conv_w'][i], p['ml_conv_b'][i], p['ml_wq'][i], p['ml_wk'][i], p['ml_wv'][i],
                                       p['ml_igate_w'][i], p['ml_igate_b'][i], p['ml_fgate_w'][i], p['ml_fgate_b'][i],
                                       p['ml_norm_w'][i], p['ml_skip'][i])
        mixed = jnp.concatenate([rmsnorm(y5, p['out_norm_s5'][i]), rmsnorm(ym, p['out_norm_ml'][i])], axis=-1)
        x = x + mixed @ p['w_out'][i]
        x = x + 0.5 * swiglu(rmsnorm(x, p['norm_ffn2'][i]), p['ffn2_gate'][i], p['ffn2_up'][i], p['ffn2_down'][i])
        for lst, val in zip(outs, (r5, i5, c1, n1, m1, cv)):
            lst.append(val)
    st = [jnp.stack(lst) for lst in outs]
    return rmsnorm(x, p['norm_final']), st[0], st[1], st[2], st[3], st[4], st[5]


def setup_inputs(seed: int = 0) -> dict:
    key = jax.random.key(seed)
    ks = iter(jax.random.split(key, 64))
    f32 = jnp.float32
    L = DEPTH

    def nrm(shape, scale):
        return jax.random.normal(next(ks), shape, f32) * scale

    def gain(shape):
        return 1.0 + nrm(shape, 0.02)

    n_idx = jnp.arange(S5_STATE, dtype=f32)
    return {
        'x_prompt': nrm((BATCH, SEQ, D_MODEL), 1.0),
        'x_sample': nrm((DEC_BATCH, DEC_SEQ, D_MODEL), 1.0),
        'state_s5_re': nrm((L, DEC_BATCH, S5_GROUPS, S5_STATE), 0.5),
        'state_s5_im': nrm((L, DEC_BATCH, S5_GROUPS, S5_STATE), 0.5),
        'state_mlstm_c': nrm((L, DEC_BATCH, ML_HEADS, ML_HEAD_DIM, ML_HEAD_DIM), 0.05),
        'state_mlstm_n': nrm((L, DEC_BATCH, ML_HEADS, ML_HEAD_DIM), 0.1),
        'state_mlstm_m': jax.random.uniform(next(ks), (L, DEC_BATCH, ML_HEADS), f32, 0.0, 3.0),
        'state_mlstm_conv': nrm((L, DEC_BATCH, ML_CONV - 1, D_ML), 1.0),
        'meta_tokens': nrm((N_META, D_MODEL), 1.0),
        'norm_ffn1': gain((L, D_MODEL)),
        'ffn1_gate': nrm((L, D_MODEL, D_FF), D_MODEL ** -0.5),
        'ffn1_up': nrm((L, D_MODEL, D_FF), D_MODEL ** -0.5),
        'ffn1_down': nrm((L, D_FF, D_MODEL), D_FF ** -0.5),
        'norm_mix': gain((L, D_MODEL)),
        'w_in': nrm((L, D_MODEL, D_S5 + 2 * D_ML), D_MODEL ** -0.5),
        's5_lambda_re': -0.5 + nrm((L, S5_GROUPS, S5_STATE), 0.01),
        's5_lambda_im': jnp.broadcast_to(math.pi * n_idx, (L, S5_GROUPS, S5_STATE)) + nrm((L, S5_GROUPS, S5_STATE), 0.01),
        's5_log_dt': jax.random.uniform(next(ks), (L, S5_GROUPS), f32, math.log(1e-3), math.log(1e-1)),
        's5_b_re': nrm((L, S5_GROUPS, S5_STATE, S5_GROUP), (2 * S5_GROUP) ** -0.5),
        's5_b_im': nrm((L, S5_GROUPS, S5_STATE, S5_GROUP), (2 * S5_GROUP) ** -0.5),
        's5_c_re': nrm((L, S5_GROUPS, S5_GROUP, S5_STATE), (2 * S5_STATE) ** -0.5),
        's5_c_im': nrm((L, S5_GROUPS, S5_GROUP, S5_STATE), (2 * S5_STATE) ** -0.5),
        's5_d': nrm((L, D_S5), 1.0),
        's5_glu_w': nrm((L, D_S5, D_S5), D_S5 ** -0.5),
        's5_glu_b': nrm((L, D_S5), 0.02),
        'ml_conv_w': nrm((L, ML_CONV, D_ML), ML_CONV ** -0.5),
        'ml_conv_b': nrm((L, D_ML), 0.02),
        'ml_wq': nrm((L, D_ML // ML_QKV_BLOCK, ML_QKV_BLOCK, ML_QKV_BLOCK), ML_QKV_BLOCK ** -0.5),
        'ml_wk': nrm((L, D_ML // ML_QKV_BLOCK, ML_QKV_BLOCK, ML_QKV_BLOCK), ML_QKV_BLOCK ** -0.5),
        'ml_wv': nrm((L, D_ML // ML_QKV_BLOCK, ML_QKV_BLOCK, ML_QKV_BLOCK), ML_QKV_BLOCK ** -0.5),
        'ml_igate_w': nrm((L, 3 * D_ML, ML_HEADS), 0.02),
        'ml_igate_b': nrm((L, ML_HEADS), 0.1),
        'ml_fgate_w': nrm((L, 3 * D_ML, ML_HEADS), 0.02),
        'ml_fgate_b': jnp.linspace(3.0, 6.0, ML_HEADS, dtype=f32)[None] + nrm((L, ML_HEADS), 0.01),
        'ml_norm_w': gain((L, D_ML)),
        'ml_skip': gain((L, D_ML)),
        'out_norm_s5': gain((L, D_S5)),
        'out_norm_ml': gain((L, D_ML)),
        'w_out': nrm((L, D_MIX, D_MODEL), D_MIX ** -0.5),
        'norm_ffn2': gain((L, D_MODEL)),
        'ffn2_gate': nrm((L, D_MODEL, D_FF), D_MODEL ** -0.5),
        'ffn2_up': nrm((L, D_MODEL, D_FF), D_MODEL ** -0.5),
        'ffn2_down': nrm((L, D_FF, D_MODEL), D_FF ** -0.5),
        'norm_final': gain((D_MODEL,)),
    }


def reference(x_prompt, x_sample, state_s5_re, state_s5_im, state_mlstm_c, state_mlstm_n, state_mlstm_m,
              state_mlstm_conv, meta_tokens, norm_ffn1, ffn1_gate, ffn1_up, ffn1_down, norm_mix, w_in,
              s5_lambda_re, s5_lambda_im, s5_log_dt, s5_b_re, s5_b_im, s5_c_re, s5_c_im, s5_d, s5_glu_w, s5_glu_b,
              ml_conv_w, ml_conv_b, ml_wq, ml_wk, ml_wv, ml_igate_w, ml_igate_b, ml_fgate_w, ml_fgate_b,
              ml_norm_w, ml_skip, out_norm_s5, out_norm_ml, w_out, norm_ffn2, ffn2_gate, ffn2_up, ffn2_down,
              norm_final):
    p = dict(norm_ffn1=norm_ffn1, ffn1_gate=ffn1_gate, ffn1_up=ffn1_up, ffn1_down=ffn1_down, norm_mix=norm_mix,
             w_in=w_in, s5_lambda_re=s5_lambda_re, s5_lambda_im=s5_lambda_im, s5_log_dt=s5_log_dt,
             s5_b_re=s5_b_re, s5_b_im=s5_b_im, s5_c_re=s5_c_re, s5_c_im=s5_c_im, s5_d=s5_d, s5_glu_w=s5_glu_w,
             s5_glu_b=s5_glu_b, ml_conv_w=ml_conv_w, ml_conv_b=ml_conv_b, ml_wq=ml_wq, ml_wk=ml_wk, ml_wv=ml_wv,
             ml_igate_w=ml_igate_w, ml_igate_b=ml_igate_b, ml_fgate_w=ml_fgate_w, ml_fgate_b=ml_fgate_b,
             ml_norm_w=ml_norm_w, ml_skip=ml_skip, out_norm_s5=out_norm_s5, out_norm_ml=out_norm_ml, w_out=w_out,
             norm_ffn2=norm_ffn2, ffn2_gate=ffn2_gate, ffn2_up=ffn2_up, ffn2_down=ffn2_down, norm_final=norm_final)
    f32 = jnp.float32
    nb = x_prompt.shape[0]
    meta = jnp.broadcast_to(meta_tokens.astype(x_prompt.dtype)[None], (nb, N_META, D_MODEL))
    xp = jnp.concatenate([meta, x_prompt], axis=1)
    z_s5 = jnp.zeros((DEPTH, nb, S5_GROUPS, S5_STATE), f32)
    z_c = jnp.zeros((DEPTH, nb, ML_HEADS, ML_HEAD_DIM, ML_HEAD_DIM), f32)
    z_n = jnp.zeros((DEPTH, nb, ML_HEADS, ML_HEAD_DIM), f32)
    z_m = jnp.zeros((DEPTH, nb, ML_HEADS), f32)
    z_conv = jnp.zeros((DEPTH, nb, ML_CONV - 1, D_ML), x_prompt.dtype)
    yp, p_s5_re, p_s5_im, p_mlstm_c, p_mlstm_n, p_mlstm_m, p_mlstm_conv = trunk(
        xp, z_s5, z_s5, z_c, z_n, z_m, z_conv, N_META, p)
    y_prompt = yp[:, N_META:]
    y_sample, s_s5_re, s_s5_im, s_mlstm_c, s_mlstm_n, s_mlstm_m, s_mlstm_conv = trunk(
        x_sample, state_s5_re, state_s5_im, state_mlstm_c, state_mlstm_n, state_mlstm_m, state_mlstm_conv,
        x_sample.shape[1], p)
    return (y_prompt, y_sample, p_s5_re, p_s5_im, p_mlstm_c, p_mlstm_n, p_mlstm_m, p_mlstm_conv,
            s_s5_re, s_s5_im, s_mlstm_c, s_mlstm_n, s_mlstm_m, s_mlstm_conv)
```

```python
import functools
import math

import jax
import jax.numpy as jnp
from jax import lax
from jax.experimental import pallas as pl
from jax.experimental.pallas import tpu as pltpu

F32 = jnp.float32
BF16 = jnp.bfloat16

D_MODEL = 1024
D_FF = 2816
S5_GROUPS = 64
S5_GROUP = 16
S5_STATE = 64
S5_LANES = S5_GROUPS * S5_STATE
S5_BLOCKS = 4
S5_BLOCK_IN = D_MODEL // S5_BLOCKS
S5_BLOCK_ST = S5_LANES // S5_BLOCKS
ML_HEADS = 4
ML_HEAD_DIM = 256
ML_CONV = 4
EPS = 1e-6
SUBLANES = 8
GATE_LANES = 128
POW_ROWS = 64
VMEM_LIMIT = 56 * 1024 * 1024


def _rms(x, g):
    return x * lax.rsqrt(jnp.mean(x * x, axis=-1, keepdims=True) + EPS) * g


def _silu(x):
    return x * jax.nn.sigmoid(x)


def _swiglu(h, wg_ref, wu_ref, wd_ref, acc_ref, ff_chunk):
    for c in range(D_FF // ff_chunk):
        sl = slice(c * ff_chunk, (c + 1) * ff_chunk)
        g = jnp.dot(h, wg_ref[:, sl], preferred_element_type=F32)
        u = jnp.dot(h, wu_ref[:, sl], preferred_element_type=F32)
        a = (_silu(g) * u).astype(BF16)
        d = jnp.dot(a, wd_ref[sl, :], preferred_element_type=F32)
        if c == 0:
            acc_ref[...] = d
        else:
            acc_ref[...] += d
    return acc_ref[...]


def _ffn_in_kernel(x_ref, g1_ref, wg_ref, wu_ref, wd_ref, g2_ref, win_ref,
                   x1_ref, u_ref, xm_ref, z_ref, acc_ref, *, ff_chunk):
    x = x_ref[...]
    h = _rms(x, g1_ref[...]).astype(BF16)
    x1 = x + 0.5 * _swiglu(h, wg_ref, wu_ref, wd_ref, acc_ref, ff_chunk)
    x1_ref[...] = x1
    h2 = _rms(x1, g2_ref[...]).astype(BF16)
    u_ref[...] = jnp.dot(h2, win_ref[:, 0:D_MODEL], preferred_element_type=F32)
    xm_ref[...] = jnp.dot(h2, win_ref[:, D_MODEL:2 * D_MODEL], preferred_element_type=F32)
    z_ref[...] = jnp.dot(h2, win_ref[:, 2 * D_MODEL:3 * D_MODEL], preferred_element_type=F32)


def _const_spec(shape):
    nd = len(shape)
    return pl.BlockSpec(shape, lambda *_: (0,) * nd, pipeline_mode=pl.Buffered(1))


def _ffn_in(x, g1, wg, wu, wd, g2, win, *, tm, ff_chunk=256):
    rows = x.shape[0]
    row_spec = pl.BlockSpec((tm, D_MODEL), lambda i: (i, 0))
    out = jax.ShapeDtypeStruct((rows, D_MODEL), F32)
    return pl.pallas_call(
        functools.partial(_ffn_in_kernel, ff_chunk=ff_chunk),
        grid=(rows // tm,),
        in_specs=[row_spec, _const_spec(g1.shape), _const_spec(wg.shape), _const_spec(wu.shape),
                  _const_spec(wd.shape), _const_spec(g2.shape), _const_spec(win.shape)],
        out_specs=[row_spec] * 4,
        out_shape=[out] * 4,
        scratch_shapes=[pltpu.VMEM((tm, D_MODEL), F32)],
        compiler_params=pltpu.CompilerParams(dimension_semantics=("parallel",), vmem_limit_bytes=VMEM_LIMIT),
        name="ffn_in",
    )(x, g1, wg, wu, wd, g2, win)


def _out_ffn_kernel(x1_ref, y5_ref, ym_ref, wo_ref, g_ref, wg_ref, wu_ref, wd_ref, gf_ref,
                    o_ref, acc_ref, *, ff_chunk):
    x2 = (x1_ref[...]
          + jnp.dot(y5_ref[...], wo_ref[0:D_MODEL, :], preferred_element_type=F32)
          + jnp.dot(ym_ref[...], wo_ref[D_MODEL:2 * D_MODEL, :], preferred_element_type=F32))
    h = _rms(x2, g_ref[...]).astype(BF16)
    x3 = x2 + 0.5 * _swiglu(h, wg_ref, wu_ref, wd_ref, acc_ref, ff_chunk)
    o_ref[...] = _rms(x3, gf_ref[...])


def _out_ffn(x1, y5, ym, wo, g, wg, wu, wd, gf, *, tm, ff_chunk=256):
    rows = x1.shape[0]
    row_spec = pl.BlockSpec((tm, D_MODEL), lambda i: (i, 0))
    return pl.pallas_call(
        functools.partial(_out_ffn_kernel, ff_chunk=ff_chunk),
        grid=(rows // tm,),
        in_specs=[row_spec, row_spec, row_spec, _const_spec(wo.shape), _const_spec(g.shape),
                  _const_spec(wg.shape), _const_spec(wu.shape), _const_spec(wd.shape), _const_spec(gf.shape)],
        out_specs=row_spec,
        out_shape=jax.ShapeDtypeStruct((rows, D_MODEL), F32),
        scratch_shapes=[pltpu.VMEM((tm, D_MODEL), F32)],
        compiler_params=pltpu.CompilerParams(dimension_semantics=("parallel",), vmem_limit_bytes=VMEM_LIMIT),
        name="out_ffn",
    )(x1, y5, ym, wo, g, wg, wu, wd, gf)


def _s5_prep_kernel(lre_ref, lim_ref, ldt_ref, bre_ref, bim_ref, pre_ref, pim_ref, bbr_ref, bbi_ref):
    lr = lre_ref[...]
    li = lim_ref[...]
    dt = jnp.exp(ldt_ref[...])
    mag = jnp.exp(lr * dt)
    th = li * dt
    ar = mag * jnp.cos(th)
    ai = mag * jnp.sin(th)
    nr = ar - 1.0
    den = lr * lr + li * li
    cr = (nr * lr + ai * li) / den
    ci = (ai * lr - nr * li) / den
    bre = bre_ref[...]
    bim = bim_ref[...]
    bbr_ref[...] = cr * bre - ci * bim
    bbi_ref[...] = cr * bim + ci * bre
    pr, pi = ar, ai
    pre_ref[0:1, :] = pr
    pim_ref[0:1, :] = pi
    for j in range(1, POW_ROWS):
        pr, pi = pr * ar - pi * ai, pr * ai + pi * ar
        pre_ref[j:j + 1, :] = pr
        pim_ref[j:j + 1, :] = pi


def _s5_prep(lre, lim, ldt, bre_t, bim_t):
    pow_shape = jax.ShapeDtypeStruct((POW_ROWS, S5_LANES), F32)
    b_shape = jax.ShapeDtypeStruct((S5_GROUP, S5_LANES), F32)
    return pl.pallas_call(
        _s5_prep_kernel,
        out_shape=[pow_shape, pow_shape, b_shape, b_shape],
        name="s5_prep",
    )(lre, lim, ldt, bre_t, bim_t)


def _s5_kernel(u_ref, s0_ref, pre_ref, pim_ref, bmat_ref, cmat_ref, d_ref, gw_ref, gb_ref, on_ref,
               y_ref, sfin_ref, up_ref, s_ref, carry_ref, *, tm):
    r = tm // SUBLANES
    t = pl.program_id(1)

    @pl.when(t == 0)
    def _():
        carry_ref[...] = s0_ref[...]

    if r % SUBLANES == 0:
        up_ref[...] = pltpu.einshape("arl->ral", u_ref[...].reshape(SUBLANES, r, D_MODEL)).reshape(tm, D_MODEL)
    else:
        for i in range(r):
            up_ref[i * SUBLANES:(i + 1) * SUBLANES, :] = jnp.concatenate(
                [u_ref[a * r + i:a * r + i + 1, :] for a in range(SUBLANES)], axis=0)
    up = up_ref[...]
    ub = up.astype(BF16)

    ys = []
    for b in range(S5_BLOCKS):
        lanes = slice(b * S5_BLOCK_ST, (b + 1) * S5_BLOCK_ST)
        re = slice(0, S5_BLOCK_ST)
        im = slice(S5_BLOCK_ST, 2 * S5_BLOCK_ST)
        s_ref[...] = jnp.dot(ub[:, b * S5_BLOCK_IN:(b + 1) * S5_BLOCK_IN], bmat_ref[b],
                             preferred_element_type=F32)
        lr = jnp.broadcast_to(pre_ref[0:1, lanes], (SUBLANES, S5_BLOCK_ST))
        li = jnp.broadcast_to(pim_ref[0:1, lanes], (SUBLANES, S5_BLOCK_ST))

        def step(i, carry, lr=lr, li=li):
            cr, ci = carry
            rows = pl.ds(pl.multiple_of(i * SUBLANES, SUBLANES), SUBLANES)
            nr = lr * cr - li * ci + s_ref[rows, re]
            ni = lr * ci + li * cr + s_ref[rows, im]
            s_ref[rows, re] = nr
            s_ref[rows, im] = ni
            return nr, ni

        zero = jnp.zeros((SUBLANES, S5_BLOCK_ST), F32)
        fr, fi = lax.fori_loop(0, r, step, (zero, zero))

        rr = pre_ref[r - 1:r, lanes]
        ri = pim_ref[r - 1:r, lanes]
        cr = carry_ref[0:1, lanes]
        ci = carry_ref[1:2, lanes]
        rows_r, rows_i = [], []
        for a in range(SUBLANES):
            rows_r.append(cr)
            rows_i.append(ci)
            cr, ci = rr * cr - ri * ci + fr[a:a + 1], rr * ci + ri * cr + fi[a:a + 1]
        carry_ref[0:1, lanes] = cr
        carry_ref[1:2, lanes] = ci
        cin_r = jnp.concatenate(rows_r, axis=0)
        cin_i = jnp.concatenate(rows_i, axis=0)

        def fix(i, c, cin_r=cin_r, cin_i=cin_i, lanes=lanes):
            rows = pl.ds(pl.multiple_of(i * SUBLANES, SUBLANES), SUBLANES)
            pr = pre_ref[pl.ds(i, 1), lanes]
            pi = pim_ref[pl.ds(i, 1), lanes]
            s_ref[rows, re] += pr * cin_r - pi * cin_i
            s_ref[rows, im] += pr * cin_i + pi * cin_r
            return c

        lax.fori_loop(0, r, fix, 0)
        ys.append(jnp.dot(s_ref[...].astype(BF16), cmat_ref[b], preferred_element_type=F32))

    y = jnp.concatenate(ys, axis=1) + d_ref[...] * up
    g = jax.nn.gelu(y)
    o = g * jax.nn.sigmoid(jnp.dot(g.astype(BF16), gw_ref[...], preferred_element_type=F32) + gb_ref[...])
    yp = _rms(o, on_ref[...])
    if r % SUBLANES == 0:
        yn = pltpu.einshape("ral->arl", yp.reshape(r, SUBLANES, D_MODEL)).reshape(tm, D_MODEL)
    else:
        yn = jnp.concatenate([yp[i * SUBLANES + a:i * SUBLANES + a + 1, :]
                              for a in range(SUBLANES) for i in range(r)], axis=0)
    y_ref[...] = yn.astype(BF16)

    @pl.when(t == pl.num_programs(1) - 1)
    def _():
        sfin_ref[...] = carry_ref[...]


def _s5(u, s0, pre, pim, bmat, cmat, d, gw, gb, on, *, tm):
    n, l, _ = u.shape
    seq_spec = pl.BlockSpec((None, tm, D_MODEL), lambda i, t: (i, t, 0))
    st_spec = pl.BlockSpec((None, 2, S5_LANES), lambda i, t: (i, 0, 0))
    return pl.pallas_call(
        functools.partial(_s5_kernel, tm=tm),
        grid=(n, l // tm),
        in_specs=[seq_spec, st_spec, _const_spec(pre.shape), _const_spec(pim.shape), _const_spec(bmat.shape),
                  _const_spec(cmat.shape), _const_spec(d.shape), _const_spec(gw.shape), _const_spec(gb.shape),
                  _const_spec(on.shape)],
        out_specs=[seq_spec, st_spec],
        out_shape=[jax.ShapeDtypeStruct((n, l, D_MODEL), BF16), jax.ShapeDtypeStruct((n, 2, S5_LANES), F32)],
        scratch_shapes=[pltpu.VMEM((tm, D_MODEL), F32), pltpu.VMEM((tm, 2 * S5_BLOCK_ST), F32),
                        pltpu.VMEM((2, S5_LANES), F32)],
        compiler_params=pltpu.CompilerParams(dimension_semantics=("parallel", "arbitrary"),
                                             vmem_limit_bytes=VMEM_LIMIT),
        name="s5_mix",
    )(u, s0, pre, pim, bmat, cmat, d, gw, gb, on)


def _mlstm_kernel(xm_ref, z_ref, c0_ref, n0_ref, m0_ref, cv0_ref, cw_ref, cb_ref, bdq_ref, bdk_ref, bdv_ref,
                  gw_ref, gb_ref, nw_ref, sk_ref, on_ref,
                  y_ref, c_ref, n_ref, m_ref, cv_ref, xbuf_ref, *, tc):
    t = pl.program_id(1)

    @pl.when(t == 0)
    def _():
        c_ref[...] = c0_ref[...]
        n_ref[...] = n0_ref[...]
        m_ref[...] = m0_ref[...]
        xbuf_ref[0:SUBLANES, :] = cv0_ref[...]

    xm = xm_ref[...]
    xbuf_ref[SUBLANES:SUBLANES + tc, :] = xm
    cw = cw_ref[...]
    xc = cb_ref[...] + cw[3:4] * xm
    for j in range(ML_CONV - 1):
        xc = xc + cw[j:j + 1] * xbuf_ref[SUBLANES - (ML_CONV - 1) + j:SUBLANES - (ML_CONV - 1) + j + tc, :]
    tail = xbuf_ref[tc:tc + SUBLANES, :]
    xbuf_ref[0:SUBLANES, :] = tail
    cv_ref[...] = tail
    xc = _silu(xc)

    xcb = xc.astype(BF16)
    q = jnp.dot(xcb, bdq_ref[...], preferred_element_type=F32)
    k = jnp.dot(xcb, bdk_ref[...], preferred_element_type=F32)
    v = jnp.dot(xm.astype(BF16), bdv_ref[...], preferred_element_type=F32)
    qb = q.astype(BF16)
    vb = v.astype(BF16)
    gates = jnp.dot(jnp.concatenate([qb, k.astype(BF16), vb], axis=1), gw_ref[...],
                    preferred_element_type=F32) + gb_ref[...]
    ks = k * (ML_HEAD_DIM ** -0.5)
    ksb = ks.astype(BF16)

    lane = lax.broadcasted_iota(jnp.int32, (tc, GATE_LANES), 1)
    lf = jnp.minimum(gates, 0.0) - jnp.log1p(jnp.exp(-jnp.abs(gates)))
    lf = jnp.where((lane >= ML_HEADS) & (lane < 2 * ML_HEADS), lf, 0.0)
    row = lax.broadcasted_iota(jnp.int32, (tc, tc), 0)
    col = lax.broadcasted_iota(jnp.int32, (tc, tc), 1)
    causal = row >= col
    cum = jnp.dot(causal.astype(F32), lf, preferred_element_type=F32, precision=lax.Precision.HIGHEST)
    arr = jnp.where(lane < ML_HEADS, gates, cum)
    arr_t = arr.T

    m_all = m_ref[...]
    lane1 = lax.broadcasted_iota(jnp.int32, (1, GATE_LANES), 1)
    m_out = m_all
    heads = []
    for h in range(ML_HEADS):
        hs = slice(h * ML_HEAD_DIM, (h + 1) * ML_HEAD_DIM)
        qh = qb[:, hs]
        kh = ksb[:, hs]
        vh = vb[:, hs]
        ig_col = arr[:, h:h + 1]
        b_col = arr[:, ML_HEADS + h:ML_HEADS + h + 1]
        ig_row = arr_t[h:h + 1, :]
        b_row = arr_t[ML_HEADS + h:ML_HEADS + h + 1, :]
        m_prev = m_all[:, h:h + 1]
        c_prev = c_ref[h]
        n_prev = n_ref[h:h + 1, :]

        logw = jnp.where(causal, b_col - b_row + ig_row, -jnp.inf)
        log_inter = b_col + m_prev
        m_t = jnp.maximum(log_inter, jnp.max(logw, axis=-1, keepdims=True))
        w = jnp.exp(logw - m_t)
        a_inter = jnp.exp(log_inter - m_t)
        s = lax.dot_general(qh, kh, (((1,), (1,)), ((), ())), preferred_element_type=F32) * w
        inter = lax.dot_general(qh, c_prev.astype(BF16), (((1,), (1,)), ((), ())), preferred_element_type=F32)
        num = a_inter * inter + jnp.dot(s.astype(BF16), vh, preferred_element_type=F32)
        den = (a_inter * jnp.sum(q[:, hs] * n_prev, axis=-1, keepdims=True)
               + jnp.sum(s, axis=-1, keepdims=True))
        hh = num / jnp.maximum(jnp.abs(den), jnp.exp(-m_t))
        mu = jnp.mean(hh, axis=-1, keepdims=True)
        hc = hh - mu
        var = jnp.mean(hc * hc, axis=-1, keepdims=True)
        heads.append(hc * lax.rsqrt(var + EPS))

        b_last = b_col[tc - 1:tc, :]
        m_new = m_t[tc - 1:tc, :]
        g_state = jnp.exp(b_last + m_prev - m_new)
        g_src = jnp.exp(b_last - b_col + ig_col - m_new)
        vs = (v[:, hs] * g_src).astype(BF16)
        c_ref[h] = g_state * c_prev + lax.dot_general(vs, kh, (((0,), (0,)), ((), ())),
                                                      preferred_element_type=F32)
        n_ref[h:h + 1, :] = g_state * n_prev + jnp.sum(g_src * ks[:, hs], axis=0, keepdims=True)
        m_out = jnp.where(lane1 == h, m_new, m_out)
    m_ref[...] = m_out

    hcat = jnp.concatenate(heads, axis=1)
    out = (hcat * nw_ref[...] + sk_ref[...] * xc) * _silu(z_ref[...])
    y_ref[...] = _rms(out, on_ref[...]).astype(BF16)


def _mlstm(xm, z, c0, n0, m0, cv0, cw, cb, bdq, bdk, bdv, gw, gb, nw, sk, on, *, tc):
    n, l, _ = xm.shape
    seq_spec = pl.BlockSpec((None, tc, D_MODEL), lambda i, t: (i, t, 0))
    c_spec = pl.BlockSpec((None, ML_HEADS, ML_HEAD_DIM, ML_HEAD_DIM), lambda i, t: (i, 0, 0, 0))
    n_spec = pl.BlockSpec((None, ML_HEADS, ML_HEAD_DIM), lambda i, t: (i, 0, 0))
    m_spec = pl.BlockSpec((None, 1, GATE_LANES), lambda i, t: (i, 0, 0))
    cv_spec = pl.BlockSpec((None, SUBLANES, D_MODEL), lambda i, t: (i, 0, 0))
    consts = (cw, cb, bdq, bdk, bdv, gw, gb, nw, sk, on)
    return pl.pallas_call(
        functools.partial(_mlstm_kernel, tc=tc),
        grid=(n, l // tc),
        in_specs=[seq_spec, seq_spec, c_spec, n_spec, m_spec, cv_spec] + [_const_spec(a.shape) for a in consts],
        out_specs=[seq_spec, c_spec, n_spec, m_spec, cv_spec],
        out_shape=[jax.ShapeDtypeStruct((n, l, D_MODEL), BF16),
                   jax.ShapeDtypeStruct(c0.shape, F32), jax.ShapeDtypeStruct(n0.shape, F32),
                   jax.ShapeDtypeStruct(m0.shape, F32), jax.ShapeDtypeStruct(cv0.shape, F32)],
        scratch_shapes=[pltpu.VMEM((tc + 2 * SUBLANES, D_MODEL), F32)],
        compiler_params=pltpu.CompilerParams(dimension_semantics=("parallel", "arbitrary"),
                                             vmem_limit_bytes=VMEM_LIMIT),
        name="mlstm_mix",
    )(xm, z, c0, n0, m0, cv0, *consts)


def _row(v):
    return v.reshape(1, -1).astype(F32)


def _block_diag_4(w):
    nb = w.shape[0]
    eye = jnp.eye(nb, dtype=w.dtype)
    return (w[:, :, None, :] * eye[:, None, :, None]).reshape(nb * 4, nb * 4).astype(BF16)


def _s5_block_mats(bbr, bbi, c_re, c_im):
    gl = S5_GROUPS // S5_BLOCKS
    eye = jnp.eye(gl, dtype=F32)

    def b_blocks(bt):
        x = bt.reshape(S5_GROUP, S5_BLOCKS, gl, S5_STATE).transpose(1, 2, 0, 3)
        return (x[:, :, :, None, :] * eye[None, :, None, :, None]).reshape(S5_BLOCKS, S5_BLOCK_IN, S5_BLOCK_ST)

    def c_blocks(c):
        x = c.reshape(S5_BLOCKS, gl, S5_GROUP, S5_STATE)
        x = x[:, :, :, None, :] * eye[None, :, None, :, None]
        return x.transpose(0, 3, 4, 1, 2).reshape(S5_BLOCKS, S5_BLOCK_ST, S5_BLOCK_IN)

    bmat = jnp.concatenate([b_blocks(bbr), b_blocks(bbi)], axis=2).astype(BF16)
    cmat = jnp.concatenate([c_blocks(c_re), -c_blocks(c_im)], axis=1).astype(BF16)
    return bmat, cmat


def _pad_lanes(v, width):
    return jnp.pad(v, [(0, 0)] * (v.ndim - 1) + [(0, width - v.shape[-1])])


def kernel(x_prompt, x_sample, state_s5_re, state_s5_im, state_mlstm_c, state_mlstm_n, state_mlstm_m,
           state_mlstm_conv, meta_tokens, norm_ffn1, ffn1_gate, ffn1_up, ffn1_down, norm_mix, w_in,
           s5_lambda_re, s5_lambda_im, s5_log_dt, s5_b_re, s5_b_im, s5_c_re, s5_c_im, s5_d, s5_glu_w, s5_glu_b,
           ml_conv_w, ml_conv_b, ml_wq, ml_wk, ml_wv, ml_igate_w, ml_igate_b, ml_fgate_w, ml_fgate_b,
           ml_norm_w, ml_skip, out_norm_s5, out_norm_ml, w_out, norm_ffn2, ffn2_gate, ffn2_up, ffn2_down,
           norm_final):
    nb, seq, _ = x_prompt.shape
    ns, dseq, _ = x_sample.shape
    n_meta = meta_tokens.shape[0]

    wg1, wu1, wd1 = ffn1_gate[0].astype(BF16), ffn1_up[0].astype(BF16), ffn1_down[0].astype(BF16)
    wg2, wu2, wd2 = ffn2_gate[0].astype(BF16), ffn2_up[0].astype(BF16), ffn2_down[0].astype(BF16)
    win = w_in[0].astype(BF16)
    wo = w_out[0].astype(BF16)
    glu_w = s5_glu_w[0].astype(BF16)

    ldt = jnp.broadcast_to(s5_log_dt[0][:, None], (S5_GROUPS, S5_STATE)).reshape(1, S5_LANES)
    bre_t = s5_b_re[0].reshape(S5_LANES, S5_GROUP).T
    bim_t = s5_b_im[0].reshape(S5_LANES, S5_GROUP).T
    pre, pim, bbr, bbi = _s5_prep(s5_lambda_re[0].reshape(1, S5_LANES), s5_lambda_im[0].reshape(1, S5_LANES),
                                  ldt, bre_t, bim_t)
    bmat, cmat = _s5_block_mats(bbr, bbi, s5_c_re[0], s5_c_im[0])
    s5_consts = (pre, pim, bmat, cmat, _row(s5_d[0]), glu_w, _row(s5_glu_b[0]), _row(out_norm_s5[0]))

    gate_w = _pad_lanes(jnp.concatenate([ml_igate_w[0], ml_fgate_w[0]], axis=1), GATE_LANES).astype(BF16)
    gate_b = _pad_lanes(jnp.concatenate([ml_igate_b[0], ml_fgate_b[0]])[None, :], GATE_LANES)
    ml_consts = (ml_conv_w[0], _row(ml_conv_b[0]), _block_diag_4(ml_wq[0]), _block_diag_4(ml_wk[0]),
                 _block_diag_4(ml_wv[0]), gate_w, gate_b, _row(ml_norm_w[0]), _row(ml_skip[0]),
                 _row(out_norm_ml[0]))

    def conv_in(cv):
        return jnp.pad(cv, ((0, 0), (SUBLANES - (ML_CONV - 1), 0), (0, 0)))

    def m_in(m):
        return _pad_lanes(m, GATE_LANES)[:, None, :]

    ffn1 = (_row(norm_ffn1[0]), wg1, wu1, wd1, _row(norm_mix[0]), win)
    x1_p, u_p, xm_p, z_p = _ffn_in(x_prompt.reshape(nb * seq, D_MODEL), *ffn1, tm=512)
    small = jnp.concatenate([x_sample.reshape(ns * dseq, D_MODEL), meta_tokens], axis=0)
    x1_s, u_s, xm_s, z_s = _ffn_in(small, *ffn1, tm=small.shape[0])
    n_s = ns * dseq

    zeros_s5 = jnp.zeros((1, 2, S5_LANES), F32)
    _, s5_meta = _s5(u_s[n_s:].reshape(1, n_meta, D_MODEL), zeros_s5, *s5_consts, tm=n_meta)
    y5_p, s5_p = _s5(u_p.reshape(nb, seq, D_MODEL), jnp.broadcast_to(s5_meta, (nb, 2, S5_LANES)),
                     *s5_consts, tm=512)
    s5_s0 = jnp.stack([state_s5_re[0].reshape(ns, S5_LANES), state_s5_im[0].reshape(ns, S5_LANES)], axis=1)
    y5_s, s5_s = _s5(u_s[:n_s].reshape(ns, dseq, D_MODEL), s5_s0, *s5_consts, tm=dseq)

    zc = jnp.zeros((1, ML_HEADS, ML_HEAD_DIM, ML_HEAD_DIM), F32)
    zn = jnp.zeros((1, ML_HEADS, ML_HEAD_DIM), F32)
    zm = jnp.zeros((1, 1, GATE_LANES), F32)
    zcv = jnp.zeros((1, SUBLANES, D_MODEL), F32)
    _, c_m, n_m, m_m, cv_m = _mlstm(xm_s[n_s:].reshape(1, n_meta, D_MODEL), z_s[n_s:].reshape(1, n_meta, D_MODEL),
                                    zc, zn, zm, zcv, *ml_consts, tc=n_meta)
    bc = lambda a: jnp.broadcast_to(a, (nb,) + a.shape[1:])
    ym_p, c_p, n_p, m_p, cv_p = _mlstm(xm_p.reshape(nb, seq, D_MODEL), z_p.reshape(nb, seq, D_MODEL),
                                       bc(c_m), bc(n_m), bc(m_m), bc(cv_m), *ml_consts, tc=128)
    ym_s, c_s, n_sn, m_s, cv_s = _mlstm(xm_s[:n_s].reshape(ns, dseq, D_MODEL), z_s[:n_s].reshape(ns, dseq, D_MODEL),
                                        state_mlstm_c[0], state_mlstm_n[0], m_in(state_mlstm_m[0]),
                                        conv_in(state_mlstm_conv[0]), *ml_consts, tc=dseq)

    ffn2 = (wo, _row(norm_ffn2[0]), wg2, wu2, wd2, _row(norm_final))
    y_p = _out_ffn(x1_p, y5_p.reshape(nb * seq, D_MODEL), ym_p.reshape(nb * seq, D_MODEL), *ffn2, tm=512)
    y_s = _out_ffn(x1_s[:n_s], y5_s.reshape(n_s, D_MODEL), ym_s.reshape(n_s, D_MODEL), *ffn2, tm=n_s)

    def s5_out(s, n):
        return (s[:, 0].reshape(1, n, S5_GROUPS, S5_STATE), s[:, 1].reshape(1, n, S5_GROUPS, S5_STATE))

    p_re, p_im = s5_out(s5_p, nb)
    s_re, s_im = s5_out(s5_s, ns)
    tail = slice(SUBLANES - (ML_CONV - 1), SUBLANES)
    return (y_p.reshape(nb, seq, D_MODEL), y_s.reshape(ns, dseq, D_MODEL),
            p_re, p_im, c_p[None], n_p[None], m_p[:, 0, :ML_HEADS][None], cv_p[:, tail][None],
            s_re, s_im, c_s[None], n_sn[None], m_s[:, 0, :ML_HEADS][None], cv_s[:, tail][None])
```

```python
import functools

import jax
import jax.numpy as jnp
from jax import lax
from jax.experimental import pallas as pl
from jax.experimental.pallas import tpu as pltpu

F32 = jnp.float32
BF16 = jnp.bfloat16

D_MODEL = 1024
D_FF = 2816
S5_GROUPS = 64
S5_GROUP = 16
S5_STATE = 64
S5_LANES = S5_GROUPS * S5_STATE
S5_BLOCKS = 4
S5_BLOCK_GROUPS = S5_GROUPS // S5_BLOCKS
S5_BLOCK_IN = D_MODEL // S5_BLOCKS
S5_BLOCK_ST = S5_LANES // S5_BLOCKS
ML_HEADS = 4
ML_HEAD_DIM = 256
ML_CONV = 4
ML_QKV_BLOCK = 4
EPS = 1e-6
SUBLANES = 8
GATE_LANES = 128
POW_ROWS = 64
VMEM_LIMIT = 56 * 1024 * 1024

TOKEN_TILE = 512
ML_CHUNK = 128


def _rms(x, g):
    return x * lax.rsqrt(jnp.mean(x * x, axis=-1, keepdims=True) + EPS) * g


def _silu(x):
    return x * jax.nn.sigmoid(x)


def _swiglu(h, wg_ref, wu_ref, wd_ref, acc_ref, ff_chunk):
    for c in range(D_FF // ff_chunk):
        sl = slice(c * ff_chunk, (c + 1) * ff_chunk)
        g = jnp.dot(h, wg_ref[:, sl], preferred_element_type=F32)
        u = jnp.dot(h, wu_ref[:, sl], preferred_element_type=F32)
        a = (_silu(g) * u).astype(BF16)
        d = jnp.dot(a, wd_ref[sl, :], preferred_element_type=F32)
        if c == 0:
            acc_ref[...] = d
        else:
            acc_ref[...] += d
    return acc_ref[...]


def _const_spec(shape):
    nd = len(shape)
    return pl.BlockSpec(shape, lambda *_: (0,) * nd, pipeline_mode=pl.Buffered(1))


def _ffn_in_kernel(x_ref, g1_ref, wg_ref, wu_ref, wd_ref, g2_ref, win_ref,
                   x1_ref, u_ref, xm_ref, z_ref, acc_ref, *, ff_chunk):
    x = x_ref[...]
    h = _rms(x, g1_ref[...]).astype(BF16)
    x1 = x + 0.5 * _swiglu(h, wg_ref, wu_ref, wd_ref, acc_ref, ff_chunk)
    x1_ref[...] = x1
    h2 = _rms(x1, g2_ref[...]).astype(BF16)
    u_ref[...] = jnp.dot(h2, win_ref[:, 0:D_MODEL], preferred_element_type=F32)
    xm_ref[...] = jnp.dot(h2, win_ref[:, D_MODEL:2 * D_MODEL], preferred_element_type=F32)
    z_ref[...] = jnp.dot(h2, win_ref[:, 2 * D_MODEL:3 * D_MODEL], preferred_element_type=F32)


def _ffn_in(x, g1, wg, wu, wd, g2, win, *, tm, ff_chunk=256):
    rows = x.shape[0]
    row_spec = pl.BlockSpec((tm, D_MODEL), lambda i: (i, 0))
    out = jax.ShapeDtypeStruct((rows, D_MODEL), F32)
    return pl.pallas_call(
        functools.partial(_ffn_in_kernel, ff_chunk=ff_chunk),
        grid=(rows // tm,),
        in_specs=[row_spec, _const_spec(g1.shape), _const_spec(wg.shape), _const_spec(wu.shape),
                  _const_spec(wd.shape), _const_spec(g2.shape), _const_spec(win.shape)],
        out_specs=[row_spec] * 4,
        out_shape=[out] * 4,
        scratch_shapes=[pltpu.VMEM((tm, D_MODEL), F32)],
        compiler_params=pltpu.CompilerParams(dimension_semantics=("parallel",), vmem_limit_bytes=VMEM_LIMIT),
        name="ffn_in",
    )(x, g1, wg, wu, wd, g2, win)


def _out_ffn_kernel(x1_ref, y5_ref, ym_ref, wo_ref, g_ref, wg_ref, wu_ref, wd_ref, gf_ref,
                    o_ref, acc_ref, *, ff_chunk):
    x2 = (x1_ref[...]
          + jnp.dot(y5_ref[...], wo_ref[0:D_MODEL, :], preferred_element_type=F32)
          + jnp.dot(ym_ref[...], wo_ref[D_MODEL:2 * D_MODEL, :], preferred_element_type=F32))
    h = _rms(x2, g_ref[...]).astype(BF16)
    x3 = x2 + 0.5 * _swiglu(h, wg_ref, wu_ref, wd_ref, acc_ref, ff_chunk)
    o_ref[...] = _rms(x3, gf_ref[...])


def _out_ffn(x1, y5, ym, wo, g, wg, wu, wd, gf, *, rows, tm, ff_chunk=256):
    row_spec = pl.BlockSpec((tm, D_MODEL), lambda i: (i, 0))
    return pl.pallas_call(
        functools.partial(_out_ffn_kernel, ff_chunk=ff_chunk),
        grid=(rows // tm,),
        in_specs=[row_spec, row_spec, row_spec, _const_spec(wo.shape), _const_spec(g.shape),
                  _const_spec(wg.shape), _const_spec(wu.shape), _const_spec(wd.shape), _const_spec(gf.shape)],
        out_specs=row_spec,
        out_shape=jax.ShapeDtypeStruct((rows, D_MODEL), F32),
        scratch_shapes=[pltpu.VMEM((tm, D_MODEL), F32)],
        compiler_params=pltpu.CompilerParams(dimension_semantics=("parallel",), vmem_limit_bytes=VMEM_LIMIT),
        name="out_ffn",
    )(x1, y5, ym, wo, g, wg, wu, wd, gf)


def _s5_prep_kernel(lre_ref, lim_ref, ldt_ref, bre_ref, bim_ref, cre_ref, cim_ref,
                    pre_ref, pim_ref, bmat_ref, cmat_ref):
    lr = lre_ref[...]
    li = lim_ref[...]
    dt = jnp.exp(ldt_ref[...])
    mag = jnp.exp(lr * dt)
    th = li * dt
    ar = mag * jnp.cos(th)
    ai = mag * jnp.sin(th)
    nr = ar - 1.0
    den = lr * lr + li * li
    cr = (nr * lr + ai * li) / den
    ci = (ai * lr - nr * li) / den
    bre = bre_ref[...]
    bim = bim_ref[...]
    bbr = cr * bre - ci * bim
    bbi = cr * bim + ci * bre

    pr, pi = ar, ai
    pre_ref[0:1, :] = pr
    pim_ref[0:1, :] = pi
    for j in range(1, POW_ROWS):
        pr, pi = pr * ar - pi * ai, pr * ai + pi * ar
        pre_ref[j:j + 1, :] = pr
        pim_ref[j:j + 1, :] = pi

    g_shift, s_shift = S5_GROUP.bit_length() - 1, S5_STATE.bit_length() - 1
    b_shape = (S5_BLOCK_IN, S5_BLOCK_ST)
    b_mask = (lax.broadcasted_iota(jnp.int32, b_shape, 0) >> g_shift) == (
        lax.broadcasted_iota(jnp.int32, b_shape, 1) >> s_shift)
    c_shape = (S5_BLOCK_ST, S5_BLOCK_IN)
    c_mask = (lax.broadcasted_iota(jnp.int32, c_shape, 0) >> s_shift) == (
        lax.broadcasted_iota(jnp.int32, c_shape, 1) >> g_shift)
    cre = cre_ref[...]
    ncim = -cim_ref[...]
    for blk in range(S5_BLOCKS):
        for part, bb in enumerate((bbr, bbi)):
            x = bb[:, blk * S5_BLOCK_ST:(blk + 1) * S5_BLOCK_ST]
            x = jnp.concatenate([x] * S5_BLOCK_GROUPS, axis=0)
            bmat_ref[blk, :, part * S5_BLOCK_ST:(part + 1) * S5_BLOCK_ST] = jnp.where(b_mask, x, 0.0).astype(BF16)
        for part, cc in enumerate((cre, ncim)):
            x = cc[:, blk * S5_BLOCK_IN:(blk + 1) * S5_BLOCK_IN]
            x = jnp.concatenate([x] * S5_BLOCK_GROUPS, axis=0)
            cmat_ref[blk, part * S5_BLOCK_ST:(part + 1) * S5_BLOCK_ST, :] = jnp.where(c_mask, x, 0.0).astype(BF16)


def _s5_prep(lre, lim, ldt, bre_t, bim_t, cre_t, cim_t):
    pow_shape = jax.ShapeDtypeStruct((POW_ROWS, S5_LANES), F32)
    return pl.pallas_call(
        _s5_prep_kernel,
        out_shape=[pow_shape, pow_shape,
                   jax.ShapeDtypeStruct((S5_BLOCKS, S5_BLOCK_IN, 2 * S5_BLOCK_ST), BF16),
                   jax.ShapeDtypeStruct((S5_BLOCKS, 2 * S5_BLOCK_ST, S5_BLOCK_IN), BF16)],
        compiler_params=pltpu.CompilerParams(vmem_limit_bytes=VMEM_LIMIT),
        name="s5_prep",
    )(lre, lim, ldt, bre_t, bim_t, cre_t, cim_t)


def _ml_prep_kernel(wq_ref, wk_ref, wv_ref, oq_ref, ok_ref, ov_ref):
    rows = 128
    shift = ML_QKV_BLOCK.bit_length() - 1
    col = lax.broadcasted_iota(jnp.int32, (rows, D_MODEL), 1)
    sel = col & (ML_QKV_BLOCK - 1)
    for w_ref, o_ref in ((wq_ref, oq_ref), (wk_ref, ok_ref), (wv_ref, ov_ref)):
        for c in range(D_MODEL // rows):
            w = w_ref[c * rows:(c + 1) * rows, :]
            row = lax.broadcasted_iota(jnp.int32, (rows, D_MODEL), 0) + c * rows
            acc = jnp.zeros((rows, D_MODEL), F32)
            for o in range(ML_QKV_BLOCK):
                acc = jnp.where(sel == o, w[:, o:o + 1], acc)
            o_ref[c * rows:(c + 1) * rows, :] = jnp.where((row >> shift) == (col >> shift), acc, 0.0).astype(BF16)


def _ml_prep(wq, wk, wv):
    out = jax.ShapeDtypeStruct((D_MODEL, D_MODEL), BF16)
    return pl.pallas_call(_ml_prep_kernel, out_shape=[out] * 3, name="ml_prep")(wq, wk, wv)


def _s5_kernel(u_ref, sre0_ref, sim0_ref, pre_ref, pim_ref, bmat_ref, cmat_ref, d_ref, gw_ref, gb_ref, on_ref,
               y_ref, sre_ref, sim_ref, up_ref, s_ref, *, tm, independent):
    r = tm // SUBLANES
    t = pl.program_id(1)

    @pl.when(t == 0)
    def _():
        sre_ref[...] = sre0_ref[...]
        sim_ref[...] = sim0_ref[...]

    if r % SUBLANES == 0:
        up_ref[...] = jnp.swapaxes(u_ref[...].reshape(SUBLANES, r, D_MODEL), 0, 1).reshape(tm, D_MODEL)
    else:
        for i in range(r):
            up_ref[i * SUBLANES:(i + 1) * SUBLANES, :] = jnp.concatenate(
                [u_ref[a * r + i:a * r + i + 1, :] for a in range(SUBLANES)], axis=0)
    up = up_ref[...]
    ub = up.astype(BF16)

    ys = []
    for b in range(S5_BLOCKS):
        lanes = slice(b * S5_BLOCK_ST, (b + 1) * S5_BLOCK_ST)
        re = slice(0, S5_BLOCK_ST)
        im = slice(S5_BLOCK_ST, 2 * S5_BLOCK_ST)
        s_ref[...] = jnp.dot(ub[:, b * S5_BLOCK_IN:(b + 1) * S5_BLOCK_IN], bmat_ref[b],
                             preferred_element_type=F32)
        lr = jnp.broadcast_to(pre_ref[0:1, lanes], (SUBLANES, S5_BLOCK_ST))
        li = jnp.broadcast_to(pim_ref[0:1, lanes], (SUBLANES, S5_BLOCK_ST))

        def step(i, carry, lr=lr, li=li):
            cr, ci = carry
            rows = pl.ds(pl.multiple_of(i * SUBLANES, SUBLANES), SUBLANES)
            nr = lr * cr - li * ci + s_ref[rows, re]
            ni = lr * ci + li * cr + s_ref[rows, im]
            s_ref[rows, re] = nr
            s_ref[rows, im] = ni
            return nr, ni

        if independent:
            fr, fi = lax.fori_loop(0, r, step, (sre_ref[:, lanes], sim_ref[:, lanes]))
            sre_ref[:, lanes] = fr
            sim_ref[:, lanes] = fi
        else:
            zero = jnp.zeros((SUBLANES, S5_BLOCK_ST), F32)
            fr, fi = lax.fori_loop(0, r, step, (zero, zero))
            rr = pre_ref[r - 1:r, lanes]
            ri = pim_ref[r - 1:r, lanes]
            cr = sre_ref[:, lanes]
            ci = sim_ref[:, lanes]
            rows_r, rows_i = [], []
            for a in range(SUBLANES):
                rows_r.append(cr)
                rows_i.append(ci)
                cr, ci = rr * cr - ri * ci + fr[a:a + 1], rr * ci + ri * cr + fi[a:a + 1]
            sre_ref[:, lanes] = cr
            sim_ref[:, lanes] = ci
            cin_r = jnp.concatenate(rows_r, axis=0)
            cin_i = jnp.concatenate(rows_i, axis=0)

            def fix(i, c, cin_r=cin_r, cin_i=cin_i, lanes=lanes):
                rows = pl.ds(pl.multiple_of(i * SUBLANES, SUBLANES), SUBLANES)
                pr = pre_ref[pl.ds(i, 1), lanes]
                pi = pim_ref[pl.ds(i, 1), lanes]
                s_ref[rows, re] += pr * cin_r - pi * cin_i
                s_ref[rows, im] += pr * cin_i + pi * cin_r
                return c

            lax.fori_loop(0, r, fix, 0)
        ys.append(jnp.dot(s_ref[...].astype(BF16), cmat_ref[b], preferred_element_type=F32))

    y = jnp.concatenate(ys, axis=1) + d_ref[...] * up
    g = jax.nn.gelu(y)
    o = g * jax.nn.sigmoid(jnp.dot(g.astype(BF16), gw_ref[...], preferred_element_type=F32) + gb_ref[...])
    yp = _rms(o, on_ref[...])
    if r % SUBLANES == 0:
        yn = jnp.swapaxes(yp.reshape(r, SUBLANES, D_MODEL), 0, 1).reshape(tm, D_MODEL)
    else:
        yn = jnp.concatenate([yp[i * SUBLANES + a:i * SUBLANES + a + 1, :]
                              for a in range(SUBLANES) for i in range(r)], axis=0)
    y_ref[...] = yn.astype(BF16)


def _s5(u, sre0, sim0, consts, *, n_seq, tiles, tm, row_block0=0, independent=False, shared_init=False):
    srows = SUBLANES if independent else 1
    seq_spec = pl.BlockSpec((tm, D_MODEL), lambda i, t: (row_block0 + i * tiles + t, 0))
    out_seq_spec = pl.BlockSpec((tm, D_MODEL), lambda i, t: (i * tiles + t, 0))
    st_in = pl.BlockSpec((None, srows, S5_LANES), (lambda i, t: (0, 0, 0)) if shared_init else (lambda i, t: (i, 0, 0)))
    st_out = pl.BlockSpec((None, srows, S5_LANES), lambda i, t: (i, 0, 0))
    st_shape = jax.ShapeDtypeStruct((n_seq, srows, S5_LANES), F32)
    return pl.pallas_call(
        functools.partial(_s5_kernel, tm=tm, independent=independent),
        grid=(n_seq, tiles),
        in_specs=[seq_spec, st_in, st_in] + [_const_spec(a.shape) for a in consts],
        out_specs=[out_seq_spec, st_out, st_out],
        out_shape=[jax.ShapeDtypeStruct((n_seq * tiles * tm, D_MODEL), BF16), st_shape, st_shape],
        scratch_shapes=[pltpu.VMEM((tm, D_MODEL), F32), pltpu.VMEM((tm, 2 * S5_BLOCK_ST), F32)],
        compiler_params=pltpu.CompilerParams(dimension_semantics=("parallel", "arbitrary"),
                                             vmem_limit_bytes=VMEM_LIMIT),
        name="s5_mix",
    )(u, sre0, sim0, *consts)


def _mlstm_kernel(xm_ref, z_ref, c0_ref, n0_ref, m0_ref, cv0_ref, cw_ref, cb_ref, bdq_ref, bdk_ref, bdv_ref,
                  gw_ref, gb_ref, nw_ref, sk_ref, on_ref,
                  y_ref, c_ref, n_ref, m_ref, cv_ref,
                  xbuf_ref, xc_ref, q_ref, k_ref, v_ref, vf_ref, g_ref, h_ref, *, tm, tc):
    t = pl.program_id(1)
    pre = ML_CONV - 1

    @pl.when(t == 0)
    def _():
        c_ref[...] = c0_ref[...]
        n_ref[...] = n0_ref[...]
        m_ref[...] = m0_ref[...]
        cv_ref[...] = cv0_ref[...]

    xm = xm_ref[...]
    xbuf_ref[SUBLANES - pre:SUBLANES, :] = cv_ref[...]
    xbuf_ref[SUBLANES:SUBLANES + tm, :] = xm
    cw = cw_ref[...]
    xc = cb_ref[...] + cw[pre:pre + 1] * xm
    for j in range(pre):
        xc = xc + cw[j:j + 1] * xbuf_ref[SUBLANES - pre + j:SUBLANES - pre + j + tm, :]
    cv_ref[...] = xbuf_ref[SUBLANES + tm - pre:SUBLANES + tm, :]
    xc = _silu(xc)
    xc_ref[...] = xc

    xcb = xc.astype(BF16)
    q = jnp.dot(xcb, bdq_ref[...], preferred_element_type=F32)
    k = jnp.dot(xcb, bdk_ref[...], preferred_element_type=F32)
    v = jnp.dot(xm.astype(BF16), bdv_ref[...], preferred_element_type=F32)
    qb = q.astype(BF16)
    vb = v.astype(BF16)
    g_ref[...] = jnp.dot(jnp.concatenate([qb, k.astype(BF16), vb], axis=1), gw_ref[...],
                         preferred_element_type=F32) + gb_ref[...]
    q_ref[...] = qb
    k_ref[...] = (k * (ML_HEAD_DIM ** -0.5)).astype(BF16)
    v_ref[...] = vb
    vf_ref[...] = v

    lane = lax.broadcasted_iota(jnp.int32, (tc, GATE_LANES), 1)
    row = lax.broadcasted_iota(jnp.int32, (tc, tc), 0)
    col = lax.broadcasted_iota(jnp.int32, (tc, tc), 1)
    causal = row >= col
    tril = causal.astype(F32)

    def chunk(rows):
        gates = g_ref[rows, :]
        lf = jnp.minimum(gates, 0.0) - jnp.log1p(jnp.exp(-jnp.abs(gates)))
        lf = jnp.where((lane >= ML_HEADS) & (lane < 2 * ML_HEADS), lf, 0.0)
        cum = jnp.dot(tril, lf, preferred_element_type=F32, precision=lax.Precision.HIGHEST)
        arr = jnp.where(lane < ML_HEADS, gates, cum)
        arr_t = arr.T
        for h in range(ML_HEADS):
            hs = slice(h * ML_HEAD_DIM, (h + 1) * ML_HEAD_DIM)
            qh = q_ref[rows, hs]
            kh = k_ref[rows, hs]
            vh = v_ref[rows, hs]
            ig_col = arr[:, h:h + 1]
            b_col = arr[:, ML_HEADS + h:ML_HEADS + h + 1]
            ig_row = arr_t[h:h + 1, :]
            b_row = arr_t[ML_HEADS + h:ML_HEADS + h + 1, :]
            m_prev = m_ref[:, h:h + 1]
            c_prev = c_ref[h]
            n_prev = n_ref[h:h + 1, :]

            logw = jnp.where(causal, b_col - b_row + ig_row, -jnp.inf)
            log_inter = b_col + m_prev
            m_t = jnp.maximum(log_inter, jnp.max(logw, axis=-1, keepdims=True))
            w = jnp.exp(logw - m_t)
            a_inter = jnp.exp(log_inter - m_t)
            s = lax.dot_general(qh, kh, (((1,), (1,)), ((), ())), preferred_element_type=F32) * w
            inter = lax.dot_general(qh, c_prev.astype(BF16), (((1,), (1,)), ((), ())),
                                    preferred_element_type=F32)
            num = a_inter * inter + jnp.dot(s.astype(BF16), vh, preferred_element_type=F32)
            den = (a_inter * jnp.sum(qh.astype(F32) * n_prev, axis=-1, keepdims=True)
                   + jnp.sum(s, axis=-1, keepdims=True))
            hh = num / jnp.maximum(jnp.abs(den), jnp.exp(-m_t))
            mu = jnp.mean(hh, axis=-1, keepdims=True)
            hc = hh - mu
            var = jnp.mean(hc * hc, axis=-1, keepdims=True)
            h_ref[rows, hs] = hc * lax.rsqrt(var + EPS)

            b_last = b_col[tc - 1:tc, :]
            m_new = m_t[tc - 1:tc, :]
            g_state = jnp.exp(b_last + m_prev - m_new)
            g_src = jnp.exp(b_last - b_col + ig_col - m_new)
            vs = (vf_ref[rows, hs] * g_src).astype(BF16)
            c_ref[h] = g_state * c_prev + lax.dot_general(vs, kh, (((0,), (0,)), ((), ())),
                                                          preferred_element_type=F32)
            n_ref[h:h + 1, :] = g_state * n_prev + jnp.sum(g_src * kh.astype(F32), axis=0, keepdims=True)
            m_ref[:, h:h + 1] = m_new

    if tm == tc:
        chunk(slice(0, tc))
    else:
        def body(j, c):
            chunk(pl.ds(pl.multiple_of(j * tc, tc), tc))
            return c

        lax.fori_loop(0, tm // tc, body, 0)

    out = (h_ref[...] * nw_ref[...] + sk_ref[...] * xc_ref[...]) * _silu(z_ref[...])
    y_ref[...] = _rms(out, on_ref[...]).astype(BF16)


def _mlstm(xm, z, c0, n0, m0, cv0, consts, *, n_seq, tiles, tm, tc, row_block0=0, shared_init=False):
    seq_spec = pl.BlockSpec((tm, D_MODEL), lambda i, t: (row_block0 + i * tiles + t, 0))
    out_seq_spec = pl.BlockSpec((tm, D_MODEL), lambda i, t: (i * tiles + t, 0))

    def st(shape, shared):
        nd = len(shape)
        return pl.BlockSpec((None,) + shape,
                            (lambda i, t: (0,) * (nd + 1)) if shared else (lambda i, t: (i,) + (0,) * nd))

    st_shapes = ((ML_HEADS, ML_HEAD_DIM, ML_HEAD_DIM), (ML_HEADS, ML_HEAD_DIM), (1, ML_HEADS),
                 (ML_CONV - 1, D_MODEL))
    return pl.pallas_call(
        functools.partial(_mlstm_kernel, tm=tm, tc=tc),
        grid=(n_seq, tiles),
        in_specs=[seq_spec, seq_spec] + [st(s, shared_init) for s in st_shapes]
                 + [_const_spec(a.shape) for a in consts],
        out_specs=[out_seq_spec] + [st(s, False) for s in st_shapes],
        out_shape=[jax.ShapeDtypeStruct((n_seq * tiles * tm, D_MODEL), BF16)]
                  + [jax.ShapeDtypeStruct((n_seq,) + s, F32) for s in st_shapes],
        scratch_shapes=[pltpu.VMEM((tm + 2 * SUBLANES, D_MODEL), F32), pltpu.VMEM((tm, D_MODEL), F32),
                        pltpu.VMEM((tm, D_MODEL), BF16), pltpu.VMEM((tm, D_MODEL), BF16),
                        pltpu.VMEM((tm, D_MODEL), BF16), pltpu.VMEM((tm, D_MODEL), F32),
                        pltpu.VMEM((tm, GATE_LANES), F32), pltpu.VMEM((tm, D_MODEL), F32)],
        compiler_params=pltpu.CompilerParams(dimension_semantics=("parallel", "arbitrary"),
                                             vmem_limit_bytes=VMEM_LIMIT),
        name="mlstm_mix",
    )(xm, z, c0, n0, m0, cv0, *consts)


def _row(v):
    return v.reshape(1, -1).astype(F32)


def _pad_lanes(v, width):
    return jnp.pad(v, [(0, 0)] * (v.ndim - 1) + [(0, width - v.shape[-1])])


def kernel(x_prompt, x_sample, state_s5_re, state_s5_im, state_mlstm_c, state_mlstm_n, state_mlstm_m,
           state_mlstm_conv, meta_tokens, norm_ffn1, ffn1_gate, ffn1_up, ffn1_down, norm_mix, w_in,
           s5_lambda_re, s5_lambda_im, s5_log_dt, s5_b_re, s5_b_im, s5_c_re, s5_c_im, s5_d, s5_glu_w, s5_glu_b,
           ml_conv_w, ml_conv_b, ml_wq, ml_wk, ml_wv, ml_igate_w, ml_igate_b, ml_fgate_w, ml_fgate_b,
           ml_norm_w, ml_skip, out_norm_s5, out_norm_ml, w_out, norm_ffn2, ffn2_gate, ffn2_up, ffn2_down,
           norm_final):
    nb, seq, _ = x_prompt.shape
    ns, dseq, _ = x_sample.shape
    n_meta = meta_tokens.shape[0]
    n_p, n_s = nb * seq, ns * dseq
    tile_p = min(TOKEN_TILE, seq)
    chunk_p = min(ML_CHUNK, tile_p)

    wg1, wu1, wd1 = ffn1_gate[0].astype(BF16), ffn1_up[0].astype(BF16), ffn1_down[0].astype(BF16)
    wg2, wu2, wd2 = ffn2_gate[0].astype(BF16), ffn2_up[0].astype(BF16), ffn2_down[0].astype(BF16)
    win = w_in[0].astype(BF16)
    wo = w_out[0].astype(BF16)
    glu_w = s5_glu_w[0].astype(BF16)

    ldt = jnp.broadcast_to(s5_log_dt[0][:, None], (S5_GROUPS, S5_STATE)).reshape(1, S5_LANES)
    pre, pim, bmat, cmat = _s5_prep(
        s5_lambda_re[0].reshape(1, S5_LANES), s5_lambda_im[0].reshape(1, S5_LANES), ldt,
        s5_b_re[0].reshape(S5_LANES, S5_GROUP).T, s5_b_im[0].reshape(S5_LANES, S5_GROUP).T,
        s5_c_re[0].reshape(D_MODEL, S5_STATE).T, s5_c_im[0].reshape(D_MODEL, S5_STATE).T)
    s5_consts = (pre, pim, bmat, cmat, _row(s5_d[0]), glu_w, _row(s5_glu_b[0]), _row(out_norm_s5[0]))

    bdq, bdk, bdv = _ml_prep(ml_wq[0].reshape(D_MODEL, ML_QKV_BLOCK), ml_wk[0].reshape(D_MODEL, ML_QKV_BLOCK),
                             ml_wv[0].reshape(D_MODEL, ML_QKV_BLOCK))
    gate_w = _pad_lanes(jnp.concatenate([ml_igate_w[0], ml_fgate_w[0]], axis=1), GATE_LANES).astype(BF16)
    gate_b = _pad_lanes(jnp.concatenate([ml_igate_b[0], ml_fgate_b[0]])[None, :], GATE_LANES)
    ml_consts = (ml_conv_w[0], _row(ml_conv_b[0]), bdq, bdk, bdv, gate_w, gate_b, _row(ml_norm_w[0]),
                 _row(ml_skip[0]), _row(out_norm_ml[0]))

    ffn1 = (_row(norm_ffn1[0]), wg1, wu1, wd1, _row(norm_mix[0]), win)
    x1_p, u_p, xm_p, z_p = _ffn_in(x_prompt.reshape(n_p, D_MODEL), *ffn1, tm=tile_p)
    small = jnp.concatenate([x_sample.reshape(n_s, D_MODEL), meta_tokens], axis=0)
    x1_s, u_s, xm_s, z_s = _ffn_in(small, *ffn1, tm=small.shape[0])

    zs5 = jnp.zeros((1, 1, S5_LANES), F32)
    _, mre, mim = _s5(u_s, zs5, zs5, s5_consts, n_seq=1, tiles=1, tm=n_meta, row_block0=n_s // n_meta)
    y5_p, pre_s, pim_s = _s5(u_p, mre, mim, s5_consts, n_seq=nb, tiles=seq // tile_p, tm=tile_p,
                             shared_init=True)
    y5_s, sre_s, sim_s = _s5(u_s, state_s5_re[0].reshape(ns // SUBLANES, SUBLANES, S5_LANES),
                             state_s5_im[0].reshape(ns // SUBLANES, SUBLANES, S5_LANES), s5_consts,
                             n_seq=ns // SUBLANES, tiles=1, tm=SUBLANES * dseq, independent=True)

    zc = jnp.zeros((1, ML_HEADS, ML_HEAD_DIM, ML_HEAD_DIM), F32)
    zn = jnp.zeros((1, ML_HEADS, ML_HEAD_DIM), F32)
    zm = jnp.zeros((1, 1, ML_HEADS), F32)
    zcv = jnp.zeros((1, ML_CONV - 1, D_MODEL), F32)
    _, c_m, n_m, m_m, cv_m = _mlstm(xm_s, z_s, zc, zn, zm, zcv, ml_consts, n_seq=1, tiles=1, tm=n_meta,
                                    tc=n_meta, row_block0=n_s // n_meta)
    ym_p, c_p, nn_p, m_p, cv_p = _mlstm(xm_p, z_p, c_m, n_m, m_m, cv_m, ml_consts, n_seq=nb,
                                        tiles=seq // tile_p, tm=tile_p, tc=chunk_p, shared_init=True)
    ym_s, c_s, nn_s, m_s, cv_s = _mlstm(xm_s, z_s, state_mlstm_c[0], state_mlstm_n[0],
                                        state_mlstm_m[0][:, None, :], state_mlstm_conv[0], ml_consts,
                                        n_seq=ns, tiles=1, tm=dseq, tc=dseq)

    ffn2 = (wo, _row(norm_ffn2[0]), wg2, wu2, wd2, _row(norm_final))
    y_p = _out_ffn(x1_p, y5_p, ym_p, *ffn2, rows=n_p, tm=tile_p)
    y_s = _out_ffn(x1_s, y5_s, ym_s, *ffn2, rows=n_s, tm=n_s)

    def s5_state(s, n):
        return s.reshape(1, n, S5_GROUPS, S5_STATE)

    return (y_p.reshape(nb, seq, D_MODEL), y_s.reshape(ns, dseq, D_MODEL),
            s5_state(pre_s, nb), s5_state(pim_s, nb), c_p[None], nn_p[None], m_p[:, 0][None], cv_p[None],
            s5_state(sre_s, ns), s5_state(sim_s, ns), c_s[None], nn_s[None], m_s[:, 0][None], cv_s[None])
```

```python
import functools

import jax
import jax.numpy as jnp
from jax import lax
from jax.experimental import pallas as pl
from jax.experimental.pallas import tpu as pltpu

F32 = jnp.float32
BF16 = jnp.bfloat16

D_MODEL = 1024
D_FF = 2816
S5_GROUPS = 64
S5_GROUP = 16
S5_STATE = 64
S5_LANES = S5_GROUPS * S5_STATE
S5_BLOCKS = 4
S5_BLOCK_GROUPS = S5_GROUPS // S5_BLOCKS
S5_BLOCK_IN = D_MODEL // S5_BLOCKS
S5_BLOCK_ST = S5_LANES // S5_BLOCKS
ML_HEADS = 4
ML_HEAD_DIM = 256
ML_CONV = 4
ML_QKV_BLOCK = 4
EPS = 1e-6
SUBLANES = 8
GATE_LANES = 128
POW_ROWS = 64
VMEM_LIMIT = 56 * 1024 * 1024

TOKEN_TILE = 512
ML_CHUNK = 128


def _rms(x, g):
    return x * lax.rsqrt(jnp.mean(x * x, axis=-1, keepdims=True) + EPS) * g


def _silu(x):
    return x * jax.nn.sigmoid(x)


def _swiglu(h, wg_ref, wu_ref, wd_ref, acc_ref, ff_chunk):
    for c in range(D_FF // ff_chunk):
        sl = slice(c * ff_chunk, (c + 1) * ff_chunk)
        g = jnp.dot(h, wg_ref[:, sl], preferred_element_type=F32)
        u = jnp.dot(h, wu_ref[:, sl], preferred_element_type=F32)
        a = (_silu(g) * u).astype(BF16)
        d = jnp.dot(a, wd_ref[sl, :], preferred_element_type=F32)
        if c == 0:
            acc_ref[...] = d
        else:
            acc_ref[...] += d
    return acc_ref[...]


def _const_spec(shape):
    nd = len(shape)
    return pl.BlockSpec(shape, lambda *_: (0,) * nd, pipeline_mode=pl.Buffered(1))


def _segment_rows(x, layout, inverse=False):
    parts = []
    for row0, r in layout:
        n = SUBLANES * r
        g = x[row0:row0 + n]
        if r % SUBLANES == 0:
            shape = (r, SUBLANES) if inverse else (SUBLANES, r)
            g = jnp.swapaxes(g.reshape(shape + g.shape[1:]), 0, 1).reshape(g.shape)
        elif inverse:
            g = jnp.concatenate([g[i * SUBLANES + a:i * SUBLANES + a + 1]
                                 for a in range(SUBLANES) for i in range(r)], axis=0)
        else:
            g = jnp.concatenate([g[a * r + i:a * r + i + 1]
                                 for i in range(r) for a in range(SUBLANES)], axis=0)
        parts.append(g)
    return parts[0] if len(parts) == 1 else jnp.concatenate(parts, axis=0)


def _ffn_in_kernel(x_ref, g1_ref, wg_ref, wu_ref, wd_ref, g2_ref, win_ref,
                   x1_ref, u_ref, xm_ref, z_ref, acc_ref, *, ff_chunk, seg_layout):
    x = x_ref[...]
    h = _rms(x, g1_ref[...]).astype(BF16)
    x1 = x + 0.5 * _swiglu(h, wg_ref, wu_ref, wd_ref, acc_ref, ff_chunk)
    x1_ref[...] = x1
    h2f = _rms(x1, g2_ref[...])
    h2 = h2f.astype(BF16)
    h2s = _segment_rows(h2f, seg_layout).astype(BF16)
    u_ref[...] = jnp.dot(h2s, win_ref[:, 0:D_MODEL], preferred_element_type=F32)
    xm_ref[...] = jnp.dot(h2, win_ref[:, D_MODEL:2 * D_MODEL], preferred_element_type=F32)
    z_ref[...] = jnp.dot(h2, win_ref[:, 2 * D_MODEL:3 * D_MODEL], preferred_element_type=F32)


def _ffn_in(x, g1, wg, wu, wd, g2, win, *, tm, seg_layout, ff_chunk=256):
    rows = x.shape[0]
    row_spec = pl.BlockSpec((tm, D_MODEL), lambda i: (i, 0))
    out = jax.ShapeDtypeStruct((rows, D_MODEL), F32)
    return pl.pallas_call(
        functools.partial(_ffn_in_kernel, ff_chunk=ff_chunk, seg_layout=seg_layout),
        grid=(rows // tm,),
        in_specs=[row_spec, _const_spec(g1.shape), _const_spec(wg.shape), _const_spec(wu.shape),
                  _const_spec(wd.shape), _const_spec(g2.shape), _const_spec(win.shape)],
        out_specs=[row_spec] * 4,
        out_shape=[out] * 4,
        scratch_shapes=[pltpu.VMEM((tm, D_MODEL), F32)],
        compiler_params=pltpu.CompilerParams(dimension_semantics=("parallel",), vmem_limit_bytes=VMEM_LIMIT),
        name="ffn_in",
    )(x, g1, wg, wu, wd, g2, win)


def _out_ffn_kernel(x1_ref, y5_ref, ym_ref, wo_ref, g_ref, wg_ref, wu_ref, wd_ref, gf_ref,
                    o_ref, acc_ref, *, ff_chunk, seg_layout):
    p5 = jnp.dot(y5_ref[...], wo_ref[0:D_MODEL, :], preferred_element_type=F32)
    x2 = (x1_ref[...] + _segment_rows(p5, seg_layout, inverse=True)
          + jnp.dot(ym_ref[...], wo_ref[D_MODEL:2 * D_MODEL, :], preferred_element_type=F32))
    h = _rms(x2, g_ref[...]).astype(BF16)
    x3 = x2 + 0.5 * _swiglu(h, wg_ref, wu_ref, wd_ref, acc_ref, ff_chunk)
    o_ref[...] = _rms(x3, gf_ref[...])


def _out_ffn(x1, y5, ym, wo, g, wg, wu, wd, gf, *, rows, tm, seg_layout, ff_chunk=256):
    row_spec = pl.BlockSpec((tm, D_MODEL), lambda i: (i, 0))
    return pl.pallas_call(
        functools.partial(_out_ffn_kernel, ff_chunk=ff_chunk, seg_layout=seg_layout),
        grid=(rows // tm,),
        in_specs=[row_spec, row_spec, row_spec, _const_spec(wo.shape), _const_spec(g.shape),
                  _const_spec(wg.shape), _const_spec(wu.shape), _const_spec(wd.shape), _const_spec(gf.shape)],
        out_specs=row_spec,
        out_shape=jax.ShapeDtypeStruct((rows, D_MODEL), F32),
        scratch_shapes=[pltpu.VMEM((tm, D_MODEL), F32)],
        compiler_params=pltpu.CompilerParams(dimension_semantics=("parallel",), vmem_limit_bytes=VMEM_LIMIT),
        name="out_ffn",
    )(x1, y5, ym, wo, g, wg, wu, wd, gf)


def _s5_prep_kernel(lre_ref, lim_ref, ldt_ref, bre_ref, bim_ref, cre_ref, cim_ref,
                    pre_ref, pim_ref, bmat_ref, cmat_ref):
    lr = lre_ref[...]
    li = lim_ref[...]
    dt = jnp.exp(ldt_ref[...])
    mag = jnp.exp(lr * dt)
    th = li * dt
    ar = mag * jnp.cos(th)
    ai = mag * jnp.sin(th)
    nr = ar - 1.0
    den = lr * lr + li * li
    cr = (nr * lr + ai * li) / den
    ci = (ai * lr - nr * li) / den
    bre = bre_ref[...]
    bim = bim_ref[...]
    bbr = cr * bre - ci * bim
    bbi = cr * bim + ci * bre

    pr, pi = ar, ai
    pre_ref[0:1, :] = pr
    pim_ref[0:1, :] = pi
    for j in range(1, POW_ROWS):
        pr, pi = pr * ar - pi * ai, pr * ai + pi * ar
        pre_ref[j:j + 1, :] = pr
        pim_ref[j:j + 1, :] = pi

    g_shift, s_shift = S5_GROUP.bit_length() - 1, S5_STATE.bit_length() - 1
    b_shape = (S5_BLOCK_IN, S5_BLOCK_ST)
    b_mask = (lax.broadcasted_iota(jnp.int32, b_shape, 0) >> g_shift) == (
        lax.broadcasted_iota(jnp.int32, b_shape, 1) >> s_shift)
    c_shape = (S5_BLOCK_ST, S5_BLOCK_IN)
    c_mask = (lax.broadcasted_iota(jnp.int32, c_shape, 0) >> s_shift) == (
        lax.broadcasted_iota(jnp.int32, c_shape, 1) >> g_shift)
    cre = cre_ref[...]
    ncim = -cim_ref[...]
    for blk in range(S5_BLOCKS):
        for part, bb in enumerate((bbr, bbi)):
            x = bb[:, blk * S5_BLOCK_ST:(blk + 1) * S5_BLOCK_ST]
            x = jnp.concatenate([x] * S5_BLOCK_GROUPS, axis=0)
            bmat_ref[blk, :, part * S5_BLOCK_ST:(part + 1) * S5_BLOCK_ST] = jnp.where(b_mask, x, 0.0).astype(BF16)
        for part, cc in enumerate((cre, ncim)):
            x = cc[:, blk * S5_BLOCK_IN:(blk + 1) * S5_BLOCK_IN]
            x = jnp.concatenate([x] * S5_BLOCK_GROUPS, axis=0)
            cmat_ref[blk, part * S5_BLOCK_ST:(part + 1) * S5_BLOCK_ST, :] = jnp.where(c_mask, x, 0.0).astype(BF16)


def _s5_prep(lre, lim, ldt, bre_t, bim_t, cre_t, cim_t):
    pow_shape = jax.ShapeDtypeStruct((POW_ROWS, S5_LANES), F32)
    return pl.pallas_call(
        _s5_prep_kernel,
        out_shape=[pow_shape, pow_shape,
                   jax.ShapeDtypeStruct((S5_BLOCKS, S5_BLOCK_IN, 2 * S5_BLOCK_ST), BF16),
                   jax.ShapeDtypeStruct((S5_BLOCKS, 2 * S5_BLOCK_ST, S5_BLOCK_IN), BF16)],
        compiler_params=pltpu.CompilerParams(vmem_limit_bytes=VMEM_LIMIT),
        name="s5_prep",
    )(lre, lim, ldt, bre_t, bim_t, cre_t, cim_t)


def _ml_prep_kernel(wq_ref, wk_ref, wv_ref, oq_ref, ok_ref, ov_ref):
    rows = 128
    shift = ML_QKV_BLOCK.bit_length() - 1
    col = lax.broadcasted_iota(jnp.int32, (rows, D_MODEL), 1)
    sel = col & (ML_QKV_BLOCK - 1)
    for w_ref, o_ref in ((wq_ref, oq_ref), (wk_ref, ok_ref), (wv_ref, ov_ref)):
        for c in range(D_MODEL // rows):
            w = w_ref[c * rows:(c + 1) * rows, :]
            row = lax.broadcasted_iota(jnp.int32, (rows, D_MODEL), 0) + c * rows
            acc = jnp.zeros((rows, D_MODEL), F32)
            for o in range(ML_QKV_BLOCK):
                acc = jnp.where(sel == o, w[:, o:o + 1], acc)
            o_ref[c * rows:(c + 1) * rows, :] = jnp.where((row >> shift) == (col >> shift), acc, 0.0).astype(BF16)


def _ml_prep(wq, wk, wv):
    out = jax.ShapeDtypeStruct((D_MODEL, D_MODEL), BF16)
    return pl.pallas_call(_ml_prep_kernel, out_shape=[out] * 3, name="ml_prep")(wq, wk, wv)


def _s5_kernel(u_ref, sre0_ref, sim0_ref, pre_ref, pim_ref, bmat_ref, cmat_ref, d_ref, gw_ref, gb_ref, on_ref,
               y_ref, sre_ref, sim_ref, s_ref, *, tm, independent):
    r = tm // SUBLANES
    t = pl.program_id(1)

    @pl.when(t == 0)
    def _():
        sre_ref[...] = sre0_ref[...]
        sim_ref[...] = sim0_ref[...]

    up = u_ref[...]
    ub = up.astype(BF16)

    ys = []
    for b in range(S5_BLOCKS):
        lanes = slice(b * S5_BLOCK_ST, (b + 1) * S5_BLOCK_ST)
        re = slice(0, S5_BLOCK_ST)
        im = slice(S5_BLOCK_ST, 2 * S5_BLOCK_ST)
        s_ref[...] = jnp.dot(ub[:, b * S5_BLOCK_IN:(b + 1) * S5_BLOCK_IN], bmat_ref[b],
                             preferred_element_type=F32)
        lr = jnp.broadcast_to(pre_ref[0:1, lanes], (SUBLANES, S5_BLOCK_ST))
        li = jnp.broadcast_to(pim_ref[0:1, lanes], (SUBLANES, S5_BLOCK_ST))

        def scan(cr, ci):
            for i in range(r):
                rows = slice(i * SUBLANES, (i + 1) * SUBLANES)
                cr, ci = lr * cr - li * ci + s_ref[rows, re], lr * ci + li * cr + s_ref[rows, im]
                s_ref[rows, re] = cr
                s_ref[rows, im] = ci
            return cr, ci

        if independent:
            fr, fi = scan(sre_ref[:, lanes], sim_ref[:, lanes])
            sre_ref[:, lanes] = fr
            sim_ref[:, lanes] = fi
        else:
            zero = jnp.zeros((SUBLANES, S5_BLOCK_ST), F32)
            fr, fi = scan(zero, zero)
            rr = pre_ref[r - 1:r, lanes]
            ri = pim_ref[r - 1:r, lanes]
            cr = sre_ref[:, lanes]
            ci = sim_ref[:, lanes]
            rows_r, rows_i = [], []
            for a in range(SUBLANES):
                rows_r.append(cr)
                rows_i.append(ci)
                cr, ci = rr * cr - ri * ci + fr[a:a + 1], rr * ci + ri * cr + fi[a:a + 1]
            sre_ref[:, lanes] = cr
            sim_ref[:, lanes] = ci
            cin_r = jnp.concatenate(rows_r, axis=0)
            cin_i = jnp.concatenate(rows_i, axis=0)
            for i in range(r):
                rows = slice(i * SUBLANES, (i + 1) * SUBLANES)
                pr = pre_ref[i:i + 1, lanes]
                pi = pim_ref[i:i + 1, lanes]
                s_ref[rows, re] += pr * cin_r - pi * cin_i
                s_ref[rows, im] += pr * cin_i + pi * cin_r
        ys.append(jnp.dot(s_ref[...].astype(BF16), cmat_ref[b], preferred_element_type=F32))

    y = jnp.concatenate(ys, axis=1) + d_ref[...] * up
    g = jax.nn.gelu(y)
    o = g * jax.nn.sigmoid(jnp.dot(g.astype(BF16), gw_ref[...], preferred_element_type=F32) + gb_ref[...])
    y_ref[...] = _rms(o, on_ref[...]).astype(BF16)


def _s5(u, sre0, sim0, consts, *, n_seq, tiles, tm, row_block0=0, independent=False, shared_init=False):
    srows = SUBLANES if independent else 1
    seq_spec = pl.BlockSpec((tm, D_MODEL), lambda i, t: (row_block0 + i * tiles + t, 0))
    out_seq_spec = pl.BlockSpec((tm, D_MODEL), lambda i, t: (i * tiles + t, 0))
    st_in = pl.BlockSpec((None, srows, S5_LANES), (lambda i, t: (0, 0, 0)) if shared_init else (lambda i, t: (i, 0, 0)))
    st_out = pl.BlockSpec((None, srows, S5_LANES), lambda i, t: (i, 0, 0))
    st_shape = jax.ShapeDtypeStruct((n_seq, srows, S5_LANES), F32)
    return pl.pallas_call(
        functools.partial(_s5_kernel, tm=tm, independent=independent),
        grid=(n_seq, tiles),
        in_specs=[seq_spec, st_in, st_in] + [_const_spec(a.shape) for a in consts],
        out_specs=[out_seq_spec, st_out, st_out],
        out_shape=[jax.ShapeDtypeStruct((n_seq * tiles * tm, D_MODEL), BF16), st_shape, st_shape],
        scratch_shapes=[pltpu.VMEM((tm, 2 * S5_BLOCK_ST), F32)],
        compiler_params=pltpu.CompilerParams(dimension_semantics=("parallel", "arbitrary"),
                                             vmem_limit_bytes=VMEM_LIMIT),
        name="s5_mix",
    )(u, sre0, sim0, *consts)


def _mlstm_kernel(xm_ref, z_ref, c0_ref, n0_ref, m0_ref, cv0_ref, cw_ref, cb_ref, bdq_ref, bdk_ref, bdv_ref,
                  gw_ref, gb_ref, nw_ref, sk_ref, on_ref,
                  y_ref, c_ref, n_ref, m_ref, cv_ref,
                  xbuf_ref, xc_ref, q_ref, k_ref, v_ref, vf_ref, g_ref, h_ref, *, tm, tc):
    t = pl.program_id(1)
    pre = ML_CONV - 1

    @pl.when(t == 0)
    def _():
        c_ref[...] = c0_ref[...]
        n_ref[...] = n0_ref[...]
        m_ref[...] = m0_ref[...]
        cv_ref[...] = cv0_ref[...]

    xm = xm_ref[...]
    xbuf_ref[SUBLANES - pre:SUBLANES, :] = cv_ref[...]
    xbuf_ref[SUBLANES:SUBLANES + tm, :] = xm
    cw = cw_ref[...]
    xc = cb_ref[...] + cw[pre:pre + 1] * xm
    for j in range(pre):
        xc = xc + cw[j:j + 1] * xbuf_ref[SUBLANES - pre + j:SUBLANES - pre + j + tm, :]
    cv_ref[...] = xbuf_ref[SUBLANES + tm - pre:SUBLANES + tm, :]
    xc = _silu(xc)
    xc_ref[...] = xc

    xcb = xc.astype(BF16)
    q = jnp.dot(xcb, bdq_ref[...], preferred_element_type=F32)
    k = jnp.dot(xcb, bdk_ref[...], preferred_element_type=F32)
    v = jnp.dot(xm.astype(BF16), bdv_ref[...], preferred_element_type=F32)
    qb = q.astype(BF16)
    vb = v.astype(BF16)
    g_ref[...] = jnp.dot(jnp.concatenate([qb, k.astype(BF16), vb], axis=1), gw_ref[...],
                         preferred_element_type=F32) + gb_ref[...]
    q_ref[...] = qb
    k_ref[...] = (k * (ML_HEAD_DIM ** -0.5)).astype(BF16)
    v_ref[...] = vb
    vf_ref[...] = v

    lane = lax.broadcasted_iota(jnp.int32, (tc, GATE_LANES), 1)
    row = lax.broadcasted_iota(jnp.int32, (tc, tc), 0)
    col = lax.broadcasted_iota(jnp.int32, (tc, tc), 1)
    causal = row >= col
    tril = causal.astype(F32)

    def chunk(rows):
        gates = g_ref[rows, :]
        lf = jnp.minimum(gates, 0.0) - jnp.log1p(jnp.exp(-jnp.abs(gates)))
        lf = jnp.where((lane >= ML_HEADS) & (lane < 2 * ML_HEADS), lf, 0.0)
        cum = jnp.dot(tril, lf, preferred_element_type=F32, precision=lax.Precision.HIGHEST)
        arr = jnp.where(lane < ML_HEADS, gates, cum)
        arr_t = arr.T
        for h in range(ML_HEADS):
            hs = slice(h * ML_HEAD_DIM, (h + 1) * ML_HEAD_DIM)
            qh = q_ref[rows, hs]
            kh = k_ref[rows, hs]
            vh = v_ref[rows, hs]
            ig_col = arr[:, h:h + 1]
            b_col = arr[:, ML_HEADS + h:ML_HEADS + h + 1]
            ig_row = arr_t[h:h + 1, :]
            b_row = arr_t[ML_HEADS + h:ML_HEADS + h + 1, :]
            m_prev = m_ref[:, h:h + 1]
            c_prev = c_ref[h]
            n_prev = n_ref[h:h + 1, :]

            logw = jnp.where(causal, b_col - b_row + ig_row, -jnp.inf)
            log_inter = b_col + m_prev
            m_t = jnp.maximum(log_inter, jnp.max(logw, axis=-1, keepdims=True))
            w = jnp.exp(logw - m_t)
            a_inter = jnp.exp(log_inter - m_t)
            s = lax.dot_general(qh, kh, (((1,), (1,)), ((), ())), preferred_element_type=F32) * w
            inter = lax.dot_general(qh, c_prev.astype(BF16), (((1,), (1,)), ((), ())),
                                    preferred_element_type=F32)
            num = a_inter * inter + jnp.dot(s.astype(BF16), vh, preferred_element_type=F32)
            den = (a_inter * jnp.sum(qh.astype(F32) * n_prev, axis=-1, keepdims=True)
                   + jnp.sum(s, axis=-1, keepdims=True))
            hh = num / jnp.maximum(jnp.abs(den), jnp.exp(-m_t))
            mu = jnp.mean(hh, axis=-1, keepdims=True)
            hc = hh - mu
            var = jnp.mean(hc * hc, axis=-1, keepdims=True)
            h_ref[rows, hs] = hc * lax.rsqrt(var + EPS)

            b_last = b_col[tc - 1:tc, :]
            m_new = m_t[tc - 1:tc, :]
            g_state = jnp.exp(b_last + m_prev - m_new)
            g_src = jnp.exp(b_last - b_col + ig_col - m_new)
            vs = (vf_ref[rows, hs] * g_src).astype(BF16)
            c_ref[h] = g_state * c_prev + lax.dot_general(vs, kh, (((0,), (0,)), ((), ())),
                                                          preferred_element_type=F32)
            n_ref[h:h + 1, :] = g_state * n_prev + jnp.sum(g_src * kh.astype(F32), axis=0, keepdims=True)
            m_ref[:, h:h + 1] = m_new

    for j in range(tm // tc):
        chunk(slice(j * tc, (j + 1) * tc))

    out = (h_ref[...] * nw_ref[...] + sk_ref[...] * xc_ref[...]) * _silu(z_ref[...])
    y_ref[...] = _rms(out, on_ref[...]).astype(BF16)


def _mlstm(xm, z, c0, n0, m0, cv0, consts, *, n_seq, tiles, tm, tc, row_block0=0, shared_init=False):
    seq_spec = pl.BlockSpec((tm, D_MODEL), lambda i, t: (row_block0 + i * tiles + t, 0))
    out_seq_spec = pl.BlockSpec((tm, D_MODEL), lambda i, t: (i * tiles + t, 0))

    def st(shape, shared):
        nd = len(shape)
        return pl.BlockSpec((None,) + shape,
                            (lambda i, t: (0,) * (nd + 1)) if shared else (lambda i, t: (i,) + (0,) * nd))

    st_shapes = ((ML_HEADS, ML_HEAD_DIM, ML_HEAD_DIM), (ML_HEADS, ML_HEAD_DIM), (1, ML_HEADS),
                 (ML_CONV - 1, D_MODEL))
    return pl.pallas_call(
        functools.partial(_mlstm_kernel, tm=tm, tc=tc),
        grid=(n_seq, tiles),
        in_specs=[seq_spec, seq_spec] + [st(s, shared_init) for s in st_shapes]
                 + [_const_spec(a.shape) for a in consts],
        out_specs=[out_seq_spec] + [st(s, False) for s in st_shapes],
        out_shape=[jax.ShapeDtypeStruct((n_seq * tiles * tm, D_MODEL), BF16)]
                  + [jax.ShapeDtypeStruct((n_seq,) + s, F32) for s in st_shapes],
        scratch_shapes=[pltpu.VMEM((tm + 2 * SUBLANES, D_MODEL), F32), pltpu.VMEM((tm, D_MODEL), F32),
                        pltpu.VMEM((tm, D_MODEL), BF16), pltpu.VMEM((tm, D_MODEL), BF16),
                        pltpu.VMEM((tm, D_MODEL), BF16), pltpu.VMEM((tm, D_MODEL), F32),
                        pltpu.VMEM((tm, GATE_LANES), F32), pltpu.VMEM((tm, D_MODEL), F32)],
        compiler_params=pltpu.CompilerParams(dimension_semantics=("parallel", "arbitrary"),
                                             vmem_limit_bytes=VMEM_LIMIT),
        name="mlstm_mix",
    )(xm, z, c0, n0, m0, cv0, *consts)


def _row(v):
    return v.reshape(1, -1).astype(F32)


def _pad_lanes(v, width):
    return jnp.pad(v, [(0, 0)] * (v.ndim - 1) + [(0, width - v.shape[-1])])


def kernel(x_prompt, x_sample, state_s5_re, state_s5_im, state_mlstm_c, state_mlstm_n, state_mlstm_m,
           state_mlstm_conv, meta_tokens, norm_ffn1, ffn1_gate, ffn1_up, ffn1_down, norm_mix, w_in,
           s5_lambda_re, s5_lambda_im, s5_log_dt, s5_b_re, s5_b_im, s5_c_re, s5_c_im, s5_d, s5_glu_w, s5_glu_b,
           ml_conv_w, ml_conv_b, ml_wq, ml_wk, ml_wv, ml_igate_w, ml_igate_b, ml_fgate_w, ml_fgate_b,
           ml_norm_w, ml_skip, out_norm_s5, out_norm_ml, w_out, norm_ffn2, ffn2_gate, ffn2_up, ffn2_down,
           norm_final):
    nb, seq, _ = x_prompt.shape
    ns, dseq, _ = x_sample.shape
    n_meta = meta_tokens.shape[0]
    n_p, n_s = nb * seq, ns * dseq
    tile_p = min(TOKEN_TILE, seq)
    chunk_p = min(ML_CHUNK, tile_p)

    wg1, wu1, wd1 = ffn1_gate[0].astype(BF16), ffn1_up[0].astype(BF16), ffn1_down[0].astype(BF16)
    wg2, wu2, wd2 = ffn2_gate[0].astype(BF16), ffn2_up[0].astype(BF16), ffn2_down[0].astype(BF16)
    win = w_in[0].astype(BF16)
    wo = w_out[0].astype(BF16)
    glu_w = s5_glu_w[0].astype(BF16)

    ldt = jnp.broadcast_to(s5_log_dt[0][:, None], (S5_GROUPS, S5_STATE)).reshape(1, S5_LANES)
    pre, pim, bmat, cmat = _s5_prep(
        s5_lambda_re[0].reshape(1, S5_LANES), s5_lambda_im[0].reshape(1, S5_LANES), ldt,
        s5_b_re[0].reshape(S5_LANES, S5_GROUP).T, s5_b_im[0].reshape(S5_LANES, S5_GROUP).T,
        s5_c_re[0].reshape(D_MODEL, S5_STATE).T, s5_c_im[0].reshape(D_MODEL, S5_STATE).T)
    s5_consts = (pre, pim, bmat, cmat, _row(s5_d[0]), glu_w, _row(s5_glu_b[0]), _row(out_norm_s5[0]))

    bdq, bdk, bdv = _ml_prep(ml_wq[0].reshape(D_MODEL, ML_QKV_BLOCK), ml_wk[0].reshape(D_MODEL, ML_QKV_BLOCK),
                             ml_wv[0].reshape(D_MODEL, ML_QKV_BLOCK))
    gate_w = _pad_lanes(jnp.concatenate([ml_igate_w[0], ml_fgate_w[0]], axis=1), GATE_LANES).astype(BF16)
    gate_b = _pad_lanes(jnp.concatenate([ml_igate_b[0], ml_fgate_b[0]])[None, :], GATE_LANES)
    ml_consts = (ml_conv_w[0], _row(ml_conv_b[0]), bdq, bdk, bdv, gate_w, gate_b, _row(ml_norm_w[0]),
                 _row(ml_skip[0]), _row(out_norm_ml[0]))

    ffn1 = (_row(norm_ffn1[0]), wg1, wu1, wd1, _row(norm_mix[0]), win)
    seg_p = ((0, tile_p // SUBLANES),)
    seg_s = tuple((g * SUBLANES * dseq, dseq) for g in range(ns // SUBLANES))
    seg_m = ((n_s, n_meta // SUBLANES),)
    x1_p, u_p, xm_p, z_p = _ffn_in(x_prompt.reshape(n_p, D_MODEL), *ffn1, tm=tile_p, seg_layout=seg_p)
    small = jnp.concatenate([x_sample.reshape(n_s, D_MODEL), meta_tokens], axis=0)
    x1_s, u_s, xm_s, z_s = _ffn_in(small, *ffn1, tm=small.shape[0], seg_layout=seg_s + seg_m)

    zs5 = jnp.zeros((1, 1, S5_LANES), F32)
    _, mre, mim = _s5(u_s, zs5, zs5, s5_consts, n_seq=1, tiles=1, tm=n_meta, row_block0=n_s // n_meta)
    y5_p, pre_s, pim_s = _s5(u_p, mre, mim, s5_consts, n_seq=nb, tiles=seq // tile_p, tm=tile_p,
                             shared_init=True)
    y5_s, sre_s, sim_s = _s5(u_s, state_s5_re[0].reshape(ns // SUBLANES, SUBLANES, S5_LANES),
                             state_s5_im[0].reshape(ns // SUBLANES, SUBLANES, S5_LANES), s5_consts,
                             n_seq=ns // SUBLANES, tiles=1, tm=SUBLANES * dseq, independent=True)

    zc = jnp.zeros((1, ML_HEADS, ML_HEAD_DIM, ML_HEAD_DIM), F32)
    zn = jnp.zeros((1, ML_HEADS, ML_HEAD_DIM), F32)
    zm = jnp.zeros((1, 1, ML_HEADS), F32)
    zcv = jnp.zeros((1, ML_CONV - 1, D_MODEL), F32)
    _, c_m, n_m, m_m, cv_m = _mlstm(xm_s, z_s, zc, zn, zm, zcv, ml_consts, n_seq=1, tiles=1, tm=n_meta,
                                    tc=n_meta, row_block0=n_s // n_meta)
    ym_p, c_p, nn_p, m_p, cv_p = _mlstm(xm_p, z_p, c_m, n_m, m_m, cv_m, ml_consts, n_seq=nb,
                                        tiles=seq // tile_p, tm=tile_p, tc=chunk_p, shared_init=True)
    ym_s, c_s, nn_s, m_s, cv_s = _mlstm(xm_s, z_s, state_mlstm_c[0], state_mlstm_n[0],
                                        state_mlstm_m[0][:, None, :], state_mlstm_conv[0], ml_consts,
                                        n_seq=ns, tiles=1, tm=dseq, tc=dseq)

    ffn2 = (wo, _row(norm_ffn2[0]), wg2, wu2, wd2, _row(norm_final))
    y_p = _out_ffn(x1_p, y5_p, ym_p, *ffn2, rows=n_p, tm=tile_p, seg_layout=seg_p)
    y_s = _out_ffn(x1_s, y5_s, ym_s, *ffn2, rows=n_s, tm=n_s, seg_layout=seg_s)

    def s5_state(s, n):
        return s.reshape(1, n, S5_GROUPS, S5_STATE)

    return (y_p.reshape(nb, seq, D_MODEL), y_s.reshape(ns, dseq, D_MODEL),
            s5_state(pre_s, nb), s5_state(pim_s, nb), c_p[None], nn_p[None], m_p[:, 0][None], cv_p[None],
            s5_state(sre_s, ns), s5_state(sim_s, ns), c_s[None], nn_s[None], m_s[:, 0][None], cv_s[None])
```

```python
import functools

import jax
import jax.numpy as jnp
from jax import lax
from jax.experimental import pallas as pl
from jax.experimental.pallas import tpu as pltpu

F32 = jnp.float32
BF16 = jnp.bfloat16

D_MODEL = 1024
D_FF = 2816
S5_GROUPS = 64
S5_GROUP = 16
S5_STATE = 64
S5_LANES = S5_GROUPS * S5_STATE
S5_BLOCKS = 4
S5_BLOCK_GROUPS = S5_GROUPS // S5_BLOCKS
S5_BLOCK_IN = D_MODEL // S5_BLOCKS
S5_BLOCK_ST = S5_LANES // S5_BLOCKS
ML_HEADS = 4
ML_HEAD_DIM = 256
ML_CONV = 4
ML_QKV_BLOCK = 4
EPS = 1e-6
SUBLANES = 8
GATE_LANES = 128
POW_ROWS = 64
VMEM_LIMIT = 56 * 1024 * 1024

TOKEN_TILE = 512
ML_CHUNK = 256


def _rms(x, g):
    return x * lax.rsqrt(jnp.mean(x * x, axis=-1, keepdims=True) + EPS) * g


def _silu(x):
    return x * jax.nn.sigmoid(x)


def _swiglu(h, wg_ref, wu_ref, wd_ref, acc_ref, ff_chunk):
    for c in range(D_FF // ff_chunk):
        sl = slice(c * ff_chunk, (c + 1) * ff_chunk)
        g = jnp.dot(h, wg_ref[:, sl], preferred_element_type=F32)
        u = jnp.dot(h, wu_ref[:, sl], preferred_element_type=F32)
        a = (_silu(g) * u).astype(BF16)
        d = jnp.dot(a, wd_ref[sl, :], preferred_element_type=F32)
        if c == 0:
            acc_ref[...] = d
        else:
            acc_ref[...] += d
    return acc_ref[...]


def _const_spec(shape):
    nd = len(shape)
    return pl.BlockSpec(shape, lambda *_: (0,) * nd, pipeline_mode=pl.Buffered(1))


def _segment_rows(x, layout, inverse=False):
    parts = []
    for row0, r in layout:
        n = SUBLANES * r
        g = x[row0:row0 + n]
        if r % SUBLANES == 0:
            shape = (r, SUBLANES) if inverse else (SUBLANES, r)
            g = jnp.swapaxes(g.reshape(shape + g.shape[1:]), 0, 1).reshape(g.shape)
        elif inverse:
            g = jnp.concatenate([g[i * SUBLANES + a:i * SUBLANES + a + 1]
                                 for a in range(SUBLANES) for i in range(r)], axis=0)
        else:
            g = jnp.concatenate([g[a * r + i:a * r + i + 1]
                                 for i in range(r) for a in range(SUBLANES)], axis=0)
        parts.append(g)
    return parts[0] if len(parts) == 1 else jnp.concatenate(parts, axis=0)


def _ffn_in_kernel(x_ref, g1_ref, wg_ref, wu_ref, wd_ref, g2_ref, win_ref,
                   x1_ref, u_ref, xm_ref, z_ref, acc_ref, *, ff_chunk, seg_layout):
    x = x_ref[...]
    h = _rms(x, g1_ref[...]).astype(BF16)
    x1 = x + 0.5 * _swiglu(h, wg_ref, wu_ref, wd_ref, acc_ref, ff_chunk)
    x1_ref[...] = x1
    h2f = _rms(x1, g2_ref[...])
    h2 = h2f.astype(BF16)
    h2s = _segment_rows(h2f, seg_layout).astype(BF16)
    u_ref[...] = jnp.dot(h2s, win_ref[:, 0:D_MODEL], preferred_element_type=F32)
    xm_ref[...] = jnp.dot(h2, win_ref[:, D_MODEL:2 * D_MODEL], preferred_element_type=F32)
    z_ref[...] = jnp.dot(h2, win_ref[:, 2 * D_MODEL:3 * D_MODEL], preferred_element_type=F32)


def _ffn_in(x, g1, wg, wu, wd, g2, win, *, tm, seg_layout, ff_chunk=256):
    rows = x.shape[0]
    row_spec = pl.BlockSpec((tm, D_MODEL), lambda i: (i, 0))
    out = jax.ShapeDtypeStruct((rows, D_MODEL), F32)
    return pl.pallas_call(
        functools.partial(_ffn_in_kernel, ff_chunk=ff_chunk, seg_layout=seg_layout),
        grid=(rows // tm,),
        in_specs=[row_spec, _const_spec(g1.shape), _const_spec(wg.shape), _const_spec(wu.shape),
                  _const_spec(wd.shape), _const_spec(g2.shape), _const_spec(win.shape)],
        out_specs=[row_spec] * 4,
        out_shape=[out] * 4,
        scratch_shapes=[pltpu.VMEM((tm, D_MODEL), F32)],
        compiler_params=pltpu.CompilerParams(dimension_semantics=("parallel",), vmem_limit_bytes=VMEM_LIMIT),
        name="ffn_in",
    )(x, g1, wg, wu, wd, g2, win)


def _out_ffn_kernel(x1_ref, y5_ref, ym_ref, wo_ref, g_ref, wg_ref, wu_ref, wd_ref, gf_ref,
                    o_ref, acc_ref, *, ff_chunk, seg_layout):
    p5 = jnp.dot(y5_ref[...], wo_ref[0:D_MODEL, :], preferred_element_type=F32)
    x2 = (x1_ref[...] + _segment_rows(p5, seg_layout, inverse=True)
          + jnp.dot(ym_ref[...], wo_ref[D_MODEL:2 * D_MODEL, :], preferred_element_type=F32))
    h = _rms(x2, g_ref[...]).astype(BF16)
    x3 = x2 + 0.5 * _swiglu(h, wg_ref, wu_ref, wd_ref, acc_ref, ff_chunk)
    o_ref[...] = _rms(x3, gf_ref[...])


def _out_ffn(x1, y5, ym, wo, g, wg, wu, wd, gf, *, rows, tm, seg_layout, ff_chunk=256):
    row_spec = pl.BlockSpec((tm, D_MODEL), lambda i: (i, 0))
    return pl.pallas_call(
        functools.partial(_out_ffn_kernel, ff_chunk=ff_chunk, seg_layout=seg_layout),
        grid=(rows // tm,),
        in_specs=[row_spec, row_spec, row_spec, _const_spec(wo.shape), _const_spec(g.shape),
                  _const_spec(wg.shape), _const_spec(wu.shape), _const_spec(wd.shape), _const_spec(gf.shape)],
        out_specs=row_spec,
        out_shape=jax.ShapeDtypeStruct((rows, D_MODEL), F32),
        scratch_shapes=[pltpu.VMEM((tm, D_MODEL), F32)],
        compiler_params=pltpu.CompilerParams(dimension_semantics=("parallel",), vmem_limit_bytes=VMEM_LIMIT),
        name="out_ffn",
    )(x1, y5, ym, wo, g, wg, wu, wd, gf)


def _lam_bar(lr, li, ldt):
    dt = jnp.exp(ldt)
    mag = jnp.exp(lr * dt)
    th = li * dt
    return mag * jnp.cos(th), mag * jnp.sin(th), dt


def _cmul(ar, ai, br, bi):
    return ar * br - ai * bi, ar * bi + ai * br


def _dot_split(a, b):
    a_hi = a.astype(BF16)
    b_hi = b.astype(BF16)
    a_lo = (a - a_hi.astype(F32)).astype(BF16)
    b_lo = (b - b_hi.astype(F32)).astype(BF16)
    return (jnp.dot(a_hi, b_hi, preferred_element_type=F32) + jnp.dot(a_hi, b_lo, preferred_element_type=F32)
            + jnp.dot(a_lo, b_hi, preferred_element_type=F32))


def _s5_prep_kernel(lre_ref, lim_ref, ldt_ref, bre_ref, bim_ref, lre_t_ref, lim_t_ref, ldt_t_ref, cre_ref, cim_ref,
                    pre_ref, pim_ref, bcat_ref, wout_ref):
    lr = lre_ref[...]
    li = lim_ref[...]
    ar, ai, _ = _lam_bar(lr, li, ldt_ref[...])
    nr = ar - 1.0
    den = lr * lr + li * li
    cr = (nr * lr + ai * li) / den
    ci = (ai * lr - nr * li) / den
    b0r, b0i = _cmul(cr, ci, bre_ref[...], bim_ref[...])
    b1r, b1i = _cmul(ar, ai, b0r, b0i)

    pr, pi = ar, ai
    pre_ref[0:1, :] = pr
    pim_ref[0:1, :] = pi
    for j in range(1, POW_ROWS):
        pr, pi = _cmul(pr, pi, ar, ai)
        pre_ref[j:j + 1, :] = pr
        pim_ref[j:j + 1, :] = pi

    tr, ti, _ = _lam_bar(lre_t_ref[...], lim_t_ref[...], ldt_t_ref[...])
    c0r, c0i = cre_ref[...], cim_ref[...]
    c1r, c1i = _cmul(tr, ti, c0r, c0i)
    c2r, c2i = _cmul(tr, ti, c1r, c1i)

    g_shift, s_shift = S5_GROUP.bit_length() - 1, S5_STATE.bit_length() - 1
    b_shape = (S5_BLOCK_IN, S5_BLOCK_ST)
    b_mask = (lax.broadcasted_iota(jnp.int32, b_shape, 0) >> g_shift) == (
        lax.broadcasted_iota(jnp.int32, b_shape, 1) >> s_shift)
    c_shape = (S5_BLOCK_ST, S5_BLOCK_IN)
    c_mask = (lax.broadcasted_iota(jnp.int32, c_shape, 0) >> s_shift) == (
        lax.broadcasted_iota(jnp.int32, c_shape, 1) >> g_shift)

    def b_block(br, bi, blk):
        parts = []
        for x in (br, bi):
            x = jnp.concatenate([x[:, blk * S5_BLOCK_ST:(blk + 1) * S5_BLOCK_ST]] * S5_BLOCK_GROUPS, axis=0)
            parts.append(jnp.where(b_mask, x, 0.0))
        return jnp.concatenate(parts, axis=1)

    def c_block(xr, xi, blk):
        parts = []
        for x in (xr, -xi):
            x = jnp.concatenate([x[:, blk * S5_BLOCK_IN:(blk + 1) * S5_BLOCK_IN]] * S5_BLOCK_GROUPS, axis=0)
            parts.append(jnp.where(c_mask, x, 0.0))
        return jnp.concatenate(parts, axis=0)

    zero = jnp.zeros((S5_BLOCK_IN, S5_BLOCK_IN), F32)
    for blk in range(S5_BLOCKS):
        bm0 = b_block(b0r, b0i, blk)
        bm1 = b_block(b1r, b1i, blk)
        cm0 = c_block(c0r, c0i, blk)
        cm1 = c_block(c1r, c1i, blk)
        cm2 = c_block(c2r, c2i, blk)
        k01 = _dot_split(bm0, jnp.concatenate([cm0, cm1], axis=1))
        k0 = k01[:, 0:S5_BLOCK_IN]
        k1 = k01[:, S5_BLOCK_IN:2 * S5_BLOCK_IN]
        bcat_ref[blk, 0:S5_BLOCK_IN, :] = bm1.astype(BF16)
        bcat_ref[blk, S5_BLOCK_IN:2 * S5_BLOCK_IN, :] = bm0.astype(BF16)
        z_rows = 2 * S5_BLOCK_ST
        wout_ref[blk, 0:z_rows, :] = jnp.concatenate([cm1, cm2], axis=1).astype(BF16)
        wout_ref[blk, z_rows:z_rows + S5_BLOCK_IN, :] = jnp.concatenate([k0, k1], axis=1).astype(BF16)
        wout_ref[blk, z_rows + S5_BLOCK_IN:z_rows + 2 * S5_BLOCK_IN, :] = jnp.concatenate(
            [zero, k0], axis=1).astype(BF16)


def _s5_prep(lre, lim, ldt, bre_t, bim_t, lre_t, lim_t, ldt_t, cre_t, cim_t):
    pow_shape = jax.ShapeDtypeStruct((POW_ROWS, S5_LANES), F32)
    return pl.pallas_call(
        _s5_prep_kernel,
        out_shape=[pow_shape, pow_shape,
                   jax.ShapeDtypeStruct((S5_BLOCKS, 2 * S5_BLOCK_IN, 2 * S5_BLOCK_ST), BF16),
                   jax.ShapeDtypeStruct((S5_BLOCKS, 2 * S5_BLOCK_ST + 2 * S5_BLOCK_IN, 2 * S5_BLOCK_IN), BF16)],
        compiler_params=pltpu.CompilerParams(vmem_limit_bytes=VMEM_LIMIT),
        name="s5_prep",
    )(lre, lim, ldt, bre_t, bim_t, lre_t, lim_t, ldt_t, cre_t, cim_t)


def _ml_prep_kernel(wq_ref, wk_ref, wv_ref, oq_ref, ok_ref, ov_ref):
    rows = 128
    shift = ML_QKV_BLOCK.bit_length() - 1
    col = lax.broadcasted_iota(jnp.int32, (rows, D_MODEL), 1)
    sel = col & (ML_QKV_BLOCK - 1)
    for w_ref, o_ref in ((wq_ref, oq_ref), (wk_ref, ok_ref), (wv_ref, ov_ref)):
        for c in range(D_MODEL // rows):
            w = w_ref[c * rows:(c + 1) * rows, :]
            row = lax.broadcasted_iota(jnp.int32, (rows, D_MODEL), 0) + c * rows
            acc = jnp.zeros((rows, D_MODEL), F32)
            for o in range(ML_QKV_BLOCK):
                acc = jnp.where(sel == o, w[:, o:o + 1], acc)
            o_ref[c * rows:(c + 1) * rows, :] = jnp.where((row >> shift) == (col >> shift), acc, 0.0).astype(BF16)


def _ml_prep(wq, wk, wv):
    out = jax.ShapeDtypeStruct((D_MODEL, D_MODEL), BF16)
    return pl.pallas_call(_ml_prep_kernel, out_shape=[out] * 3, name="ml_prep")(wq, wk, wv)


def _s5_kernel(u_ref, sre0_ref, sim0_ref, pre_ref, pim_ref, bcat_ref, wout_ref, d_ref, gw_ref, gb_ref, on_ref,
               y_ref, sre_ref, sim_ref, s_ref, *, tm, independent):
    r = tm // SUBLANES
    nc = r // 2
    half = nc * SUBLANES
    t = pl.program_id(1)

    @pl.when(t == 0)
    def _():
        sre_ref[...] = sre0_ref[...]
        sim_ref[...] = sim0_ref[...]

    up = u_ref[...]
    u3 = up.reshape(nc, 2 * SUBLANES, D_MODEL)
    u0 = u3[:, 0:SUBLANES, :].reshape(half, D_MODEL).astype(BF16)
    u1 = u3[:, SUBLANES:2 * SUBLANES, :].reshape(half, D_MODEL).astype(BF16)

    y0s, y1s = [], []
    for b in range(S5_BLOCKS):
        lanes = slice(b * S5_BLOCK_ST, (b + 1) * S5_BLOCK_ST)
        chans = slice(b * S5_BLOCK_IN, (b + 1) * S5_BLOCK_IN)
        re = slice(0, S5_BLOCK_ST)
        im = slice(S5_BLOCK_ST, 2 * S5_BLOCK_ST)
        ucat = jnp.concatenate([u0[:, chans], u1[:, chans]], axis=1)
        s_ref[...] = jnp.dot(ucat, bcat_ref[b], preferred_element_type=F32)
        l2r = jnp.broadcast_to(pre_ref[1:2, lanes], (SUBLANES, S5_BLOCK_ST))
        l2i = jnp.broadcast_to(pim_ref[1:2, lanes], (SUBLANES, S5_BLOCK_ST))

        def scan(cr, ci):
            for c in range(nc):
                rows = slice(c * SUBLANES, (c + 1) * SUBLANES)
                nr = l2r * cr - l2i * ci + s_ref[rows, re]
                ni = l2r * ci + l2i * cr + s_ref[rows, im]
                s_ref[rows, re] = cr
                s_ref[rows, im] = ci
                cr, ci = nr, ni
            return cr, ci

        if independent:
            fr, fi = scan(sre_ref[:, lanes], sim_ref[:, lanes])
            sre_ref[:, lanes] = fr
            sim_ref[:, lanes] = fi
        else:
            zero = jnp.zeros((SUBLANES, S5_BLOCK_ST), F32)
            fr, fi = scan(zero, zero)
            rr = pre_ref[r - 1:r, lanes]
            ri = pim_ref[r - 1:r, lanes]
            cr = sre_ref[:, lanes]
            ci = sim_ref[:, lanes]
            rows_r, rows_i = [], []
            for a in range(SUBLANES):
                rows_r.append(cr)
                rows_i.append(ci)
                cr, ci = rr * cr - ri * ci + fr[a:a + 1], rr * ci + ri * cr + fi[a:a + 1]
            sre_ref[:, lanes] = cr
            sim_ref[:, lanes] = ci
            cin_r = jnp.concatenate(rows_r, axis=0)
            cin_i = jnp.concatenate(rows_i, axis=0)
            s_ref[0:SUBLANES, re] = cin_r
            s_ref[0:SUBLANES, im] = cin_i
            for c in range(1, nc):
                rows = slice(c * SUBLANES, (c + 1) * SUBLANES)
                pr = pre_ref[2 * c - 1:2 * c, lanes]
                pi = pim_ref[2 * c - 1:2 * c, lanes]
                s_ref[rows, re] += pr * cin_r - pi * cin_i
                s_ref[rows, im] += pr * cin_i + pi * cin_r
        yb = jnp.dot(jnp.concatenate([s_ref[...].astype(BF16), ucat], axis=1), wout_ref[b],
                     preferred_element_type=F32)
        y0s.append(yb[:, 0:S5_BLOCK_IN])
        y1s.append(yb[:, S5_BLOCK_IN:2 * S5_BLOCK_IN])

    y0 = jnp.concatenate(y0s, axis=1).reshape(nc, SUBLANES, D_MODEL)
    y1 = jnp.concatenate(y1s, axis=1).reshape(nc, SUBLANES, D_MODEL)
    y = jnp.concatenate([y0, y1], axis=1).reshape(tm, D_MODEL) + d_ref[...] * up
    g = jax.nn.gelu(y)
    o = g * jax.nn.sigmoid(jnp.dot(g.astype(BF16), gw_ref[...], preferred_element_type=F32) + gb_ref[...])
    y_ref[...] = _rms(o, on_ref[...]).astype(BF16)


def _s5(u, sre0, sim0, consts, *, n_seq, tiles, tm, row_block0=0, independent=False, shared_init=False):
    srows = SUBLANES if independent else 1
    seq_spec = pl.BlockSpec((tm, D_MODEL), lambda i, t: (row_block0 + i * tiles + t, 0))
    out_seq_spec = pl.BlockSpec((tm, D_MODEL), lambda i, t: (i * tiles + t, 0))
    st_in = pl.BlockSpec((None, srows, S5_LANES), (lambda i, t: (0, 0, 0)) if shared_init else (lambda i, t: (i, 0, 0)))
    st_out = pl.BlockSpec((None, srows, S5_LANES), lambda i, t: (i, 0, 0))
    st_shape = jax.ShapeDtypeStruct((n_seq, srows, S5_LANES), F32)
    return pl.pallas_call(
        functools.partial(_s5_kernel, tm=tm, independent=independent),
        grid=(n_seq, tiles),
        in_specs=[seq_spec, st_in, st_in] + [_const_spec(a.shape) for a in consts],
        out_specs=[out_seq_spec, st_out, st_out],
        out_shape=[jax.ShapeDtypeStruct((n_seq * tiles * tm, D_MODEL), BF16), st_shape, st_shape],
        scratch_shapes=[pltpu.VMEM((tm // 2, 2 * S5_BLOCK_ST), F32)],
        compiler_params=pltpu.CompilerParams(dimension_semantics=("parallel", "arbitrary"),
                                             vmem_limit_bytes=VMEM_LIMIT),
        name="s5_mix",
    )(u, sre0, sim0, *consts)


def _ml_front(xm_ref, cv_ref, xbuf_ref, consts, bufs, tm):
    cw_ref, cb_ref, bdq_ref, bdk_ref, bdv_ref, gw_ref, gb_ref = consts
    xc_ref, q_ref, k_ref, v_ref, vf_ref, g_ref = bufs
    pre = ML_CONV - 1
    xm = xm_ref[...]
    xbuf_ref[SUBLANES - pre:SUBLANES, :] = cv_ref[...]
    xbuf_ref[SUBLANES:SUBLANES + tm, :] = xm
    cw = cw_ref[...]
    xc = cb_ref[...] + cw[pre:pre + 1] * xm
    for j in range(pre):
        xc = xc + cw[j:j + 1] * xbuf_ref[SUBLANES - pre + j:SUBLANES - pre + j + tm, :]
    cv_ref[...] = xbuf_ref[SUBLANES + tm - pre:SUBLANES + tm, :]
    xc = _silu(xc)
    xc_ref[...] = xc

    xcb = xc.astype(BF16)
    q = jnp.dot(xcb, bdq_ref[...], preferred_element_type=F32)
    k = jnp.dot(xcb, bdk_ref[...], preferred_element_type=F32)
    v = jnp.dot(xm.astype(BF16), bdv_ref[...], preferred_element_type=F32)
    qb = q.astype(BF16)
    vb = v.astype(BF16)
    g_ref[...] = jnp.dot(jnp.concatenate([qb, k.astype(BF16), vb], axis=1), gw_ref[...],
                         preferred_element_type=F32) + gb_ref[...]
    q_ref[...] = qb
    k_ref[...] = (k * (ML_HEAD_DIM ** -0.5)).astype(BF16)
    v_ref[...] = vb
    vf_ref[...] = v


def _ml_back(bufs, z_ref, state, consts, y_ref, h_ref, tm, tc):
    xc_ref, q_ref, k_ref, v_ref, vf_ref, g_ref = bufs
    c_ref, n_ref, m_ref = state
    nw_ref, sk_ref, on_ref = consts
    lane = lax.broadcasted_iota(jnp.int32, (tc, GATE_LANES), 1)
    row = lax.broadcasted_iota(jnp.int32, (tc, tc), 0)
    col = lax.broadcasted_iota(jnp.int32, (tc, tc), 1)
    causal = row >= col
    tril = causal.astype(F32)

    def chunk(rows):
        gates = g_ref[rows, :]
        lf = jnp.minimum(gates, 0.0) - jnp.log1p(jnp.exp(-jnp.abs(gates)))
        lf = jnp.where((lane >= ML_HEADS) & (lane < 2 * ML_HEADS), lf, 0.0)
        cum = jnp.dot(tril, lf, preferred_element_type=F32, precision=lax.Precision.HIGHEST)
        arr = jnp.where(lane < ML_HEADS, gates, cum)
        arr_t = arr.T
        for h in range(ML_HEADS):
            hs = slice(h * ML_HEAD_DIM, (h + 1) * ML_HEAD_DIM)
            qh = q_ref[rows, hs]
            kh = k_ref[rows, hs]
            vh = v_ref[rows, hs]
            ig_col = arr[:, h:h + 1]
            b_col = arr[:, ML_HEADS + h:ML_HEADS + h + 1]
            ig_row = arr_t[h:h + 1, :]
            b_row = arr_t[ML_HEADS + h:ML_HEADS + h + 1, :]
            m_prev = m_ref[:, h:h + 1]
            c_prev = c_ref[h]
            n_prev = n_ref[h:h + 1, :]

            logw = jnp.where(causal, b_col - b_row + ig_row, -jnp.inf)
            log_inter = b_col + m_prev
            m_t = jnp.maximum(log_inter, jnp.max(logw, axis=-1, keepdims=True))
            w = jnp.exp(logw - m_t)
            a_inter = jnp.exp(log_inter - m_t)
            s = lax.dot_general(qh, kh, (((1,), (1,)), ((), ())), preferred_element_type=F32) * w
            inter = lax.dot_general(qh, c_prev.astype(BF16), (((1,), (1,)), ((), ())),
                                    preferred_element_type=F32)
            num = a_inter * inter + jnp.dot(s.astype(BF16), vh, preferred_element_type=F32)
            den = (a_inter * jnp.sum(qh.astype(F32) * n_prev, axis=-1, keepdims=True)
                   + jnp.sum(s, axis=-1, keepdims=True))
            hh = num / jnp.maximum(jnp.abs(den), jnp.exp(-m_t))
            mu = jnp.mean(hh, axis=-1, keepdims=True)
            hc = hh - mu
            var = jnp.mean(hc * hc, axis=-1, keepdims=True)
            h_ref[rows, hs] = hc * lax.rsqrt(var + EPS)

            b_last = b_col[tc - 1:tc, :]
            m_new = m_t[tc - 1:tc, :]
            g_state = jnp.exp(b_last + m_prev - m_new)
            g_src = jnp.exp(b_last - b_col + ig_col - m_new)
            vs = (vf_ref[rows, hs] * g_src).astype(BF16)
            c_ref[h] = g_state * c_prev + lax.dot_general(vs, kh, (((0,), (0,)), ((), ())),
                                                          preferred_element_type=F32)
            n_ref[h:h + 1, :] = g_state * n_prev + jnp.sum(g_src * kh.astype(F32), axis=0, keepdims=True)
            m_ref[:, h:h + 1] = m_new

    for j in range(tm // tc):
        chunk(slice(j * tc, (j + 1) * tc))

    out = (h_ref[...] * nw_ref[...] + sk_ref[...] * xc_ref[...]) * _silu(z_ref[...])
    y_ref[...] = _rms(out, on_ref[...]).astype(BF16)


def _mlstm_kernel(xm_ref, z_ref, c0_ref, n0_ref, m0_ref, cv0_ref, cw_ref, cb_ref, bdq_ref, bdk_ref, bdv_ref,
                  gw_ref, gb_ref, nw_ref, sk_ref, on_ref,
                  y_ref, c_ref, n_ref, m_ref, cv_ref, xbuf_ref, h_ref, *bufs, tm, tc, pipelined):
    t = pl.program_id(1)
    front_consts = (cw_ref, cb_ref, bdq_ref, bdk_ref, bdv_ref, gw_ref, gb_ref)
    back_consts = (nw_ref, sk_ref, on_ref)
    state = (c_ref, n_ref, m_ref)
    n_buf = len(bufs) // 2 if pipelined else len(bufs)

    @pl.when(t == 0)
    def _():
        cv_ref[...] = cv0_ref[...]
        if pipelined:
            for ref in bufs[n_buf:]:
                ref[...] = jnp.zeros(ref.shape, ref.dtype)

    @pl.when(t <= (1 if pipelined else 0))
    def _():
        c_ref[...] = c0_ref[...]
        n_ref[...] = n0_ref[...]
        m_ref[...] = m0_ref[...]

    def step(front_set, back_set):
        _ml_front(xm_ref, cv_ref, xbuf_ref, front_consts, front_set, tm)
        _ml_back(back_set, z_ref, state, back_consts, y_ref, h_ref, tm, tc)

    if pipelined:
        pl.when(t % 2 == 0)(lambda: step(bufs[:n_buf], bufs[n_buf:]))
        pl.when(t % 2 == 1)(lambda: step(bufs[n_buf:], bufs[:n_buf]))
    else:
        step(bufs, bufs)


def _mlstm(xm, z, c0, n0, m0, cv0, consts, *, n_seq, tiles, tm, tc, row_block0=0, shared_init=False,
           pipelined=False):
    if pipelined:
        steps = tiles + 1
        front_spec = pl.BlockSpec((tm, D_MODEL),
                                  lambda i, t: (row_block0 + i * tiles + jnp.minimum(t, tiles - 1), 0))
        back_spec = pl.BlockSpec((tm, D_MODEL),
                                 lambda i, t: (row_block0 + i * tiles + jnp.maximum(t - 1, 0), 0))
        out_seq_spec = pl.BlockSpec((tm, D_MODEL), lambda i, t: (i * tiles + jnp.maximum(t - 1, 0), 0))
    else:
        steps = tiles
        front_spec = back_spec = pl.BlockSpec((tm, D_MODEL), lambda i, t: (row_block0 + i * tiles + t, 0))
        out_seq_spec = pl.BlockSpec((tm, D_MODEL), lambda i, t: (i * tiles + t, 0))

    def st(shape, shared):
        nd = len(shape)
        return pl.BlockSpec((None,) + shape,
                            (lambda i, t: (0,) * (nd + 1)) if shared else (lambda i, t: (i,) + (0,) * nd))

    st_shapes = ((ML_HEADS, ML_HEAD_DIM, ML_HEAD_DIM), (ML_HEADS, ML_HEAD_DIM), (1, ML_HEADS),
                 (ML_CONV - 1, D_MODEL))
    buf_set = [pltpu.VMEM((tm, D_MODEL), F32), pltpu.VMEM((tm, D_MODEL), BF16), pltpu.VMEM((tm, D_MODEL), BF16),
               pltpu.VMEM((tm, D_MODEL), BF16), pltpu.VMEM((tm, D_MODEL), F32), pltpu.VMEM((tm, GATE_LANES), F32)]
    return pl.pallas_call(
        functools.partial(_mlstm_kernel, tm=tm, tc=tc, pipelined=pipelined),
        grid=(n_seq, steps),
        in_specs=[front_spec, back_spec] + [st(s, shared_init) for s in st_shapes]
                 + [_const_spec(a.shape) for a in consts],
        out_specs=[out_seq_spec] + [st(s, False) for s in st_shapes],
        out_shape=[jax.ShapeDtypeStruct((n_seq * tiles * tm, D_MODEL), BF16)]
                  + [jax.ShapeDtypeStruct((n_seq,) + s, F32) for s in st_shapes],
        scratch_shapes=[pltpu.VMEM((tm + 2 * SUBLANES, D_MODEL), F32), pltpu.VMEM((tm, D_MODEL), F32)]
                       + buf_set * (2 if pipelined else 1),
        compiler_params=pltpu.CompilerParams(dimension_semantics=("parallel", "arbitrary"),
                                             vmem_limit_bytes=VMEM_LIMIT),
        name="mlstm_mix",
    )(xm, z, c0, n0, m0, cv0, *consts)


def _row(v):
    return v.reshape(1, -1).astype(F32)


def _pad_lanes(v, width):
    return jnp.pad(v, [(0, 0)] * (v.ndim - 1) + [(0, width - v.shape[-1])])


def kernel(x_prompt, x_sample, state_s5_re, state_s5_im, state_mlstm_c, state_mlstm_n, state_mlstm_m,
           state_mlstm_conv, meta_tokens, norm_ffn1, ffn1_gate, ffn1_up, ffn1_down, norm_mix, w_in,
           s5_lambda_re, s5_lambda_im, s5_log_dt, s5_b_re, s5_b_im, s5_c_re, s5_c_im, s5_d, s5_glu_w, s5_glu_b,
           ml_conv_w, ml_conv_b, ml_wq, ml_wk, ml_wv, ml_igate_w, ml_igate_b, ml_fgate_w, ml_fgate_b,
           ml_norm_w, ml_skip, out_norm_s5, out_norm_ml, w_out, norm_ffn2, ffn2_gate, ffn2_up, ffn2_down,
           norm_final):
    nb, seq, _ = x_prompt.shape
    ns, dseq, _ = x_sample.shape
    n_meta = meta_tokens.shape[0]
    n_p, n_s = nb * seq, ns * dseq
    tile_p = min(TOKEN_TILE, seq)
    chunk_p = min(ML_CHUNK, tile_p)

    wg1, wu1, wd1 = ffn1_gate[0].astype(BF16), ffn1_up[0].astype(BF16), ffn1_down[0].astype(BF16)
    wg2, wu2, wd2 = ffn2_gate[0].astype(BF16), ffn2_up[0].astype(BF16), ffn2_down[0].astype(BF16)
    win = w_in[0].astype(BF16)
    wo = w_out[0].astype(BF16)
    glu_w = s5_glu_w[0].astype(BF16)

    ldt_gs = jnp.broadcast_to(s5_log_dt[0][:, None], (S5_GROUPS, S5_STATE))

    def state_major(a):
        return jnp.repeat(a.T, S5_GROUP, axis=1)

    pre, pim, bcat, wout = _s5_prep(
        s5_lambda_re[0].reshape(1, S5_LANES), s5_lambda_im[0].reshape(1, S5_LANES), ldt_gs.reshape(1, S5_LANES),
        s5_b_re[0].reshape(S5_LANES, S5_GROUP).T, s5_b_im[0].reshape(S5_LANES, S5_GROUP).T,
        state_major(s5_lambda_re[0]), state_major(s5_lambda_im[0]), state_major(ldt_gs),
        s5_c_re[0].reshape(D_MODEL, S5_STATE).T, s5_c_im[0].reshape(D_MODEL, S5_STATE).T)
    s5_consts = (pre, pim, bcat, wout, _row(s5_d[0]), glu_w, _row(s5_glu_b[0]), _row(out_norm_s5[0]))

    bdq, bdk, bdv = _ml_prep(ml_wq[0].reshape(D_MODEL, ML_QKV_BLOCK), ml_wk[0].reshape(D_MODEL, ML_QKV_BLOCK),
                             ml_wv[0].reshape(D_MODEL, ML_QKV_BLOCK))
    gate_w = _pad_lanes(jnp.concatenate([ml_igate_w[0], ml_fgate_w[0]], axis=1), GATE_LANES).astype(BF16)
    gate_b = _pad_lanes(jnp.concatenate([ml_igate_b[0], ml_fgate_b[0]])[None, :], GATE_LANES)
    ml_consts = (ml_conv_w[0], _row(ml_conv_b[0]), bdq, bdk, bdv, gate_w, gate_b, _row(ml_norm_w[0]),
                 _row(ml_skip[0]), _row(out_norm_ml[0]))

    ffn1 = (_row(norm_ffn1[0]), wg1, wu1, wd1, _row(norm_mix[0]), win)
    seg_p = ((0, tile_p // SUBLANES),)
    seg_s = tuple((g * SUBLANES * dseq, dseq) for g in range(ns // SUBLANES))
    seg_m = ((n_s, n_meta // SUBLANES),)
    x1_p, u_p, xm_p, z_p = _ffn_in(x_prompt.reshape(n_p, D_MODEL), *ffn1, tm=tile_p, seg_layout=seg_p)
    small = jnp.concatenate([x_sample.reshape(n_s, D_MODEL), meta_tokens], axis=0)
    x1_s, u_s, xm_s, z_s = _ffn_in(small, *ffn1, tm=small.shape[0], seg_layout=seg_s + seg_m)

    zs5 = jnp.zeros((1, 1, S5_LANES), F32)
    _, mre, mim = _s5(u_s, zs5, zs5, s5_consts, n_seq=1, tiles=1, tm=n_meta, row_block0=n_s // n_meta)
    y5_p, pre_s, pim_s = _s5(u_p, mre, mim, s5_consts, n_seq=nb, tiles=seq // tile_p, tm=tile_p,
                             shared_init=True)
    y5_s, sre_s, sim_s = _s5(u_s, state_s5_re[0].reshape(ns // SUBLANES, SUBLANES, S5_LANES),
                             state_s5_im[0].reshape(ns // SUBLANES, SUBLANES, S5_LANES), s5_consts,
                             n_seq=ns // SUBLANES, tiles=1, tm=SUBLANES * dseq, independent=True)

    zc = jnp.zeros((1, ML_HEADS, ML_HEAD_DIM, ML_HEAD_DIM), F32)
    zn = jnp.zeros((1, ML_HEADS, ML_HEAD_DIM), F32)
    zm = jnp.zeros((1, 1, ML_HEADS), F32)
    zcv = jnp.zeros((1, ML_CONV - 1, D_MODEL), F32)
    _, c_m, n_m, m_m, cv_m = _mlstm(xm_s, z_s, zc, zn, zm, zcv, ml_consts, n_seq=1, tiles=1, tm=n_meta,
                                    tc=n_meta, row_block0=n_s // n_meta)
    ym_p, c_p, nn_p, m_p, cv_p = _mlstm(xm_p, z_p, c_m, n_m, m_m, cv_m, ml_consts, n_seq=nb,
                                        tiles=seq // tile_p, tm=tile_p, tc=chunk_p, shared_init=True,
                                        pipelined=True)
    ym_s, c_s, nn_s, m_s, cv_s = _mlstm(xm_s, z_s, state_mlstm_c[0], state_mlstm_n[0],
                                        state_mlstm_m[0][:, None, :], state_mlstm_conv[0], ml_consts,
                                        n_seq=ns, tiles=1, tm=dseq, tc=dseq)

    ffn2 = (wo, _row(norm_ffn2[0]), wg2, wu2, wd2, _row(norm_final))
    y_p = _out_ffn(x1_p, y5_p, ym_p, *ffn2, rows=n_p, tm=tile_p, seg_layout=seg_p)
    y_s = _out_ffn(x1_s, y5_s, ym_s, *ffn2, rows=n_s, tm=n_s, seg_layout=seg_s)

    def s5_state(s, n):
        return s.reshape(1, n, S5_GROUPS, S5_STATE)

    return (y_p.reshape(nb, seq, D_MODEL), y_s.reshape(ns, dseq, D_MODEL),
            s5_state(pre_s, nb), s5_state(pim_s, nb), c_p[None], nn_p[None], m_p[:, 0][None], cv_p[None],
            s5_state(sre_s, ns), s5_state(sim_s, ns), c_s[None], nn_s[None], m_s[:, 0][None], cv_s[None])
```

```python
import functools

import jax
import jax.numpy as jnp
from jax import lax
from jax.experimental import pallas as pl
from jax.experimental.pallas import tpu as pltpu

F32 = jnp.float32
BF16 = jnp.bfloat16

D_MODEL = 1024
D_FF = 2816
S5_GROUPS = 64
S5_GROUP = 16
S5_STATE = 64
S5_LANES = S5_GROUPS * S5_STATE
MXU_TILE = 256
S5_BLOCKS = 8
S5_BLOCK_GROUPS = S5_GROUPS // S5_BLOCKS
S5_BLOCK_IN = D_MODEL // S5_BLOCKS
S5_BLOCK_ST = S5_LANES // S5_BLOCKS
ML_HEADS = 4
ML_HEAD_DIM = 256
ML_CONV = 4
ML_QKV_BLOCK = 4
EPS = 1e-6
SUBLANES = 8
GATE_LANES = 128
POW_ROWS = 64
VMEM_LIMIT = 56 * 1024 * 1024

TOKEN_TILE = 512
ML_CHUNK = 256


def _rms(x, g):
    return x * lax.rsqrt(jnp.mean(x * x, axis=-1, keepdims=True) + EPS) * g


def _silu(x):
    return x * jax.nn.sigmoid(x)


def _swiglu(h, wg_ref, wu_ref, wd_ref, acc_ref, ff_chunk):
    for c in range(D_FF // ff_chunk):
        sl = slice(c * ff_chunk, (c + 1) * ff_chunk)
        g = jnp.dot(h, wg_ref[:, sl], preferred_element_type=F32)
        u = jnp.dot(h, wu_ref[:, sl], preferred_element_type=F32)
        a = (_silu(g) * u).astype(BF16)
        d = jnp.dot(a, wd_ref[sl, :], preferred_element_type=F32)
        if c == 0:
            acc_ref[...] = d
        else:
            acc_ref[...] += d
    return acc_ref[...]


def _const_spec(shape):
    nd = len(shape)
    return pl.BlockSpec(shape, lambda *_: (0,) * nd, pipeline_mode=pl.Buffered(1))


def _segment_rows(x, layout, inverse=False):
    parts = []
    for row0, r in layout:
        n = SUBLANES * r
        g = x[row0:row0 + n]
        if r % SUBLANES == 0:
            shape = (r, SUBLANES) if inverse else (SUBLANES, r)
            g = jnp.swapaxes(g.reshape(shape + g.shape[1:]), 0, 1).reshape(g.shape)
        elif inverse:
            g = jnp.concatenate([g[i * SUBLANES + a:i * SUBLANES + a + 1]
                                 for a in range(SUBLANES) for i in range(r)], axis=0)
        else:
            g = jnp.concatenate([g[a * r + i:a * r + i + 1]
                                 for i in range(r) for a in range(SUBLANES)], axis=0)
        parts.append(g)
    return parts[0] if len(parts) == 1 else jnp.concatenate(parts, axis=0)


def _ffn_in_kernel(x_ref, g1_ref, wg_ref, wu_ref, wd_ref, g2_ref, win_ref,
                   x1_ref, u_ref, xm_ref, z_ref, acc_ref, *, ff_chunk, seg_layout):
    x = x_ref[...]
    h = _rms(x, g1_ref[...]).astype(BF16)
    x1 = x + 0.5 * _swiglu(h, wg_ref, wu_ref, wd_ref, acc_ref, ff_chunk)
    x1_ref[...] = x1
    h2f = _rms(x1, g2_ref[...])
    h2 = h2f.astype(BF16)
    h2s = _segment_rows(h2f, seg_layout).astype(BF16)
    u_ref[...] = jnp.dot(h2s, win_ref[:, 0:D_MODEL], preferred_element_type=F32)
    xm_ref[...] = jnp.dot(h2, win_ref[:, D_MODEL:2 * D_MODEL], preferred_element_type=F32)
    z_ref[...] = jnp.dot(h2, win_ref[:, 2 * D_MODEL:3 * D_MODEL], preferred_element_type=F32)


def _ffn_in(x, g1, wg, wu, wd, g2, win, *, tm, seg_layout, ff_chunk=256):
    rows = x.shape[0]
    row_spec = pl.BlockSpec((tm, D_MODEL), lambda i: (i, 0))
    out = jax.ShapeDtypeStruct((rows, D_MODEL), F32)
    return pl.pallas_call(
        functools.partial(_ffn_in_kernel, ff_chunk=ff_chunk, seg_layout=seg_layout),
        grid=(rows // tm,),
        in_specs=[row_spec, _const_spec(g1.shape), _const_spec(wg.shape), _const_spec(wu.shape),
                  _const_spec(wd.shape), _const_spec(g2.shape), _const_spec(win.shape)],
        out_specs=[row_spec] * 4,
        out_shape=[out] * 4,
        scratch_shapes=[pltpu.VMEM((tm, D_MODEL), F32)],
        compiler_params=pltpu.CompilerParams(dimension_semantics=("parallel",), vmem_limit_bytes=VMEM_LIMIT),
        name="ffn_in",
    )(x, g1, wg, wu, wd, g2, win)


def _out_ffn_kernel(x1_ref, y5_ref, ym_ref, wo_ref, g_ref, wg_ref, wu_ref, wd_ref, gf_ref,
                    o_ref, acc_ref, *, ff_chunk, seg_layout):
    p5 = jnp.dot(y5_ref[...], wo_ref[0:D_MODEL, :], preferred_element_type=F32)
    x2 = (x1_ref[...] + _segment_rows(p5, seg_layout, inverse=True)
          + jnp.dot(ym_ref[...], wo_ref[D_MODEL:2 * D_MODEL, :], preferred_element_type=F32))
    h = _rms(x2, g_ref[...]).astype(BF16)
    x3 = x2 + 0.5 * _swiglu(h, wg_ref, wu_ref, wd_ref, acc_ref, ff_chunk)
    o_ref[...] = _rms(x3, gf_ref[...])


def _out_ffn(x1, y5, ym, wo, g, wg, wu, wd, gf, *, rows, tm, seg_layout, ff_chunk=256):
    row_spec = pl.BlockSpec((tm, D_MODEL), lambda i: (i, 0))
    return pl.pallas_call(
        functools.partial(_out_ffn_kernel, ff_chunk=ff_chunk, seg_layout=seg_layout),
        grid=(rows // tm,),
        in_specs=[row_spec, row_spec, row_spec, _const_spec(wo.shape), _const_spec(g.shape),
                  _const_spec(wg.shape), _const_spec(wu.shape), _const_spec(wd.shape), _const_spec(gf.shape)],
        out_specs=row_spec,
        out_shape=jax.ShapeDtypeStruct((rows, D_MODEL), F32),
        scratch_shapes=[pltpu.VMEM((tm, D_MODEL), F32)],
        compiler_params=pltpu.CompilerParams(dimension_semantics=("parallel",), vmem_limit_bytes=VMEM_LIMIT),
        name="out_ffn",
    )(x1, y5, ym, wo, g, wg, wu, wd, gf)


def _lam_bar(lr, li, ldt):
    dt = jnp.exp(ldt)
    mag = jnp.exp(lr * dt)
    th = li * dt
    return mag * jnp.cos(th), mag * jnp.sin(th), dt


def _cmul(ar, ai, br, bi):
    return ar * br - ai * bi, ar * bi + ai * br


def _dot_split(a, b):
    a_hi = a.astype(BF16)
    b_hi = b.astype(BF16)
    a_lo = (a - a_hi.astype(F32)).astype(BF16)
    b_lo = (b - b_hi.astype(F32)).astype(BF16)
    return (jnp.dot(a_hi, b_hi, preferred_element_type=F32) + jnp.dot(a_hi, b_lo, preferred_element_type=F32)
            + jnp.dot(a_lo, b_hi, preferred_element_type=F32))


def _s5_prep_kernel(lre_ref, lim_ref, ldt_ref, bre_ref, bim_ref, lre_t_ref, lim_t_ref, ldt_t_ref, cre_ref, cim_ref,
                    pre_ref, pim_ref, bcat_ref, wout_ref):
    lr = lre_ref[...]
    li = lim_ref[...]
    ar, ai, _ = _lam_bar(lr, li, ldt_ref[...])
    nr = ar - 1.0
    den = lr * lr + li * li
    cr = (nr * lr + ai * li) / den
    ci = (ai * lr - nr * li) / den
    b0r, b0i = _cmul(cr, ci, bre_ref[...], bim_ref[...])
    b1r, b1i = _cmul(ar, ai, b0r, b0i)

    pr, pi = ar, ai
    pre_ref[0:1, :] = pr
    pim_ref[0:1, :] = pi
    for j in range(1, POW_ROWS):
        pr, pi = _cmul(pr, pi, ar, ai)
        pre_ref[j:j + 1, :] = pr
        pim_ref[j:j + 1, :] = pi

    tr, ti, _ = _lam_bar(lre_t_ref[...], lim_t_ref[...], ldt_t_ref[...])
    c0r, c0i = cre_ref[...], cim_ref[...]
    c1r, c1i = _cmul(tr, ti, c0r, c0i)
    c2r, c2i = _cmul(tr, ti, c1r, c1i)

    g_shift, s_shift = S5_GROUP.bit_length() - 1, S5_STATE.bit_length() - 1
    b_shape = (S5_BLOCK_IN, S5_BLOCK_ST)
    b_mask = (lax.broadcasted_iota(jnp.int32, b_shape, 0) >> g_shift) == (
        lax.broadcasted_iota(jnp.int32, b_shape, 1) >> s_shift)
    c_shape = (S5_BLOCK_ST, S5_BLOCK_IN)
    c_mask = (lax.broadcasted_iota(jnp.int32, c_shape, 0) >> s_shift) == (
        lax.broadcasted_iota(jnp.int32, c_shape, 1) >> g_shift)

    def b_block(br, bi, blk):
        parts = []
        for x in (br, bi):
            x = jnp.concatenate([x[:, blk * S5_BLOCK_ST:(blk + 1) * S5_BLOCK_ST]] * S5_BLOCK_GROUPS, axis=0)
            parts.append(jnp.where(b_mask, x, 0.0))
        return jnp.concatenate(parts, axis=1)

    def c_block(xr, xi, blk):
        parts = []
        for x in (xr, -xi):
            x = jnp.concatenate([x[:, blk * S5_BLOCK_IN:(blk + 1) * S5_BLOCK_IN]] * S5_BLOCK_GROUPS, axis=0)
            parts.append(jnp.where(c_mask, x, 0.0))
        return jnp.concatenate(parts, axis=0)

    zero = jnp.zeros((S5_BLOCK_IN, S5_BLOCK_IN), F32)
    for blk in range(S5_BLOCKS):
        bm0 = b_block(b0r, b0i, blk)
        bm1 = b_block(b1r, b1i, blk)
        cm0 = c_block(c0r, c0i, blk)
        cm1 = c_block(c1r, c1i, blk)
        cm2 = c_block(c2r, c2i, blk)
        k01 = _dot_split(bm0, jnp.concatenate([cm0, cm1], axis=1))
        k0 = k01[:, 0:S5_BLOCK_IN]
        k1 = k01[:, S5_BLOCK_IN:2 * S5_BLOCK_IN]
        bcat_ref[blk, 0:S5_BLOCK_IN, :] = bm1.astype(BF16)
        bcat_ref[blk, S5_BLOCK_IN:2 * S5_BLOCK_IN, :] = bm0.astype(BF16)
        z_rows = 2 * S5_BLOCK_ST
        wout_ref[blk, 0:z_rows, :] = jnp.concatenate([cm1, cm2], axis=1).astype(BF16)
        wout_ref[blk, z_rows:z_rows + S5_BLOCK_IN, :] = jnp.concatenate([k0, k1], axis=1).astype(BF16)
        wout_ref[blk, z_rows + S5_BLOCK_IN:z_rows + 2 * S5_BLOCK_IN, :] = jnp.concatenate(
            [zero, k0], axis=1).astype(BF16)


def _s5_prep(lre, lim, ldt, bre_t, bim_t, lre_t, lim_t, ldt_t, cre_t, cim_t):
    pow_shape = jax.ShapeDtypeStruct((POW_ROWS, S5_LANES), F32)
    return pl.pallas_call(
        _s5_prep_kernel,
        out_shape=[pow_shape, pow_shape,
                   jax.ShapeDtypeStruct((S5_BLOCKS, 2 * S5_BLOCK_IN, 2 * S5_BLOCK_ST), BF16),
                   jax.ShapeDtypeStruct((S5_BLOCKS, 2 * S5_BLOCK_ST + 2 * S5_BLOCK_IN, 2 * S5_BLOCK_IN), BF16)],
        compiler_params=pltpu.CompilerParams(vmem_limit_bytes=VMEM_LIMIT),
        name="s5_prep",
    )(lre, lim, ldt, bre_t, bim_t, lre_t, lim_t, ldt_t, cre_t, cim_t)


def _ml_prep_kernel(wq_ref, wk_ref, wv_ref, oq_ref, ok_ref, ov_ref):
    shape = (MXU_TILE, MXU_TILE)
    shift = ML_QKV_BLOCK.bit_length() - 1
    row = lax.broadcasted_iota(jnp.int32, shape, 0)
    col = lax.broadcasted_iota(jnp.int32, shape, 1)
    same = (row >> shift) == (col >> shift)
    sel = col & (ML_QKV_BLOCK - 1)
    for w_ref, o_ref in ((wq_ref, oq_ref), (wk_ref, ok_ref), (wv_ref, ov_ref)):
        for c in range(D_MODEL // MXU_TILE):
            w = w_ref[c * MXU_TILE:(c + 1) * MXU_TILE, :]
            acc = jnp.zeros(shape, F32)
            for o in range(ML_QKV_BLOCK):
                acc = jnp.where(sel == o, w[:, o:o + 1], acc)
            o_ref[c] = jnp.where(same, acc, 0.0).astype(BF16)


def _ml_prep(wq, wk, wv):
    out = jax.ShapeDtypeStruct((D_MODEL // MXU_TILE, MXU_TILE, MXU_TILE), BF16)
    return pl.pallas_call(_ml_prep_kernel, out_shape=[out] * 3, name="ml_prep")(wq, wk, wv)


def _block_diag_dot(x, w_ref):
    return jnp.concatenate(
        [jnp.dot(x[:, c * MXU_TILE:(c + 1) * MXU_TILE], w_ref[c], preferred_element_type=F32)
         for c in range(w_ref.shape[0])], axis=1)


def _s5_kernel(u_ref, sre0_ref, sim0_ref, pre_ref, pim_ref, bcat_ref, wout_ref, d_ref, gw_ref, gb_ref, on_ref,
               y_ref, sre_ref, sim_ref, s_ref, *, tm, independent):
    r = tm // SUBLANES
    nc = r // 2
    half = nc * SUBLANES
    t = pl.program_id(1)

    @pl.when(t == 0)
    def _():
        sre_ref[...] = sre0_ref[...]
        sim_ref[...] = sim0_ref[...]

    up = u_ref[...]
    u3 = up.reshape(nc, 2 * SUBLANES, D_MODEL)
    u0 = u3[:, 0:SUBLANES, :].reshape(half, D_MODEL).astype(BF16)
    u1 = u3[:, SUBLANES:2 * SUBLANES, :].reshape(half, D_MODEL).astype(BF16)

    y0s, y1s = [], []
    for b in range(S5_BLOCKS):
        lanes = slice(b * S5_BLOCK_ST, (b + 1) * S5_BLOCK_ST)
        chans = slice(b * S5_BLOCK_IN, (b + 1) * S5_BLOCK_IN)
        re = slice(0, S5_BLOCK_ST)
        im = slice(S5_BLOCK_ST, 2 * S5_BLOCK_ST)
        ucat = jnp.concatenate([u0[:, chans], u1[:, chans]], axis=1)
        s_ref[...] = jnp.dot(ucat, bcat_ref[b], preferred_element_type=F32)
        l2r = jnp.broadcast_to(pre_ref[1:2, lanes], (SUBLANES, S5_BLOCK_ST))
        l2i = jnp.broadcast_to(pim_ref[1:2, lanes], (SUBLANES, S5_BLOCK_ST))

        def scan(cr, ci):
            for c in range(nc):
                rows = slice(c * SUBLANES, (c + 1) * SUBLANES)
                nr = l2r * cr - l2i * ci + s_ref[rows, re]
                ni = l2r * ci + l2i * cr + s_ref[rows, im]
                s_ref[rows, re] = cr
                s_ref[rows, im] = ci
                cr, ci = nr, ni
            return cr, ci

        if independent:
            fr, fi = scan(sre_ref[:, lanes], sim_ref[:, lanes])
            sre_ref[:, lanes] = fr
            sim_ref[:, lanes] = fi
        else:
            zero = jnp.zeros((SUBLANES, S5_BLOCK_ST), F32)
            fr, fi = scan(zero, zero)
            rr = pre_ref[r - 1:r, lanes]
            ri = pim_ref[r - 1:r, lanes]
            cr = sre_ref[:, lanes]
            ci = sim_ref[:, lanes]
            rows_r, rows_i = [], []
            for a in range(SUBLANES):
                rows_r.append(cr)
                rows_i.append(ci)
                cr, ci = rr * cr - ri * ci + fr[a:a + 1], rr * ci + ri * cr + fi[a:a + 1]
            sre_ref[:, lanes] = cr
            sim_ref[:, lanes] = ci
            cin_r = jnp.concatenate(rows_r, axis=0)
            cin_i = jnp.concatenate(rows_i, axis=0)
            s_ref[0:SUBLANES, re] = cin_r
            s_ref[0:SUBLANES, im] = cin_i
            for c in range(1, nc):
                rows = slice(c * SUBLANES, (c + 1) * SUBLANES)
                pr = pre_ref[2 * c - 1:2 * c, lanes]
                pi = pim_ref[2 * c - 1:2 * c, lanes]
                s_ref[rows, re] += pr * cin_r - pi * cin_i
                s_ref[rows, im] += pr * cin_i + pi * cin_r
        yb = jnp.dot(jnp.concatenate([s_ref[...].astype(BF16), ucat], axis=1), wout_ref[b],
                     preferred_element_type=F32)
        y0s.append(yb[:, 0:S5_BLOCK_IN])
        y1s.append(yb[:, S5_BLOCK_IN:2 * S5_BLOCK_IN])

    y0 = jnp.concatenate(y0s, axis=1).reshape(nc, SUBLANES, D_MODEL)
    y1 = jnp.concatenate(y1s, axis=1).reshape(nc, SUBLANES, D_MODEL)
    y = jnp.concatenate([y0, y1], axis=1).reshape(tm, D_MODEL) + d_ref[...] * up
    g = jax.nn.gelu(y)
    o = g * jax.nn.sigmoid(jnp.dot(g.astype(BF16), gw_ref[...], preferred_element_type=F32) + gb_ref[...])
    y_ref[...] = _rms(o, on_ref[...]).astype(BF16)


def _s5(u, sre0, sim0, consts, *, n_seq, tiles, tm, row_block0=0, independent=False, shared_init=False):
    srows = SUBLANES if independent else 1
    seq_spec = pl.BlockSpec((tm, D_MODEL), lambda i, t: (row_block0 + i * tiles + t, 0))
    out_seq_spec = pl.BlockSpec((tm, D_MODEL), lambda i, t: (i * tiles + t, 0))
    st_in = pl.BlockSpec((None, srows, S5_LANES), (lambda i, t: (0, 0, 0)) if shared_init else (lambda i, t: (i, 0, 0)))
    st_out = pl.BlockSpec((None, srows, S5_LANES), lambda i, t: (i, 0, 0))
    st_shape = jax.ShapeDtypeStruct((n_seq, srows, S5_LANES), F32)
    return pl.pallas_call(
        functools.partial(_s5_kernel, tm=tm, independent=independent),
        grid=(n_seq, tiles),
        in_specs=[seq_spec, st_in, st_in] + [_const_spec(a.shape) for a in consts],
        out_specs=[out_seq_spec, st_out, st_out],
        out_shape=[jax.ShapeDtypeStruct((n_seq * tiles * tm, D_MODEL), BF16), st_shape, st_shape],
        scratch_shapes=[pltpu.VMEM((tm // 2, 2 * S5_BLOCK_ST), F32)],
        compiler_params=pltpu.CompilerParams(dimension_semantics=("parallel", "arbitrary"),
                                             vmem_limit_bytes=VMEM_LIMIT),
        name="s5_mix",
    )(u, sre0, sim0, *consts)


def _ml_front(xm_ref, cv_ref, xbuf_ref, consts, bufs, tm):
    cw_ref, cb_ref, bdq_ref, bdk_ref, bdv_ref, gw_ref, gb_ref = consts
    xc_ref, q_ref, k_ref, v_ref, vf_ref, g_ref = bufs
    pre = ML_CONV - 1
    xm = xm_ref[...]
    xbuf_ref[SUBLANES - pre:SUBLANES, :] = cv_ref[...]
    xbuf_ref[SUBLANES:SUBLANES + tm, :] = xm
    cw = cw_ref[...]
    xc = cb_ref[...] + cw[pre:pre + 1] * xm
    for j in range(pre):
        xc = xc + cw[j:j + 1] * xbuf_ref[SUBLANES - pre + j:SUBLANES - pre + j + tm, :]
    cv_ref[...] = xbuf_ref[SUBLANES + tm - pre:SUBLANES + tm, :]
    xc = _silu(xc)
    xc_ref[...] = xc

    xcb = xc.astype(BF16)
    q = _block_diag_dot(xcb, bdq_ref)
    k = _block_diag_dot(xcb, bdk_ref)
    v = _block_diag_dot(xm.astype(BF16), bdv_ref)
    qb = q.astype(BF16)
    vb = v.astype(BF16)
    g_ref[...] = jnp.dot(jnp.concatenate([qb, k.astype(BF16), vb], axis=1), gw_ref[...],
                         preferred_element_type=F32) + gb_ref[...]
    q_ref[...] = qb
    k_ref[...] = (k * (ML_HEAD_DIM ** -0.5)).astype(BF16)
    v_ref[...] = vb
    vf_ref[...] = v


def _ml_back(bufs, z_ref, state, consts, y_ref, h_ref, tm, tc):
    xc_ref, q_ref, k_ref, v_ref, vf_ref, g_ref = bufs
    c_ref, n_ref, m_ref = state
    nw_ref, sk_ref, on_ref = consts
    lane = lax.broadcasted_iota(jnp.int32, (tc, GATE_LANES), 1)
    row = lax.broadcasted_iota(jnp.int32, (tc, tc), 0)
    col = lax.broadcasted_iota(jnp.int32, (tc, tc), 1)
    causal = row >= col
    tril = causal.astype(F32)

    def chunk(rows):
        gates = g_ref[rows, :]
        lf = jnp.minimum(gates, 0.0) - jnp.log1p(jnp.exp(-jnp.abs(gates)))
        lf = jnp.where((lane >= ML_HEADS) & (lane < 2 * ML_HEADS), lf, 0.0)
        cum = jnp.dot(tril, lf, preferred_element_type=F32, precision=lax.Precision.HIGHEST)
        arr = jnp.where(lane < ML_HEADS, gates, cum)
        arr_t = arr.T
        for h in range(ML_HEADS):
            hs = slice(h * ML_HEAD_DIM, (h + 1) * ML_HEAD_DIM)
            qh = q_ref[rows, hs]
            kh = k_ref[rows, hs]
            vh = v_ref[rows, hs]
            ig_col = arr[:, h:h + 1]
            b_col = arr[:, ML_HEADS + h:ML_HEADS + h + 1]
            ig_row = arr_t[h:h + 1, :]
            b_row = arr_t[ML_HEADS + h:ML_HEADS + h + 1, :]
            m_prev = m_ref[:, h:h + 1]
            c_prev = c_ref[h]
            n_prev = n_ref[h:h + 1, :]

            logw = jnp.where(causal, b_col - b_row + ig_row, -jnp.inf)
            log_inter = b_col + m_prev
            m_t = jnp.maximum(log_inter, jnp.max(logw, axis=-1, keepdims=True))
            w = jnp.exp(logw - m_t)
            a_inter = jnp.exp(log_inter - m_t)
            s = lax.dot_general(qh, kh, (((1,), (1,)), ((), ())), preferred_element_type=F32) * w
            inter = lax.dot_general(qh, c_prev.astype(BF16), (((1,), (1,)), ((), ())),
                                    preferred_element_type=F32)
            num = a_inter * inter + jnp.dot(s.astype(BF16), vh, preferred_element_type=F32)
            den = (a_inter * jnp.sum(qh.astype(F32) * n_prev, axis=-1, keepdims=True)
                   + jnp.sum(s, axis=-1, keepdims=True))
            hh = num / jnp.maximum(jnp.abs(den), jnp.exp(-m_t))
            mu = jnp.mean(hh, axis=-1, keepdims=True)
            hc = hh - mu
            var = jnp.mean(hc * hc, axis=-1, keepdims=True)
            h_ref[rows, hs] = hc * lax.rsqrt(var + EPS)

            b_last = b_col[tc - 1:tc, :]
            m_new = m_t[tc - 1:tc, :]
            g_state = jnp.exp(b_last + m_prev - m_new)
            g_src = jnp.exp(b_last - b_col + ig_col - m_new)
            vs = (vf_ref[rows, hs] * g_src).astype(BF16)
            c_ref[h] = g_state * c_prev + lax.dot_general(vs, kh, (((0,), (0,)), ((), ())),
                                                          preferred_element_type=F32)
            n_ref[h:h + 1, :] = g_state * n_prev + jnp.sum(g_src * kh.astype(F32), axis=0, keepdims=True)
            m_ref[:, h:h + 1] = m_new

    for j in range(tm // tc):
        chunk(slice(j * tc, (j + 1) * tc))

    out = (h_ref[...] * nw_ref[...] + sk_ref[...] * xc_ref[...]) * _silu(z_ref[...])
    y_ref[...] = _rms(out, on_ref[...]).astype(BF16)


def _mlstm_kernel(xm_ref, z_ref, c0_ref, n0_ref, m0_ref, cv0_ref, cw_ref, cb_ref, bdq_ref, bdk_ref, bdv_ref,
                  gw_ref, gb_ref, nw_ref, sk_ref, on_ref,
                  y_ref, c_ref, n_ref, m_ref, cv_ref, xbuf_ref, h_ref, *bufs, tm, tc, pipelined):
    t = pl.program_id(1)
    front_consts = (cw_ref, cb_ref, bdq_ref, bdk_ref, bdv_ref, gw_ref, gb_ref)
    back_consts = (nw_ref, sk_ref, on_ref)
    state = (c_ref, n_ref, m_ref)
    n_buf = len(bufs) // 2 if pipelined else len(bufs)

    @pl.when(t == 0)
    def _():
        cv_ref[...] = cv0_ref[...]
        if pipelined:
            for ref in bufs[n_buf:]:
                ref[...] = jnp.zeros(ref.shape, ref.dtype)

    @pl.when(t <= (1 if pipelined else 0))
    def _():
        c_ref[...] = c0_ref[...]
        n_ref[...] = n0_ref[...]
        m_ref[...] = m0_ref[...]

    def step(front_set, back_set):
        _ml_front(xm_ref, cv_ref, xbuf_ref, front_consts, front_set, tm)
        _ml_back(back_set, z_ref, state, back_consts, y_ref, h_ref, tm, tc)

    if pipelined:
        pl.when(t % 2 == 0)(lambda: step(bufs[:n_buf], bufs[n_buf:]))
        pl.when(t % 2 == 1)(lambda: step(bufs[n_buf:], bufs[:n_buf]))
    else:
        step(bufs, bufs)


def _mlstm(xm, z, c0, n0, m0, cv0, consts, *, n_seq, tiles, tm, tc, row_block0=0, shared_init=False,
           pipelined=False):
    if pipelined:
        steps = tiles + 1
        front_spec = pl.BlockSpec((tm, D_MODEL),
                                  lambda i, t: (row_block0 + i * tiles + jnp.minimum(t, tiles - 1), 0))
        back_spec = pl.BlockSpec((tm, D_MODEL),
                                 lambda i, t: (row_block0 + i * tiles + jnp.maximum(t - 1, 0), 0))
        out_seq_spec = pl.BlockSpec((tm, D_MODEL), lambda i, t: (i * tiles + jnp.maximum(t - 1, 0), 0))
    else:
        steps = tiles
        front_spec = back_spec = pl.BlockSpec((tm, D_MODEL), lambda i, t: (row_block0 + i * tiles + t, 0))
        out_seq_spec = pl.BlockSpec((tm, D_MODEL), lambda i, t: (i * tiles + t, 0))

    def st(shape, shared):
        nd = len(shape)
        return pl.BlockSpec((None,) + shape,
                            (lambda i, t: (0,) * (nd + 1)) if shared else (lambda i, t: (i,) + (0,) * nd))

    st_shapes = ((ML_HEADS, ML_HEAD_DIM, ML_HEAD_DIM), (ML_HEADS, ML_HEAD_DIM), (1, ML_HEADS),
                 (ML_CONV - 1, D_MODEL))
    buf_set = [pltpu.VMEM((tm, D_MODEL), F32), pltpu.VMEM((tm, D_MODEL), BF16), pltpu.VMEM((tm, D_MODEL), BF16),
               pltpu.VMEM((tm, D_MODEL), BF16), pltpu.VMEM((tm, D_MODEL), F32), pltpu.VMEM((tm, GATE_LANES), F32)]
    return pl.pallas_call(
        functools.partial(_mlstm_kernel, tm=tm, tc=tc, pipelined=pipelined),
        grid=(n_seq, steps),
        in_specs=[front_spec, back_spec] + [st(s, shared_init) for s in st_shapes]
                 + [_const_spec(a.shape) for a in consts],
        out_specs=[out_seq_spec] + [st(s, False) for s in st_shapes],
        out_shape=[jax.ShapeDtypeStruct((n_seq * tiles * tm, D_MODEL), BF16)]
                  + [jax.ShapeDtypeStruct((n_seq,) + s, F32) for s in st_shapes],
        scratch_shapes=[pltpu.VMEM((tm + 2 * SUBLANES, D_MODEL), F32), pltpu.VMEM((tm, D_MODEL), F32)]
                       + buf_set * (2 if pipelined else 1),
        compiler_params=pltpu.CompilerParams(dimension_semantics=("parallel", "arbitrary"),
                                             vmem_limit_bytes=VMEM_LIMIT),
        name="mlstm_mix",
    )(xm, z, c0, n0, m0, cv0, *consts)


def _row(v):
    return v.reshape(1, -1).astype(F32)


def _pad_lanes(v, width):
    return jnp.pad(v, [(0, 0)] * (v.ndim - 1) + [(0, width - v.shape[-1])])


def kernel(x_prompt, x_sample, state_s5_re, state_s5_im, state_mlstm_c, state_mlstm_n, state_mlstm_m,
           state_mlstm_conv, meta_tokens, norm_ffn1, ffn1_gate, ffn1_up, ffn1_down, norm_mix, w_in,
           s5_lambda_re, s5_lambda_im, s5_log_dt, s5_b_re, s5_b_im, s5_c_re, s5_c_im, s5_d, s5_glu_w, s5_glu_b,
           ml_conv_w, ml_conv_b, ml_wq, ml_wk, ml_wv, ml_igate_w, ml_igate_b, ml_fgate_w, ml_fgate_b,
           ml_norm_w, ml_skip, out_norm_s5, out_norm_ml, w_out, norm_ffn2, ffn2_gate, ffn2_up, ffn2_down,
           norm_final):
    nb, seq, _ = x_prompt.shape
    ns, dseq, _ = x_sample.shape
    n_meta = meta_tokens.shape[0]
    n_p, n_s = nb * seq, ns * dseq
    tile_p = min(TOKEN_TILE, seq)
    chunk_p = min(ML_CHUNK, tile_p)

    wg1, wu1, wd1 = ffn1_gate[0].astype(BF16), ffn1_up[0].astype(BF16), ffn1_down[0].astype(BF16)
    wg2, wu2, wd2 = ffn2_gate[0].astype(BF16), ffn2_up[0].astype(BF16), ffn2_down[0].astype(BF16)
    win = w_in[0].astype(BF16)
    wo = w_out[0].astype(BF16)
    glu_w = s5_glu_w[0].astype(BF16)

    ldt_gs = jnp.broadcast_to(s5_log_dt[0][:, None], (S5_GROUPS, S5_STATE))

    def state_major(a):
        return jnp.repeat(a.T, S5_GROUP, axis=1)

    pre, pim, bcat, wout = _s5_prep(
        s5_lambda_re[0].reshape(1, S5_LANES), s5_lambda_im[0].reshape(1, S5_LANES), ldt_gs.reshape(1, S5_LANES),
        s5_b_re[0].reshape(S5_LANES, S5_GROUP).T, s5_b_im[0].reshape(S5_LANES, S5_GROUP).T,
        state_major(s5_lambda_re[0]), state_major(s5_lambda_im[0]), state_major(ldt_gs),
        s5_c_re[0].reshape(D_MODEL, S5_STATE).T, s5_c_im[0].reshape(D_MODEL, S5_STATE).T)
    s5_consts = (pre, pim, bcat, wout, _row(s5_d[0]), glu_w, _row(s5_glu_b[0]), _row(out_norm_s5[0]))

    bdq, bdk, bdv = _ml_prep(ml_wq[0].reshape(D_MODEL, ML_QKV_BLOCK), ml_wk[0].reshape(D_MODEL, ML_QKV_BLOCK),
                             ml_wv[0].reshape(D_MODEL, ML_QKV_BLOCK))
    gate_w = _pad_lanes(jnp.concatenate([ml_igate_w[0], ml_fgate_w[0]], axis=1), GATE_LANES).astype(BF16)
    gate_b = _pad_lanes(jnp.concatenate([ml_igate_b[0], ml_fgate_b[0]])[None, :], GATE_LANES)
    ml_consts = (ml_conv_w[0], _row(ml_conv_b[0]), bdq, bdk, bdv, gate_w, gate_b, _row(ml_norm_w[0]),
                 _row(ml_skip[0]), _row(out_norm_ml[0]))

    ffn1 = (_row(norm_ffn1[0]), wg1, wu1, wd1, _row(norm_mix[0]), win)
    seg_p = ((0, tile_p // SUBLANES),)
    seg_s = tuple((g * SUBLANES * dseq, dseq) for g in range(ns // SUBLANES))
    seg_m = ((n_s, n_meta // SUBLANES),)
    x1_p, u_p, xm_p, z_p = _ffn_in(x_prompt.reshape(n_p, D_MODEL), *ffn1, tm=tile_p, seg_layout=seg_p)
    small = jnp.concatenate([x_sample.reshape(n_s, D_MODEL), meta_tokens], axis=0)
    x1_s, u_s, xm_s, z_s = _ffn_in(small, *ffn1, tm=small.shape[0], seg_layout=seg_s + seg_m)

    zs5 = jnp.zeros((1, 1, S5_LANES), F32)
    _, mre, mim = _s5(u_s, zs5, zs5, s5_consts, n_seq=1, tiles=1, tm=n_meta, row_block0=n_s // n_meta)
    y5_p, pre_s, pim_s = _s5(u_p, mre, mim, s5_consts, n_seq=nb, tiles=seq // tile_p, tm=tile_p,
                             shared_init=True)
    y5_s, sre_s, sim_s = _s5(u_s, state_s5_re[0].reshape(ns // SUBLANES, SUBLANES, S5_LANES),
                             state_s5_im[0].reshape(ns // SUBLANES, SUBLANES, S5_LANES), s5_consts,
                             n_seq=ns // SUBLANES, tiles=1, tm=SUBLANES * dseq, independent=True)

    zc = jnp.zeros((1, ML_HEADS, ML_HEAD_DIM, ML_HEAD_DIM), F32)
    zn = jnp.zeros((1, ML_HEADS, ML_HEAD_DIM), F32)
    zm = jnp.zeros((1, 1, ML_HEADS), F32)
    zcv = jnp.zeros((1, ML_CONV - 1, D_MODEL), F32)
    _, c_m, n_m, m_m, cv_m = _mlstm(xm_s, z_s, zc, zn, zm, zcv, ml_consts, n_seq=1, tiles=1, tm=n_meta,
                                    tc=n_meta, row_block0=n_s // n_meta)
    ym_p, c_p, nn_p, m_p, cv_p = _mlstm(xm_p, z_p, c_m, n_m, m_m, cv_m, ml_consts, n_seq=nb,
                                        tiles=seq // tile_p, tm=tile_p, tc=chunk_p, shared_init=True,
                                        pipelined=True)
    ym_s, c_s, nn_s, m_s, cv_s = _mlstm(xm_s, z_s, state_mlstm_c[0], state_mlstm_n[0],
                                        state_mlstm_m[0][:, None, :], state_mlstm_conv[0], ml_consts,
                                        n_seq=ns, tiles=1, tm=dseq, tc=dseq)

    ffn2 = (wo, _row(norm_ffn2[0]), wg2, wu2, wd2, _row(norm_final))
    y_p = _out_ffn(x1_p, y5_p, ym_p, *ffn2, rows=n_p, tm=tile_p, seg_layout=seg_p)
    y_s = _out_ffn(x1_s, y5_s, ym_s, *ffn2, rows=n_s, tm=n_s, seg_layout=seg_s)

    def s5_state(s, n):
        return s.reshape(1, n, S5_GROUPS, S5_STATE)

    return (y_p.reshape(nb, seq, D_MODEL), y_s.reshape(ns, dseq, D_MODEL),
            s5_state(pre_s, nb), s5_state(pim_s, nb), c_p[None], nn_p[None], m_p[:, 0][None], cv_p[None],
            s5_state(sre_s, ns), s5_state(sim_s, ns), c_s[None], nn_s[None], m_s[:, 0][None], cv_s[None])
```

```python
import functools

import jax
import jax.numpy as jnp
from jax import lax
from jax.experimental import pallas as pl
from jax.experimental.pallas import tpu as pltpu

F32 = jnp.float32
BF16 = jnp.bfloat16

D_MODEL = 1024
D_FF = 2816
S5_GROUPS = 64
S5_GROUP = 16
S5_STATE = 64
S5_LANES = S5_GROUPS * S5_STATE
MXU_TILE = 256
S5_BLOCKS = 8
S5_BLOCK_GROUPS = S5_GROUPS // S5_BLOCKS
S5_BLOCK_IN = D_MODEL // S5_BLOCKS
S5_BLOCK_ST = S5_LANES // S5_BLOCKS
ML_HEADS = 4
ML_HEAD_DIM = 256
ML_CONV = 4
ML_QKV_BLOCK = 4
EPS = 1e-6
SUBLANES = 8
GATE_LANES = 128
POW_ROWS = 64
S5_T = 4
VMEM_LIMIT = 56 * 1024 * 1024

TOKEN_TILE = 512
ML_CHUNK = 256


def _rms(x, g):
    return x * lax.rsqrt(jnp.mean(x * x, axis=-1, keepdims=True) + EPS) * g


def _silu(x):
    return x * jax.nn.sigmoid(x)


def _swiglu(h, wg_ref, wu_ref, wd_ref, acc_ref, ff_chunk):
    for c in range(D_FF // ff_chunk):
        sl = slice(c * ff_chunk, (c + 1) * ff_chunk)
        g = jnp.dot(h, wg_ref[:, sl], preferred_element_type=F32)
        u = jnp.dot(h, wu_ref[:, sl], preferred_element_type=F32)
        a = (_silu(g) * u).astype(BF16)
        d = jnp.dot(a, wd_ref[sl, :], preferred_element_type=F32)
        if c == 0:
            acc_ref[...] = d
        else:
            acc_ref[...] += d
    return acc_ref[...]


def _const_spec(shape):
    nd = len(shape)
    return pl.BlockSpec(shape, lambda *_: (0,) * nd, pipeline_mode=pl.Buffered(1))


def _segment_rows(x, layout, inverse=False):
    parts = []
    for row0, r in layout:
        n = SUBLANES * r
        g = x[row0:row0 + n]
        if r % SUBLANES == 0:
            shape = (r, SUBLANES) if inverse else (SUBLANES, r)
            g = jnp.swapaxes(g.reshape(shape + g.shape[1:]), 0, 1).reshape(g.shape)
        elif inverse:
            g = jnp.concatenate([g[i * SUBLANES + a:i * SUBLANES + a + 1]
                                 for a in range(SUBLANES) for i in range(r)], axis=0)
        else:
            g = jnp.concatenate([g[a * r + i:a * r + i + 1]
                                 for i in range(r) for a in range(SUBLANES)], axis=0)
        parts.append(g)
    return parts[0] if len(parts) == 1 else jnp.concatenate(parts, axis=0)


def _ffn_in_kernel(x_ref, g1_ref, wg_ref, wu_ref, wd_ref, g2_ref, win_ref,
                   x1_ref, u_ref, xm_ref, z_ref, acc_ref, *, ff_chunk, seg_layout):
    x = x_ref[...]
    h = _rms(x, g1_ref[...]).astype(BF16)
    x1 = x + 0.5 * _swiglu(h, wg_ref, wu_ref, wd_ref, acc_ref, ff_chunk)
    x1_ref[...] = x1
    h2f = _rms(x1, g2_ref[...])
    h2 = h2f.astype(BF16)
    h2s = _segment_rows(h2f, seg_layout).astype(BF16)
    u_ref[...] = jnp.dot(h2s, win_ref[:, 0:D_MODEL], preferred_element_type=F32)
    xm_ref[...] = jnp.dot(h2, win_ref[:, D_MODEL:2 * D_MODEL], preferred_element_type=F32)
    z_ref[...] = jnp.dot(h2, win_ref[:, 2 * D_MODEL:3 * D_MODEL], preferred_element_type=F32)


def _ffn_in(x, g1, wg, wu, wd, g2, win, *, tm, seg_layout, ff_chunk=256):
    rows = x.shape[0]
    row_spec = pl.BlockSpec((tm, D_MODEL), lambda i: (i, 0))
    out = jax.ShapeDtypeStruct((rows, D_MODEL), F32)
    return pl.pallas_call(
        functools.partial(_ffn_in_kernel, ff_chunk=ff_chunk, seg_layout=seg_layout),
        grid=(rows // tm,),
        in_specs=[row_spec, _const_spec(g1.shape), _const_spec(wg.shape), _const_spec(wu.shape),
                  _const_spec(wd.shape), _const_spec(g2.shape), _const_spec(win.shape)],
        out_specs=[row_spec] * 4,
        out_shape=[out] * 4,
        scratch_shapes=[pltpu.VMEM((tm, D_MODEL), F32)],
        compiler_params=pltpu.CompilerParams(dimension_semantics=("parallel",), vmem_limit_bytes=VMEM_LIMIT),
        name="ffn_in",
    )(x, g1, wg, wu, wd, g2, win)


def _out_ffn_kernel(x1_ref, y5_ref, ym_ref, wo_ref, g_ref, wg_ref, wu_ref, wd_ref, gf_ref,
                    o_ref, acc_ref, *, ff_chunk, seg_layout):
    p5 = jnp.dot(y5_ref[...], wo_ref[0:D_MODEL, :], preferred_element_type=F32)
    x2 = (x1_ref[...] + _segment_rows(p5, seg_layout, inverse=True)
          + jnp.dot(ym_ref[...], wo_ref[D_MODEL:2 * D_MODEL, :], preferred_element_type=F32))
    h = _rms(x2, g_ref[...]).astype(BF16)
    x3 = x2 + 0.5 * _swiglu(h, wg_ref, wu_ref, wd_ref, acc_ref, ff_chunk)
    o_ref[...] = _rms(x3, gf_ref[...])


def _out_ffn(x1, y5, ym, wo, g, wg, wu, wd, gf, *, rows, tm, seg_layout, ff_chunk=256):
    row_spec = pl.BlockSpec((tm, D_MODEL), lambda i: (i, 0))
    return pl.pallas_call(
        functools.partial(_out_ffn_kernel, ff_chunk=ff_chunk, seg_layout=seg_layout),
        grid=(rows // tm,),
        in_specs=[row_spec, row_spec, row_spec, _const_spec(wo.shape), _const_spec(g.shape),
                  _const_spec(wg.shape), _const_spec(wu.shape), _const_spec(wd.shape), _const_spec(gf.shape)],
        out_specs=row_spec,
        out_shape=jax.ShapeDtypeStruct((rows, D_MODEL), F32),
        scratch_shapes=[pltpu.VMEM((tm, D_MODEL), F32)],
        compiler_params=pltpu.CompilerParams(dimension_semantics=("parallel",), vmem_limit_bytes=VMEM_LIMIT),
        name="out_ffn",
    )(x1, y5, ym, wo, g, wg, wu, wd, gf)


def _lam_bar(lr, li, ldt):
    dt = jnp.exp(ldt)
    mag = jnp.exp(lr * dt)
    th = li * dt
    return mag * jnp.cos(th), mag * jnp.sin(th), dt


def _cmul(ar, ai, br, bi):
    return ar * br - ai * bi, ar * bi + ai * br


def _dot_split(a, b):
    a_hi = a.astype(BF16)
    b_hi = b.astype(BF16)
    a_lo = (a - a_hi.astype(F32)).astype(BF16)
    b_lo = (b - b_hi.astype(F32)).astype(BF16)
    return (jnp.dot(a_hi, b_hi, preferred_element_type=F32) + jnp.dot(a_hi, b_lo, preferred_element_type=F32)
            + jnp.dot(a_lo, b_hi, preferred_element_type=F32))


def _s5_prep_kernel(lre_ref, lim_ref, ldt_ref, bre_ref, bim_ref, lre_t_ref, lim_t_ref, ldt_t_ref, cre_ref, cim_ref,
                    pre_ref, pim_ref, bcat_ref, wout_ref):
    lr = lre_ref[...]
    li = lim_ref[...]
    ar, ai, _ = _lam_bar(lr, li, ldt_ref[...])
    nr = ar - 1.0
    den = lr * lr + li * li
    cr = (nr * lr + ai * li) / den
    ci = (ai * lr - nr * li) / den
    bs = [_cmul(cr, ci, bre_ref[...], bim_ref[...])]
    for _ in range(1, S5_T):
        bs.append(_cmul(ar, ai, *bs[-1]))

    pr, pi = ar, ai
    pre_ref[0:1, :] = pr
    pim_ref[0:1, :] = pi
    for j in range(1, POW_ROWS):
        pr, pi = _cmul(pr, pi, ar, ai)
        pre_ref[j:j + 1, :] = pr
        pim_ref[j:j + 1, :] = pi

    tr, ti, _ = _lam_bar(lre_t_ref[...], lim_t_ref[...], ldt_t_ref[...])
    cs = [(cre_ref[...], cim_ref[...])]
    for _ in range(S5_T):
        cs.append(_cmul(tr, ti, *cs[-1]))

    g_shift, s_shift = S5_GROUP.bit_length() - 1, S5_STATE.bit_length() - 1
    b_shape = (S5_BLOCK_IN, S5_BLOCK_ST)
    b_mask = (lax.broadcasted_iota(jnp.int32, b_shape, 0) >> g_shift) == (
        lax.broadcasted_iota(jnp.int32, b_shape, 1) >> s_shift)
    c_shape = (S5_BLOCK_ST, S5_BLOCK_IN)
    c_mask = (lax.broadcasted_iota(jnp.int32, c_shape, 0) >> s_shift) == (
        lax.broadcasted_iota(jnp.int32, c_shape, 1) >> g_shift)

    def b_block(br, bi, blk):
        parts = []
        for x in (br, bi):
            x = jnp.concatenate([x[:, blk * S5_BLOCK_ST:(blk + 1) * S5_BLOCK_ST]] * S5_BLOCK_GROUPS, axis=0)
            parts.append(jnp.where(b_mask, x, 0.0))
        return jnp.concatenate(parts, axis=1)

    def c_block(xr, xi, blk):
        parts = []
        for x in (xr, -xi):
            x = jnp.concatenate([x[:, blk * S5_BLOCK_IN:(blk + 1) * S5_BLOCK_IN]] * S5_BLOCK_GROUPS, axis=0)
            parts.append(jnp.where(c_mask, x, 0.0))
        return jnp.concatenate(parts, axis=0)

    zero = jnp.zeros((S5_BLOCK_IN, S5_BLOCK_IN), F32)
    z_rows = 2 * S5_BLOCK_ST
    for blk in range(S5_BLOCKS):
        bms = [b_block(br, bi, blk) for br, bi in bs]
        cms = [c_block(xr, xi, blk) for xr, xi in cs]
        kcat = _dot_split(bms[0], jnp.concatenate(cms[:S5_T], axis=1))
        ks = [kcat[:, j * S5_BLOCK_IN:(j + 1) * S5_BLOCK_IN] for j in range(S5_T)]
        for f in range(S5_T):
            bcat_ref[blk, f * S5_BLOCK_IN:(f + 1) * S5_BLOCK_IN, :] = bms[S5_T - 1 - f].astype(BF16)
            row0 = z_rows + f * S5_BLOCK_IN
            wout_ref[blk, row0:row0 + S5_BLOCK_IN, :] = jnp.concatenate(
                [ks[g - f] if g >= f else zero for g in range(S5_T)], axis=1).astype(BF16)
        wout_ref[blk, 0:z_rows, :] = jnp.concatenate(cms[1:], axis=1).astype(BF16)


def _s5_prep(lre, lim, ldt, bre_t, bim_t, lre_t, lim_t, ldt_t, cre_t, cim_t):
    pow_shape = jax.ShapeDtypeStruct((POW_ROWS, S5_LANES), F32)
    return pl.pallas_call(
        _s5_prep_kernel,
        out_shape=[pow_shape, pow_shape,
                   jax.ShapeDtypeStruct((S5_BLOCKS, S5_T * S5_BLOCK_IN, 2 * S5_BLOCK_ST), BF16),
                   jax.ShapeDtypeStruct((S5_BLOCKS, 2 * S5_BLOCK_ST + S5_T * S5_BLOCK_IN, S5_T * S5_BLOCK_IN), BF16)],
        compiler_params=pltpu.CompilerParams(vmem_limit_bytes=VMEM_LIMIT),
        name="s5_prep",
    )(lre, lim, ldt, bre_t, bim_t, lre_t, lim_t, ldt_t, cre_t, cim_t)


def _ml_prep_kernel(wq_ref, wk_ref, wv_ref, oq_ref, ok_ref, ov_ref):
    shape = (MXU_TILE, MXU_TILE)
    shift = ML_QKV_BLOCK.bit_length() - 1
    row = lax.broadcasted_iota(jnp.int32, shape, 0)
    col = lax.broadcasted_iota(jnp.int32, shape, 1)
    same = (row >> shift) == (col >> shift)
    sel = col & (ML_QKV_BLOCK - 1)
    for w_ref, o_ref in ((wq_ref, oq_ref), (wk_ref, ok_ref), (wv_ref, ov_ref)):
        for c in range(D_MODEL // MXU_TILE):
            w = w_ref[c * MXU_TILE:(c + 1) * MXU_TILE, :]
            acc = jnp.zeros(shape, F32)
            for o in range(ML_QKV_BLOCK):
                acc = jnp.where(sel == o, w[:, o:o + 1], acc)
            o_ref[c] = jnp.where(same, acc, 0.0).astype(BF16)


def _ml_prep(wq, wk, wv):
    out = jax.ShapeDtypeStruct((D_MODEL // MXU_TILE, MXU_TILE, MXU_TILE), BF16)
    return pl.pallas_call(_ml_prep_kernel, out_shape=[out] * 3, name="ml_prep")(wq, wk, wv)


def _block_diag_dot(x, w_ref):
    return jnp.concatenate(
        [jnp.dot(x[:, c * MXU_TILE:(c + 1) * MXU_TILE], w_ref[c], preferred_element_type=F32)
         for c in range(w_ref.shape[0])], axis=1)


def _s5_kernel(u_ref, sre0_ref, sim0_ref, pre_ref, pim_ref, bcat_ref, wout_ref, d_ref, gw_ref, gb_ref, on_ref,
               y_ref, sre_ref, sim_ref, s_ref, *, tm, independent, n_tiles):
    r = tm // SUBLANES
    nc = r // S5_T
    rows_c = nc * SUBLANES
    t = pl.program_id(1)

    @pl.when(t == 0)
    def _():
        sre_ref[...] = jnp.broadcast_to(sre0_ref[...], sre_ref.shape)
        sim_ref[...] = jnp.broadcast_to(sim0_ref[...], sim_ref.shape)

    ups = [u_ref[g] for g in range(n_tiles)]
    us = []
    for f in range(S5_T):
        us.append(jnp.concatenate(
            [up.reshape(nc, S5_T * SUBLANES, D_MODEL)[:, f * SUBLANES:(f + 1) * SUBLANES, :].reshape(rows_c, D_MODEL)
             for up in ups], axis=0).astype(BF16))

    ys = [[] for _ in range(S5_T)]
    for b in range(S5_BLOCKS):
        lanes = slice(b * S5_BLOCK_ST, (b + 1) * S5_BLOCK_ST)
        chans = slice(b * S5_BLOCK_IN, (b + 1) * S5_BLOCK_IN)
        re = slice(0, S5_BLOCK_ST)
        im = slice(S5_BLOCK_ST, 2 * S5_BLOCK_ST)
        ucat = jnp.concatenate([u[:, chans] for u in us], axis=1)
        s_ref[...] = jnp.dot(ucat, bcat_ref[b], preferred_element_type=F32)
        ltr = jnp.broadcast_to(pre_ref[S5_T - 1:S5_T, lanes], (SUBLANES, S5_BLOCK_ST))
        lti = jnp.broadcast_to(pim_ref[S5_T - 1:S5_T, lanes], (SUBLANES, S5_BLOCK_ST))

        def scan(cr, ci, row0):
            for c in range(nc):
                rows = slice(row0 + c * SUBLANES, row0 + (c + 1) * SUBLANES)
                nr = ltr * cr - lti * ci + s_ref[rows, re]
                ni = ltr * ci + lti * cr + s_ref[rows, im]
                s_ref[rows, re] = cr
                s_ref[rows, im] = ci
                cr, ci = nr, ni
            return cr, ci

        for g in range(n_tiles):
            row0 = g * rows_c
            if independent:
                fr, fi = scan(sre_ref[g, :, lanes], sim_ref[g, :, lanes], row0)
                sre_ref[g, :, lanes] = fr
                sim_ref[g, :, lanes] = fi
            else:
                zero = jnp.zeros((SUBLANES, S5_BLOCK_ST), F32)
                fr, fi = scan(zero, zero, row0)
                rr = pre_ref[r - 1:r, lanes]
                ri = pim_ref[r - 1:r, lanes]
                cr = sre_ref[g, :, lanes]
                ci = sim_ref[g, :, lanes]
                rows_r, rows_i = [], []
                for a in range(SUBLANES):
                    rows_r.append(cr)
                    rows_i.append(ci)
                    cr, ci = rr * cr - ri * ci + fr[a:a + 1], rr * ci + ri * cr + fi[a:a + 1]
                sre_ref[g, :, lanes] = cr
                sim_ref[g, :, lanes] = ci
                cin_r = jnp.concatenate(rows_r, axis=0)
                cin_i = jnp.concatenate(rows_i, axis=0)
                s_ref[row0:row0 + SUBLANES, re] = cin_r
                s_ref[row0:row0 + SUBLANES, im] = cin_i
                for c in range(1, nc):
                    rows = slice(row0 + c * SUBLANES, row0 + (c + 1) * SUBLANES)
                    pr = pre_ref[S5_T * c - 1:S5_T * c, lanes]
                    pi = pim_ref[S5_T * c - 1:S5_T * c, lanes]
                    s_ref[rows, re] += pr * cin_r - pi * cin_i
                    s_ref[rows, im] += pr * cin_i + pi * cin_r
        yb = jnp.dot(jnp.concatenate([s_ref[...].astype(BF16), ucat], axis=1), wout_ref[b],
                     preferred_element_type=F32)
        for f in range(S5_T):
            ys[f].append(yb[:, f * S5_BLOCK_IN:(f + 1) * S5_BLOCK_IN])

    yf = [jnp.concatenate(ys[f], axis=1) for f in range(S5_T)]
    for g in range(n_tiles):
        rows = slice(g * rows_c, (g + 1) * rows_c)
        y = jnp.concatenate([y[rows].reshape(nc, SUBLANES, D_MODEL) for y in yf], axis=1).reshape(tm, D_MODEL)
        y = y + d_ref[...] * ups[g]
        gl = jax.nn.gelu(y)
        o = gl * jax.nn.sigmoid(jnp.dot(gl.astype(BF16), gw_ref[...], preferred_element_type=F32) + gb_ref[...])
        y_ref[g] = _rms(o, on_ref[...]).astype(BF16)


def _s5(u, sre0, sim0, consts, *, n_outer, n_tiles, steps, tm, independent=False, shared_init=False):
    srows = SUBLANES if independent else 1
    u5 = u.reshape(n_outer, n_tiles, steps, tm, D_MODEL)
    seq_spec = pl.BlockSpec((None, n_tiles, None, tm, D_MODEL), lambda o, t: (o, 0, t, 0, 0))
    if shared_init:
        st_in = pl.BlockSpec((None, 1, srows, S5_LANES), lambda o, t: (0, 0, 0, 0))
    else:
        st_in = pl.BlockSpec((None, n_tiles, srows, S5_LANES), lambda o, t: (o, 0, 0, 0))
    st_out = pl.BlockSpec((None, n_tiles, srows, S5_LANES), lambda o, t: (o, 0, 0, 0))
    st_shape = jax.ShapeDtypeStruct((n_outer, n_tiles, srows, S5_LANES), F32)
    y, sre, sim = pl.pallas_call(
        functools.partial(_s5_kernel, tm=tm, independent=independent, n_tiles=n_tiles),
        grid=(n_outer, steps),
        in_specs=[seq_spec, st_in, st_in] + [_const_spec(a.shape) for a in consts],
        out_specs=[seq_spec, st_out, st_out],
        out_shape=[jax.ShapeDtypeStruct(u5.shape, BF16), st_shape, st_shape],
        scratch_shapes=[pltpu.VMEM((n_tiles * tm // S5_T, 2 * S5_BLOCK_ST), F32)],
        compiler_params=pltpu.CompilerParams(dimension_semantics=("parallel", "arbitrary"),
                                             vmem_limit_bytes=VMEM_LIMIT),
        name="s5_mix",
    )(u5, sre0, sim0, *consts)
    return y.reshape(u.shape), sre, sim


def _ml_front(xm_ref, cv_ref, xbuf_ref, consts, bufs, tm):
    cw_ref, cb_ref, bdq_ref, bdk_ref, bdv_ref, gw_ref, gb_ref = consts
    xc_ref, q_ref, k_ref, v_ref, vf_ref, g_ref = bufs
    pre = ML_CONV - 1
    xm = xm_ref[...]
    xbuf_ref[SUBLANES - pre:SUBLANES, :] = cv_ref[...]
    xbuf_ref[SUBLANES:SUBLANES + tm, :] = xm
    cw = cw_ref[...]
    xc = cb_ref[...] + cw[pre:pre + 1] * xm
    for j in range(pre):
        xc = xc + cw[j:j + 1] * xbuf_ref[SUBLANES - pre + j:SUBLANES - pre + j + tm, :]
    cv_ref[...] = xbuf_ref[SUBLANES + tm - pre:SUBLANES + tm, :]
    xc = _silu(xc)
    xc_ref[...] = xc

    xcb = xc.astype(BF16)
    q = _block_diag_dot(xcb, bdq_ref)
    k = _block_diag_dot(xcb, bdk_ref)
    v = _block_diag_dot(xm.astype(BF16), bdv_ref)
    qb = q.astype(BF16)
    vb = v.astype(BF16)
    g_ref[...] = jnp.dot(jnp.concatenate([qb, k.astype(BF16), vb], axis=1), gw_ref[...],
                         preferred_element_type=F32) + gb_ref[...]
    q_ref[...] = qb
    k_ref[...] = (k * (ML_HEAD_DIM ** -0.5)).astype(BF16)
    v_ref[...] = vb
    vf_ref[...] = v


def _ml_back(bufs, z_ref, state, consts, y_ref, h_ref, tm, tc):
    xc_ref, q_ref, k_ref, v_ref, vf_ref, g_ref = bufs
    c_ref, n_ref, m_ref = state
    nw_ref, sk_ref, on_ref = consts
    lane = lax.broadcasted_iota(jnp.int32, (tc, GATE_LANES), 1)
    row = lax.broadcasted_iota(jnp.int32, (tc, tc), 0)
    col = lax.broadcasted_iota(jnp.int32, (tc, tc), 1)
    causal = row >= col
    tril = causal.astype(F32)

    def chunk(rows):
        gates = g_ref[rows, :]
        lf = jnp.minimum(gates, 0.0) - jnp.log1p(jnp.exp(-jnp.abs(gates)))
        lf = jnp.where((lane >= ML_HEADS) & (lane < 2 * ML_HEADS), lf, 0.0)
        cum = jnp.dot(tril, lf, preferred_element_type=F32, precision=lax.Precision.HIGHEST)
        arr = jnp.where(lane < ML_HEADS, gates, cum)
        arr_t = arr.T
        for h in range(ML_HEADS):
            hs = slice(h * ML_HEAD_DIM, (h + 1) * ML_HEAD_DIM)
            qh = q_ref[rows, hs]
            kh = k_ref[rows, hs]
            vh = v_ref[rows, hs]
            ig_col = arr[:, h:h + 1]
            b_col = arr[:, ML_HEADS + h:ML_HEADS + h + 1]
            ig_row = arr_t[h:h + 1, :]
            b_row = arr_t[ML_HEADS + h:ML_HEADS + h + 1, :]
            m_prev = m_ref[:, h:h + 1]
            c_prev = c_ref[h]
            n_prev = n_ref[h:h + 1, :]

            logw = jnp.where(causal, b_col - b_row + ig_row, -jnp.inf)
            log_inter = b_col + m_prev
            m_t = jnp.maximum(log_inter, jnp.max(logw, axis=-1, keepdims=True))
            w = jnp.exp(logw - m_t)
            a_inter = jnp.exp(log_inter - m_t)
            s = lax.dot_general(qh, kh, (((1,), (1,)), ((), ())), preferred_element_type=F32) * w
            inter = lax.dot_general(qh, c_prev.astype(BF16), (((1,), (1,)), ((), ())),
                                    preferred_element_type=F32)
            num = a_inter * inter + jnp.dot(s.astype(BF16), vh, preferred_element_type=F32)
            den = (a_inter * jnp.sum(qh.astype(F32) * n_prev, axis=-1, keepdims=True)
                   + jnp.sum(s, axis=-1, keepdims=True))
            hh = num / jnp.maximum(jnp.abs(den), jnp.exp(-m_t))
            mu = jnp.mean(hh, axis=-1, keepdims=True)
            hc = hh - mu
            var = jnp.mean(hc * hc, axis=-1, keepdims=True)
            h_ref[rows, hs] = hc * lax.rsqrt(var + EPS)

            b_last = b_col[tc - 1:tc, :]
            m_new = m_t[tc - 1:tc, :]
            g_state = jnp.exp(b_last + m_prev - m_new)
            g_src = jnp.exp(b_last - b_col + ig_col - m_new)
            vs = (vf_ref[rows, hs] * g_src).astype(BF16)
            c_ref[h] = g_state * c_prev + lax.dot_general(vs, kh, (((0,), (0,)), ((), ())),
                                                          preferred_element_type=F32)
            n_ref[h:h + 1, :] = g_state * n_prev + jnp.sum(g_src * kh.astype(F32), axis=0, keepdims=True)
            m_ref[:, h:h + 1] = m_new

    for j in range(tm // tc):
        chunk(slice(j * tc, (j + 1) * tc))

    out = (h_ref[...] * nw_ref[...] + sk_ref[...] * xc_ref[...]) * _silu(z_ref[...])
    y_ref[...] = _rms(out, on_ref[...]).astype(BF16)


def _mlstm_kernel(xm_ref, z_ref, c0_ref, n0_ref, m0_ref, cv0_ref, cw_ref, cb_ref, bdq_ref, bdk_ref, bdv_ref,
                  gw_ref, gb_ref, nw_ref, sk_ref, on_ref,
                  y_ref, c_ref, n_ref, m_ref, cv_ref, xbuf_ref, h_ref, *bufs, tm, tc, pipelined):
    t = pl.program_id(1)
    front_consts = (cw_ref, cb_ref, bdq_ref, bdk_ref, bdv_ref, gw_ref, gb_ref)
    back_consts = (nw_ref, sk_ref, on_ref)
    state = (c_ref, n_ref, m_ref)
    n_buf = len(bufs) // 2 if pipelined else len(bufs)

    @pl.when(t == 0)
    def _():
        cv_ref[...] = cv0_ref[...]
        if pipelined:
            for ref in bufs[n_buf:]:
                ref[...] = jnp.zeros(ref.shape, ref.dtype)

    @pl.when(t <= (1 if pipelined else 0))
    def _():
        c_ref[...] = c0_ref[...]
        n_ref[...] = n0_ref[...]
        m_ref[...] = m0_ref[...]

    def step(front_set, back_set):
        _ml_front(xm_ref, cv_ref, xbuf_ref, front_consts, front_set, tm)
        _ml_back(back_set, z_ref, state, back_consts, y_ref, h_ref, tm, tc)

    if pipelined:
        pl.when(t % 2 == 0)(lambda: step(bufs[:n_buf], bufs[n_buf:]))
        pl.when(t % 2 == 1)(lambda: step(bufs[n_buf:], bufs[:n_buf]))
    else:
        step(bufs, bufs)


def _mlstm(xm, z, c0, n0, m0, cv0, consts, *, n_seq, tiles, tm, tc, row_block0=0, shared_init=False,
           pipelined=False):
    if pipelined:
        steps = tiles + 1
        front_spec = pl.BlockSpec((tm, D_MODEL),
                                  lambda i, t: (row_block0 + i * tiles + jnp.minimum(t, tiles - 1), 0))
        back_spec = pl.BlockSpec((tm, D_MODEL),
                                 lambda i, t: (row_block0 + i * tiles + jnp.maximum(t - 1, 0), 0))
        out_seq_spec = pl.BlockSpec((tm, D_MODEL), lambda i, t: (i * tiles + jnp.maximum(t - 1, 0), 0))
    else:
        steps = tiles
        front_spec = back_spec = pl.BlockSpec((tm, D_MODEL), lambda i, t: (row_block0 + i * tiles + t, 0))
        out_seq_spec = pl.BlockSpec((tm, D_MODEL), lambda i, t: (i * tiles + t, 0))

    def st(shape, shared):
        nd = len(shape)
        return pl.BlockSpec((None,) + shape,
                            (lambda i, t: (0,) * (nd + 1)) if shared else (lambda i, t: (i,) + (0,) * nd))

    st_shapes = ((ML_HEADS, ML_HEAD_DIM, ML_HEAD_DIM), (ML_HEADS, ML_HEAD_DIM), (1, ML_HEADS),
                 (ML_CONV - 1, D_MODEL))
    buf_set = [pltpu.VMEM((tm, D_MODEL), F32), pltpu.VMEM((tm, D_MODEL), BF16), pltpu.VMEM((tm, D_MODEL), BF16),
               pltpu.VMEM((tm, D_MODEL), BF16), pltpu.VMEM((tm, D_MODEL), F32), pltpu.VMEM((tm, GATE_LANES), F32)]
    return pl.pallas_call(
        functools.partial(_mlstm_kernel, tm=tm, tc=tc, pipelined=pipelined),
        grid=(n_seq, steps),
        in_specs=[front_spec, back_spec] + [st(s, shared_init) for s in st_shapes]
                 + [_const_spec(a.shape) for a in consts],
        out_specs=[out_seq_spec] + [st(s, False) for s in st_shapes],
        out_shape=[jax.ShapeDtypeStruct((n_seq * tiles * tm, D_MODEL), BF16)]
                  + [jax.ShapeDtypeStruct((n_seq,) + s, F32) for s in st_shapes],
        scratch_shapes=[pltpu.VMEM((tm + 2 * SUBLANES, D_MODEL), F32), pltpu.VMEM((tm, D_MODEL), F32)]
                       + buf_set * (2 if pipelined else 1),
        compiler_params=pltpu.CompilerParams(dimension_semantics=("parallel", "arbitrary"),
                                             vmem_limit_bytes=VMEM_LIMIT),
        name="mlstm_mix",
    )(xm, z, c0, n0, m0, cv0, *consts)


def _row(v):
    return v.reshape(1, -1).astype(F32)


def _pad_lanes(v, width):
    return jnp.pad(v, [(0, 0)] * (v.ndim - 1) + [(0, width - v.shape[-1])])


def kernel(x_prompt, x_sample, state_s5_re, state_s5_im, state_mlstm_c, state_mlstm_n, state_mlstm_m,
           state_mlstm_conv, meta_tokens, norm_ffn1, ffn1_gate, ffn1_up, ffn1_down, norm_mix, w_in,
           s5_lambda_re, s5_lambda_im, s5_log_dt, s5_b_re, s5_b_im, s5_c_re, s5_c_im, s5_d, s5_glu_w, s5_glu_b,
           ml_conv_w, ml_conv_b, ml_wq, ml_wk, ml_wv, ml_igate_w, ml_igate_b, ml_fgate_w, ml_fgate_b,
           ml_norm_w, ml_skip, out_norm_s5, out_norm_ml, w_out, norm_ffn2, ffn2_gate, ffn2_up, ffn2_down,
           norm_final):
    nb, seq, _ = x_prompt.shape
    ns, dseq, _ = x_sample.shape
    n_meta = meta_tokens.shape[0]
    n_p, n_s = nb * seq, ns * dseq
    tile_p = min(TOKEN_TILE, seq)
    chunk_p = min(ML_CHUNK, tile_p)

    wg1, wu1, wd1 = ffn1_gate[0].astype(BF16), ffn1_up[0].astype(BF16), ffn1_down[0].astype(BF16)
    wg2, wu2, wd2 = ffn2_gate[0].astype(BF16), ffn2_up[0].astype(BF16), ffn2_down[0].astype(BF16)
    win = w_in[0].astype(BF16)
    wo = w_out[0].astype(BF16)
    glu_w = s5_glu_w[0].astype(BF16)

    ldt_gs = jnp.broadcast_to(s5_log_dt[0][:, None], (S5_GROUPS, S5_STATE))

    def state_major(a):
        return jnp.repeat(a.T, S5_GROUP, axis=1)

    pre, pim, bcat, wout = _s5_prep(
        s5_lambda_re[0].reshape(1, S5_LANES), s5_lambda_im[0].reshape(1, S5_LANES), ldt_gs.reshape(1, S5_LANES),
        s5_b_re[0].reshape(S5_LANES, S5_GROUP).T, s5_b_im[0].reshape(S5_LANES, S5_GROUP).T,
        state_major(s5_lambda_re[0]), state_major(s5_lambda_im[0]), state_major(ldt_gs),
        s5_c_re[0].reshape(D_MODEL, S5_STATE).T, s5_c_im[0].reshape(D_MODEL, S5_STATE).T)
    s5_consts = (pre, pim, bcat, wout, _row(s5_d[0]), glu_w, _row(s5_glu_b[0]), _row(out_norm_s5[0]))

    bdq, bdk, bdv = _ml_prep(ml_wq[0].reshape(D_MODEL, ML_QKV_BLOCK), ml_wk[0].reshape(D_MODEL, ML_QKV_BLOCK),
                             ml_wv[0].reshape(D_MODEL, ML_QKV_BLOCK))
    gate_w = _pad_lanes(jnp.concatenate([ml_igate_w[0], ml_fgate_w[0]], axis=1), GATE_LANES).astype(BF16)
    gate_b = _pad_lanes(jnp.concatenate([ml_igate_b[0], ml_fgate_b[0]])[None, :], GATE_LANES)
    ml_consts = (ml_conv_w[0], _row(ml_conv_b[0]), bdq, bdk, bdv, gate_w, gate_b, _row(ml_norm_w[0]),
                 _row(ml_skip[0]), _row(out_norm_ml[0]))

    ffn1 = (_row(norm_ffn1[0]), wg1, wu1, wd1, _row(norm_mix[0]), win)
    seg_p = ((0, tile_p // SUBLANES),)
    seg_s = tuple((g * SUBLANES * dseq, dseq) for g in range(ns // SUBLANES))
    pad_m = max(SUBLANES * S5_T - n_meta, 0)
    tm_m = n_meta + pad_m
    seg_m = ((n_s, tm_m // SUBLANES),)
    x1_p, u_p, xm_p, z_p = _ffn_in(x_prompt.reshape(n_p, D_MODEL), *ffn1, tm=tile_p, seg_layout=seg_p)
    small = jnp.concatenate([x_sample.reshape(n_s, D_MODEL), jnp.zeros((pad_m, D_MODEL), F32), meta_tokens], axis=0)
    x1_s, u_s, xm_s, z_s = _ffn_in(small, *ffn1, tm=small.shape[0], seg_layout=seg_s + seg_m)

    zs5 = jnp.zeros((1, 1, 1, S5_LANES), F32)
    _, mre, mim = _s5(u_s[n_s:], zs5, zs5, s5_consts, n_outer=1, n_tiles=1, steps=1, tm=tm_m)
    y5_p, pre_s, pim_s = _s5(u_p, mre, mim, s5_consts, n_outer=1, n_tiles=nb, steps=seq // tile_p, tm=tile_p,
                             shared_init=True)
    n_grp = ns // SUBLANES
    y5_s, sre_s, sim_s = _s5(u_s[:n_s], state_s5_re[0].reshape(1, n_grp, SUBLANES, S5_LANES),
                             state_s5_im[0].reshape(1, n_grp, SUBLANES, S5_LANES), s5_consts,
                             n_outer=1, n_tiles=n_grp, steps=1, tm=SUBLANES * dseq, independent=True)

    zc = jnp.zeros((1, ML_HEADS, ML_HEAD_DIM, ML_HEAD_DIM), F32)
    zn = jnp.zeros((1, ML_HEADS, ML_HEAD_DIM), F32)
    zm = jnp.zeros((1, 1, ML_HEADS), F32)
    zcv = jnp.zeros((1, ML_CONV - 1, D_MODEL), F32)
    _, c_m, n_m, m_m, cv_m = _mlstm(xm_s, z_s, zc, zn, zm, zcv, ml_consts, n_seq=1, tiles=1, tm=n_meta,
                                    tc=n_meta, row_block0=(n_s + pad_m) // n_meta)
    ym_p, c_p, nn_p, m_p, cv_p = _mlstm(xm_p, z_p, c_m, n_m, m_m, cv_m, ml_consts, n_seq=nb,
                                        tiles=seq // tile_p, tm=tile_p, tc=chunk_p, shared_init=True,
                                        pipelined=True)
    ym_s, c_s, nn_s, m_s, cv_s = _mlstm(xm_s, z_s, state_mlstm_c[0], state_mlstm_n[0],
                                        state_mlstm_m[0][:, None, :], state_mlstm_conv[0], ml_consts,
                                        n_seq=ns, tiles=1, tm=dseq, tc=dseq)

    ffn2 = (wo, _row(norm_ffn2[0]), wg2, wu2, wd2, _row(norm_final))
    y_p = _out_ffn(x1_p, y5_p, ym_p, *ffn2, rows=n_p, tm=tile_p, seg_layout=seg_p)
    y_s = _out_ffn(x1_s, y5_s, ym_s, *ffn2, rows=n_s, tm=n_s, seg_layout=seg_s)

    def s5_state(s, n):
        return s.reshape(1, n, S5_GROUPS, S5_STATE)

    return (y_p.reshape(nb, seq, D_MODEL), y_s.reshape(ns, dseq, D_MODEL),
            s5_state(pre_s, nb), s5_state(pim_s, nb), c_p[None], nn_p[None], m_p[:, 0][None], cv_p[None],
            s5_state(sre_s, ns), s5_state(sim_s, ns), c_s[None], nn_s[None], m_s[:, 0][None], cv_s[None])
```

```python
import functools

import jax
import jax.numpy as jnp
from jax import lax
from jax.experimental import pallas as pl
from jax.experimental.pallas import tpu as pltpu

F32 = jnp.float32
BF16 = jnp.bfloat16

D_MODEL = 1024
D_FF = 2816
S5_GROUPS = 64
S5_GROUP = 16
S5_STATE = 64
S5_LANES = S5_GROUPS * S5_STATE
MXU_TILE = 256
S5_BLOCKS = 8
S5_BLOCK_GROUPS = S5_GROUPS // S5_BLOCKS
S5_BLOCK_IN = D_MODEL // S5_BLOCKS
S5_BLOCK_ST = S5_LANES // S5_BLOCKS
ML_HEADS = 4
ML_HEAD_DIM = 256
ML_CONV = 4
ML_QKV_BLOCK = 4
EPS = 1e-6
SUBLANES = 8
GATE_LANES = 128
POW_ROWS = 64
S5_T = 2
VMEM_LIMIT = 56 * 1024 * 1024

TOKEN_TILE = 512
ML_CHUNK = 256


def _rms(x, g):
    return x * lax.rsqrt(jnp.mean(x * x, axis=-1, keepdims=True) + EPS) * g


def _silu(x):
    h = 0.5 * x
    return h + h * jnp.tanh(h)


def _swiglu(h, wg_ref, wu_ref, wd_ref, acc_ref, ff_chunk):
    for c in range(D_FF // ff_chunk):
        sl = slice(c * ff_chunk, (c + 1) * ff_chunk)
        g = jnp.dot(h, wg_ref[:, sl], preferred_element_type=F32)
        u = jnp.dot(h, wu_ref[:, sl], preferred_element_type=F32)
        a = (_silu(g) * u).astype(BF16)
        d = jnp.dot(a, wd_ref[sl, :], preferred_element_type=F32)
        if c == 0:
            acc_ref[...] = d
        else:
            acc_ref[...] += d
    return acc_ref[...]


def _const_spec(shape):
    nd = len(shape)
    return pl.BlockSpec(shape, lambda *_: (0,) * nd, pipeline_mode=pl.Buffered(1))


def _segment_rows(x, layout, inverse=False):
    parts = []
    for row0, r in layout:
        n = SUBLANES * r
        g = x[row0:row0 + n]
        if r % SUBLANES == 0:
            shape = (r, SUBLANES) if inverse else (SUBLANES, r)
            g = jnp.swapaxes(g.reshape(shape + g.shape[1:]), 0, 1).reshape(g.shape)
        elif inverse:
            g = jnp.concatenate([g[i * SUBLANES + a:i * SUBLANES + a + 1]
                                 for a in range(SUBLANES) for i in range(r)], axis=0)
        else:
            g = jnp.concatenate([g[a * r + i:a * r + i + 1]
                                 for i in range(r) for a in range(SUBLANES)], axis=0)
        parts.append(g)
    return parts[0] if len(parts) == 1 else jnp.concatenate(parts, axis=0)


def _ffn_in_kernel(x_ref, g1_ref, wg_ref, wu_ref, wd_ref, g2_ref, win_ref,
                   x1_ref, u_ref, xm_ref, z_ref, acc_ref, *, ff_chunk, seg_layout):
    x = x_ref[...]
    h = _rms(x, g1_ref[...]).astype(BF16)
    x1 = x + 0.5 * _swiglu(h, wg_ref, wu_ref, wd_ref, acc_ref, ff_chunk)
    x1_ref[...] = x1
    h2f = _rms(x1, g2_ref[...])
    h2 = h2f.astype(BF16)
    h2s = _segment_rows(h2f, seg_layout).astype(BF16)
    u_ref[...] = jnp.dot(h2s, win_ref[:, 0:D_MODEL], preferred_element_type=F32)
    xm_ref[...] = jnp.dot(h2, win_ref[:, D_MODEL:2 * D_MODEL], preferred_element_type=F32)
    z_ref[...] = jnp.dot(h2, win_ref[:, 2 * D_MODEL:3 * D_MODEL], preferred_element_type=F32)


def _ffn_in(x, g1, wg, wu, wd, g2, win, *, tm, seg_layout, ff_chunk=256):
    rows = x.shape[0]
    row_spec = pl.BlockSpec((tm, D_MODEL), lambda i: (i, 0))
    out = jax.ShapeDtypeStruct((rows, D_MODEL), F32)
    return pl.pallas_call(
        functools.partial(_ffn_in_kernel, ff_chunk=ff_chunk, seg_layout=seg_layout),
        grid=(rows // tm,),
        in_specs=[row_spec, _const_spec(g1.shape), _const_spec(wg.shape), _const_spec(wu.shape),
                  _const_spec(wd.shape), _const_spec(g2.shape), _const_spec(win.shape)],
        out_specs=[row_spec] * 4,
        out_shape=[out] * 4,
        scratch_shapes=[pltpu.VMEM((tm, D_MODEL), F32)],
        compiler_params=pltpu.CompilerParams(dimension_semantics=("parallel",), vmem_limit_bytes=VMEM_LIMIT),
        name="ffn_in",
    )(x, g1, wg, wu, wd, g2, win)


def _out_ffn_kernel(x1_ref, y5_ref, ym_ref, wo_ref, g_ref, wg_ref, wu_ref, wd_ref, gf_ref,
                    o_ref, acc_ref, *, ff_chunk, seg_layout):
    p5 = jnp.dot(y5_ref[...], wo_ref[0:D_MODEL, :], preferred_element_type=F32)
    x2 = (x1_ref[...] + _segment_rows(p5, seg_layout, inverse=True)
          + jnp.dot(ym_ref[...], wo_ref[D_MODEL:2 * D_MODEL, :], preferred_element_type=F32))
    h = _rms(x2, g_ref[...]).astype(BF16)
    x3 = x2 + 0.5 * _swiglu(h, wg_ref, wu_ref, wd_ref, acc_ref, ff_chunk)
    o_ref[...] = _rms(x3, gf_ref[...])


def _out_ffn(x1, y5, ym, wo, g, wg, wu, wd, gf, *, rows, tm, seg_layout, ff_chunk=256):
    row_spec = pl.BlockSpec((tm, D_MODEL), lambda i: (i, 0))
    return pl.pallas_call(
        functools.partial(_out_ffn_kernel, ff_chunk=ff_chunk, seg_layout=seg_layout),
        grid=(rows // tm,),
        in_specs=[row_spec, row_spec, row_spec, _const_spec(wo.shape), _const_spec(g.shape),
                  _const_spec(wg.shape), _const_spec(wu.shape), _const_spec(wd.shape), _const_spec(gf.shape)],
        out_specs=row_spec,
        out_shape=jax.ShapeDtypeStruct((rows, D_MODEL), F32),
        scratch_shapes=[pltpu.VMEM((tm, D_MODEL), F32)],
        compiler_params=pltpu.CompilerParams(dimension_semantics=("parallel",), vmem_limit_bytes=VMEM_LIMIT),
        name="out_ffn",
    )(x1, y5, ym, wo, g, wg, wu, wd, gf)


def _lam_bar(lr, li, ldt):
    dt = jnp.exp(ldt)
    mag = jnp.exp(lr * dt)
    th = li * dt
    return mag * jnp.cos(th), mag * jnp.sin(th), dt


def _cmul(ar, ai, br, bi):
    return ar * br - ai * bi, ar * bi + ai * br


def _dot_split(a, b):
    a_hi = a.astype(BF16)
    b_hi = b.astype(BF16)
    a_lo = (a - a_hi.astype(F32)).astype(BF16)
    b_lo = (b - b_hi.astype(F32)).astype(BF16)
    return (jnp.dot(a_hi, b_hi, preferred_element_type=F32) + jnp.dot(a_hi, b_lo, preferred_element_type=F32)
            + jnp.dot(a_lo, b_hi, preferred_element_type=F32))


def _s5_prep_kernel(lre_ref, lim_ref, ldt_ref, bre_ref, bim_ref, lre_t_ref, lim_t_ref, ldt_t_ref, cre_ref, cim_ref,
                    pre_ref, pim_ref, bcat_ref, wout_ref):
    lr = lre_ref[...]
    li = lim_ref[...]
    ar, ai, _ = _lam_bar(lr, li, ldt_ref[...])
    nr = ar - 1.0
    den = lr * lr + li * li
    cr = (nr * lr + ai * li) / den
    ci = (ai * lr - nr * li) / den
    bs = [_cmul(cr, ci, bre_ref[...], bim_ref[...])]
    for _ in range(1, S5_T):
        bs.append(_cmul(ar, ai, *bs[-1]))

    pr, pi = ar, ai
    pre_ref[0:1, :] = pr
    pim_ref[0:1, :] = pi
    for j in range(1, POW_ROWS):
        pr, pi = _cmul(pr, pi, ar, ai)
        pre_ref[j:j + 1, :] = pr
        pim_ref[j:j + 1, :] = pi

    tr, ti, _ = _lam_bar(lre_t_ref[...], lim_t_ref[...], ldt_t_ref[...])
    cs = [(cre_ref[...], cim_ref[...])]
    for _ in range(S5_T):
        cs.append(_cmul(tr, ti, *cs[-1]))

    g_shift, s_shift = S5_GROUP.bit_length() - 1, S5_STATE.bit_length() - 1
    b_shape = (S5_BLOCK_IN, S5_BLOCK_ST)
    b_mask = (lax.broadcasted_iota(jnp.int32, b_shape, 0) >> g_shift) == (
        lax.broadcasted_iota(jnp.int32, b_shape, 1) >> s_shift)
    c_shape = (S5_BLOCK_ST, S5_BLOCK_IN)
    c_mask = (lax.broadcasted_iota(jnp.int32, c_shape, 0) >> s_shift) == (
        lax.broadcasted_iota(jnp.int32, c_shape, 1) >> g_shift)

    def b_block(br, bi, blk):
        parts = []
        for x in (br, bi):
            x = jnp.concatenate([x[:, blk * S5_BLOCK_ST:(blk + 1) * S5_BLOCK_ST]] * S5_BLOCK_GROUPS, axis=0)
            parts.append(jnp.where(b_mask, x, 0.0))
        return jnp.concatenate(parts, axis=1)

    def c_block(xr, xi, blk):
        parts = []
        for x in (xr, -xi):
            x = jnp.concatenate([x[:, blk * S5_BLOCK_IN:(blk + 1) * S5_BLOCK_IN]] * S5_BLOCK_GROUPS, axis=0)
            parts.append(jnp.where(c_mask, x, 0.0))
        return jnp.concatenate(parts, axis=0)

    zero = jnp.zeros((S5_BLOCK_IN, S5_BLOCK_IN), F32)
    z_rows = 2 * S5_BLOCK_ST
    for blk in range(S5_BLOCKS):
        bms = [b_block(br, bi, blk) for br, bi in bs]
        cms = [c_block(xr, xi, blk) for xr, xi in cs]
        kcat = _dot_split(bms[0], jnp.concatenate(cms[:S5_T], axis=1))
        ks = [kcat[:, j * S5_BLOCK_IN:(j + 1) * S5_BLOCK_IN] for j in range(S5_T)]
        for f in range(S5_T):
            bcat_ref[blk, f * S5_BLOCK_IN:(f + 1) * S5_BLOCK_IN, :] = bms[S5_T - 1 - f].astype(BF16)
            row0 = z_rows + f * S5_BLOCK_IN
            wout_ref[blk, row0:row0 + S5_BLOCK_IN, :] = jnp.concatenate(
                [ks[g - f] if g >= f else zero for g in range(S5_T)], axis=1).astype(BF16)
        wout_ref[blk, 0:z_rows, :] = jnp.concatenate(cms[1:], axis=1).astype(BF16)


def _s5_prep(lre, lim, ldt, bre_t, bim_t, lre_t, lim_t, ldt_t, cre_t, cim_t):
    pow_shape = jax.ShapeDtypeStruct((POW_ROWS, S5_LANES), F32)
    return pl.pallas_call(
        _s5_prep_kernel,
        out_shape=[pow_shape, pow_shape,
                   jax.ShapeDtypeStruct((S5_BLOCKS, S5_T * S5_BLOCK_IN, 2 * S5_BLOCK_ST), BF16),
                   jax.ShapeDtypeStruct((S5_BLOCKS, 2 * S5_BLOCK_ST + S5_T * S5_BLOCK_IN, S5_T * S5_BLOCK_IN), BF16)],
        compiler_params=pltpu.CompilerParams(vmem_limit_bytes=VMEM_LIMIT),
        name="s5_prep",
    )(lre, lim, ldt, bre_t, bim_t, lre_t, lim_t, ldt_t, cre_t, cim_t)


def _ml_prep_kernel(wq_ref, wk_ref, wv_ref, oq_ref, ok_ref, ov_ref):
    shape = (MXU_TILE, MXU_TILE)
    shift = ML_QKV_BLOCK.bit_length() - 1
    row = lax.broadcasted_iota(jnp.int32, shape, 0)
    col = lax.broadcasted_iota(jnp.int32, shape, 1)
    same = (row >> shift) == (col >> shift)
    sel = col & (ML_QKV_BLOCK - 1)
    for w_ref, o_ref in ((wq_ref, oq_ref), (wk_ref, ok_ref), (wv_ref, ov_ref)):
        for c in range(D_MODEL // MXU_TILE):
            w = w_ref[c * MXU_TILE:(c + 1) * MXU_TILE, :]
            acc = jnp.zeros(shape, F32)
            for o in range(ML_QKV_BLOCK):
                acc = jnp.where(sel == o, w[:, o:o + 1], acc)
            o_ref[c] = jnp.where(same, acc, 0.0).astype(BF16)


def _ml_prep(wq, wk, wv):
    out = jax.ShapeDtypeStruct((D_MODEL // MXU_TILE, MXU_TILE, MXU_TILE), BF16)
    return pl.pallas_call(_ml_prep_kernel, out_shape=[out] * 3, name="ml_prep")(wq, wk, wv)


def _block_diag_dot(x, w_ref):
    return jnp.concatenate(
        [jnp.dot(x[:, c * MXU_TILE:(c + 1) * MXU_TILE], w_ref[c], preferred_element_type=F32)
         for c in range(w_ref.shape[0])], axis=1)


def _s5_kernel(u_ref, sre0_ref, sim0_ref, pre_ref, pim_ref, bcat_ref, wout_ref, d_ref, gw_ref, gb_ref, on_ref,
               y_ref, sre_ref, sim_ref, s_ref, *, tm, independent, n_tiles):
    r = tm // SUBLANES
    nc = r // S5_T
    rows_c = nc * SUBLANES
    t = pl.program_id(1)

    @pl.when(t == 0)
    def _():
        sre_ref[...] = jnp.broadcast_to(sre0_ref[...], sre_ref.shape)
        sim_ref[...] = jnp.broadcast_to(sim0_ref[...], sim_ref.shape)

    ups = [u_ref[g] for g in range(n_tiles)]
    us = []
    for f in range(S5_T):
        us.append(jnp.concatenate(
            [up.reshape(nc, S5_T * SUBLANES, D_MODEL)[:, f * SUBLANES:(f + 1) * SUBLANES, :].reshape(rows_c, D_MODEL)
             for up in ups], axis=0).astype(BF16))

    ys = [[] for _ in range(S5_T)]
    for b in range(S5_BLOCKS):
        lanes = slice(b * S5_BLOCK_ST, (b + 1) * S5_BLOCK_ST)
        chans = slice(b * S5_BLOCK_IN, (b + 1) * S5_BLOCK_IN)
        re = slice(0, S5_BLOCK_ST)
        im = slice(S5_BLOCK_ST, 2 * S5_BLOCK_ST)
        ucat = jnp.concatenate([u[:, chans] for u in us], axis=1)
        s_ref[...] = jnp.dot(ucat, bcat_ref[b], preferred_element_type=F32)
        ltr = jnp.broadcast_to(pre_ref[S5_T - 1:S5_T, lanes], (SUBLANES, S5_BLOCK_ST))
        lti = jnp.broadcast_to(pim_ref[S5_T - 1:S5_T, lanes], (SUBLANES, S5_BLOCK_ST))

        def scan(cr, ci, row0):
            for c in range(nc):
                rows = slice(row0 + c * SUBLANES, row0 + (c + 1) * SUBLANES)
                nr = ltr * cr - lti * ci + s_ref[rows, re]
                ni = ltr * ci + lti * cr + s_ref[rows, im]
                s_ref[rows, re] = cr
                s_ref[rows, im] = ci
                cr, ci = nr, ni
            return cr, ci

        for g in range(n_tiles):
            row0 = g * rows_c
            if independent:
                fr, fi = scan(sre_ref[g, :, lanes], sim_ref[g, :, lanes], row0)
                sre_ref[g, :, lanes] = fr
                sim_ref[g, :, lanes] = fi
            else:
                zero = jnp.zeros((SUBLANES, S5_BLOCK_ST), F32)
                fr, fi = scan(zero, zero, row0)
                rr = pre_ref[r - 1:r, lanes]
                ri = pim_ref[r - 1:r, lanes]
                cr = sre_ref[g, :, lanes]
                ci = sim_ref[g, :, lanes]
                rows_r, rows_i = [], []
                for a in range(SUBLANES):
                    rows_r.append(cr)
                    rows_i.append(ci)
                    cr, ci = rr * cr - ri * ci + fr[a:a + 1], rr * ci + ri * cr + fi[a:a + 1]
                sre_ref[g, :, lanes] = cr
                sim_ref[g, :, lanes] = ci
                cin_r = jnp.concatenate(rows_r, axis=0)
                cin_i = jnp.concatenate(rows_i, axis=0)
                s_ref[row0:row0 + SUBLANES, re] = cin_r
                s_ref[row0:row0 + SUBLANES, im] = cin_i
                for c in range(1, nc):
                    rows = slice(row0 + c * SUBLANES, row0 + (c + 1) * SUBLANES)
                    pr = pre_ref[S5_T * c - 1:S5_T * c, lanes]
                    pi = pim_ref[S5_T * c - 1:S5_T * c, lanes]
                    s_ref[rows, re] += pr * cin_r - pi * cin_i
                    s_ref[rows, im] += pr * cin_i + pi * cin_r
        yb = jnp.dot(jnp.concatenate([s_ref[...].astype(BF16), ucat], axis=1), wout_ref[b],
                     preferred_element_type=F32)
        for f in range(S5_T):
            ys[f].append(yb[:, f * S5_BLOCK_IN:(f + 1) * S5_BLOCK_IN])

    yf = [jnp.concatenate(ys[f], axis=1) for f in range(S5_T)]
    for g in range(n_tiles):
        rows = slice(g * rows_c, (g + 1) * rows_c)
        y = jnp.concatenate([y[rows].reshape(nc, SUBLANES, D_MODEL) for y in yf], axis=1).reshape(tm, D_MODEL)
        y = y + d_ref[...] * ups[g]
        gl = jax.nn.gelu(y)
        o = gl * jax.nn.sigmoid(jnp.dot(gl.astype(BF16), gw_ref[...], preferred_element_type=F32) + gb_ref[...])
        y_ref[g] = _rms(o, on_ref[...]).astype(BF16)


def _s5(u, sre0, sim0, consts, *, n_outer, n_tiles, steps, tm, independent=False, shared_init=False):
    srows = SUBLANES if independent else 1
    u5 = u.reshape(n_outer, n_tiles, steps, tm, D_MODEL)
    seq_spec = pl.BlockSpec((None, n_tiles, None, tm, D_MODEL), lambda o, t: (o, 0, t, 0, 0))
    if shared_init:
        st_in = pl.BlockSpec((None, 1, srows, S5_LANES), lambda o, t: (0, 0, 0, 0))
    else:
        st_in = pl.BlockSpec((None, n_tiles, srows, S5_LANES), lambda o, t: (o, 0, 0, 0))
    st_out = pl.BlockSpec((None, n_tiles, srows, S5_LANES), lambda o, t: (o, 0, 0, 0))
    st_shape = jax.ShapeDtypeStruct((n_outer, n_tiles, srows, S5_LANES), F32)
    y, sre, sim = pl.pallas_call(
        functools.partial(_s5_kernel, tm=tm, independent=independent, n_tiles=n_tiles),
        grid=(n_outer, steps),
        in_specs=[seq_spec, st_in, st_in] + [_const_spec(a.shape) for a in consts],
        out_specs=[seq_spec, st_out, st_out],
        out_shape=[jax.ShapeDtypeStruct(u5.shape, BF16), st_shape, st_shape],
        scratch_shapes=[pltpu.VMEM((n_tiles * tm // S5_T, 2 * S5_BLOCK_ST), F32)],
        compiler_params=pltpu.CompilerParams(dimension_semantics=("parallel", "arbitrary"),
                                             vmem_limit_bytes=VMEM_LIMIT),
        name="s5_mix",
    )(u5, sre0, sim0, *consts)
    return y.reshape(u.shape), sre, sim


def _ml_front(xm_ref, cv_ref, xbuf_ref, consts, bufs, tm):
    cw_ref, cb_ref, bdq_ref, bdk_ref, bdv_ref, gw_ref, gb_ref = consts
    xc_ref, q_ref, k_ref, v_ref, vf_ref, g_ref = bufs
    pre = ML_CONV - 1
    xm = xm_ref[...]
    xbuf_ref[SUBLANES - pre:SUBLANES, :] = cv_ref[...]
    xbuf_ref[SUBLANES:SUBLANES + tm, :] = xm
    cw = cw_ref[...]
    xc = cb_ref[...] + cw[pre:pre + 1] * xm
    for j in range(pre):
        xc = xc + cw[j:j + 1] * xbuf_ref[SUBLANES - pre + j:SUBLANES - pre + j + tm, :]
    cv_ref[...] = xbuf_ref[SUBLANES + tm - pre:SUBLANES + tm, :]
    xc = _silu(xc)
    xc_ref[...] = xc

    xcb = xc.astype(BF16)
    q = _block_diag_dot(xcb, bdq_ref)
    k = _block_diag_dot(xcb, bdk_ref)
    v = _block_diag_dot(xm.astype(BF16), bdv_ref)
    qb = q.astype(BF16)
    vb = v.astype(BF16)
    g_ref[...] = jnp.dot(jnp.concatenate([qb, k.astype(BF16), vb], axis=1), gw_ref[...],
                         preferred_element_type=F32) + gb_ref[...]
    q_ref[...] = qb
    k_ref[...] = (k * (ML_HEAD_DIM ** -0.5)).astype(BF16)
    v_ref[...] = vb
    vf_ref[...] = v


def _ml_back(bufs, z_ref, state, consts, y_ref, h_ref, tm, tc):
    xc_ref, q_ref, k_ref, v_ref, vf_ref, g_ref = bufs
    c_ref, n_ref, m_ref = state
    nw_ref, sk_ref, on_ref = consts
    lane = lax.broadcasted_iota(jnp.int32, (tc, GATE_LANES), 1)
    row = lax.broadcasted_iota(jnp.int32, (tc, tc), 0)
    col = lax.broadcasted_iota(jnp.int32, (tc, tc), 1)
    causal = row >= col
    tril = causal.astype(F32)

    def chunk(rows):
        gates = g_ref[rows, :]
        lf = jnp.minimum(gates, 0.0) - jnp.log1p(jnp.exp(-jnp.abs(gates)))
        lf = jnp.where((lane >= ML_HEADS) & (lane < 2 * ML_HEADS), lf, 0.0)
        cum = jnp.dot(tril, lf, preferred_element_type=F32, precision=lax.Precision.HIGHEST)
        arr = jnp.where(lane < ML_HEADS, gates, cum)
        arr_t = arr.T
        for h in range(ML_HEADS):
            hs = slice(h * ML_HEAD_DIM, (h + 1) * ML_HEAD_DIM)
            qh = q_ref[rows, hs]
            kh = k_ref[rows, hs]
            vh = v_ref[rows, hs]
            ig_col = arr[:, h:h + 1]
            b_col = arr[:, ML_HEADS + h:ML_HEADS + h + 1]
            ig_row = arr_t[h:h + 1, :]
            b_row = arr_t[ML_HEADS + h:ML_HEADS + h + 1, :]
            m_prev = m_ref[:, h:h + 1]
            c_prev = c_ref[h]
            n_prev = n_ref[h:h + 1, :]

            logw = jnp.where(causal, b_col - b_row + ig_row, -jnp.inf)
            log_inter = b_col + m_prev
            m_t = jnp.maximum(log_inter, jnp.max(logw, axis=-1, keepdims=True))
            w = jnp.exp(logw - m_t)
            a_inter = jnp.exp(log_inter - m_t)
            s = lax.dot_general(qh, kh, (((1,), (1,)), ((), ())), preferred_element_type=F32) * w
            inter = lax.dot_general(qh, c_prev.astype(BF16), (((1,), (1,)), ((), ())),
                                    preferred_element_type=F32)
            num = a_inter * inter + jnp.dot(s.astype(BF16), vh, preferred_element_type=F32)
            den = (a_inter * jnp.sum(qh.astype(F32) * n_prev, axis=-1, keepdims=True)
                   + jnp.sum(s, axis=-1, keepdims=True))
            hh = num / jnp.maximum(jnp.abs(den), jnp.exp(-m_t))
            mu = jnp.mean(hh, axis=-1, keepdims=True)
            hc = hh - mu
            var = jnp.mean(hc * hc, axis=-1, keepdims=True)
            h_ref[rows, hs] = hc * lax.rsqrt(var + EPS)

            b_last = b_col[tc - 1:tc, :]
            m_new = m_t[tc - 1:tc, :]
            g_state = jnp.exp(b_last + m_prev - m_new)
            g_src = jnp.exp(b_last - b_col + ig_col - m_new)
            vs = (vf_ref[rows, hs] * g_src).astype(BF16)
            c_ref[h] = g_state * c_prev + lax.dot_general(vs, kh, (((0,), (0,)), ((), ())),
                                                          preferred_element_type=F32)
            n_ref[h:h + 1, :] = g_state * n_prev + jnp.sum(g_src * kh.astype(F32), axis=0, keepdims=True)
            m_ref[:, h:h + 1] = m_new

    for j in range(tm // tc):
        chunk(slice(j * tc, (j + 1) * tc))

    out = (h_ref[...] * nw_ref[...] + sk_ref[...] * xc_ref[...]) * _silu(z_ref[...])
    y_ref[...] = _rms(out, on_ref[...]).astype(BF16)


def _mlstm_kernel(xm_ref, z_ref, c0_ref, n0_ref, m0_ref, cv0_ref, cw_ref, cb_ref, bdq_ref, bdk_ref, bdv_ref,
                  gw_ref, gb_ref, nw_ref, sk_ref, on_ref,
                  y_ref, c_ref, n_ref, m_ref, cv_ref, xbuf_ref, h_ref, *bufs, tm, tc, pipelined):
    t = pl.program_id(1)
    front_consts = (cw_ref, cb_ref, bdq_ref, bdk_ref, bdv_ref, gw_ref, gb_ref)
    back_consts = (nw_ref, sk_ref, on_ref)
    state = (c_ref, n_ref, m_ref)
    n_buf = len(bufs) // 2 if pipelined else len(bufs)

    @pl.when(t == 0)
    def _():
        cv_ref[...] = cv0_ref[...]
        if pipelined:
            for ref in bufs[n_buf:]:
                ref[...] = jnp.zeros(ref.shape, ref.dtype)

    @pl.when(t <= (1 if pipelined else 0))
    def _():
        c_ref[...] = c0_ref[...]
        n_ref[...] = n0_ref[...]
        m_ref[...] = m0_ref[...]

    def step(front_set, back_set):
        _ml_front(xm_ref, cv_ref, xbuf_ref, front_consts, front_set, tm)
        _ml_back(back_set, z_ref, state, back_consts, y_ref, h_ref, tm, tc)

    if pipelined:
        pl.when(t % 2 == 0)(lambda: step(bufs[:n_buf], bufs[n_buf:]))
        pl.when(t % 2 == 1)(lambda: step(bufs[n_buf:], bufs[:n_buf]))
    else:
        step(bufs, bufs)


def _mlstm(xm, z, c0, n0, m0, cv0, consts, *, n_seq, tiles, tm, tc, row_block0=0, shared_init=False,
           pipelined=False):
    if pipelined:
        steps = tiles + 1
        front_spec = pl.BlockSpec((tm, D_MODEL),
                                  lambda i, t: (row_block0 + i * tiles + jnp.minimum(t, tiles - 1), 0))
        back_spec = pl.BlockSpec((tm, D_MODEL),
                                 lambda i, t: (row_block0 + i * tiles + jnp.maximum(t - 1, 0), 0))
        out_seq_spec = pl.BlockSpec((tm, D_MODEL), lambda i, t: (i * tiles + jnp.maximum(t - 1, 0), 0))
    else:
        steps = tiles
        front_spec = back_spec = pl.BlockSpec((tm, D_MODEL), lambda i, t: (row_block0 + i * tiles + t, 0))
        out_seq_spec = pl.BlockSpec((tm, D_MODEL), lambda i, t: (i * tiles + t, 0))

    def st(shape, shared):
        nd = len(shape)
        return pl.BlockSpec((None,) + shape,
                            (lambda i, t: (0,) * (nd + 1)) if shared else (lambda i, t: (i,) + (0,) * nd))

    st_shapes = ((ML_HEADS, ML_HEAD_DIM, ML_HEAD_DIM), (ML_HEADS, ML_HEAD_DIM), (1, ML_HEADS),
                 (ML_CONV - 1, D_MODEL))
    buf_set = [pltpu.VMEM((tm, D_MODEL), F32), pltpu.VMEM((tm, D_MODEL), BF16), pltpu.VMEM((tm, D_MODEL), BF16),
               pltpu.VMEM((tm, D_MODEL), BF16), pltpu.VMEM((tm, D_MODEL), F32), pltpu.VMEM((tm, GATE_LANES), F32)]
    return pl.pallas_call(
        functools.partial(_mlstm_kernel, tm=tm, tc=tc, pipelined=pipelined),
        grid=(n_seq, steps),
        in_specs=[front_spec, back_spec] + [st(s, shared_init) for s in st_shapes]
                 + [_const_spec(a.shape) for a in consts],
        out_specs=[out_seq_spec] + [st(s, False) for s in st_shapes],
        out_shape=[jax.ShapeDtypeStruct((n_seq * tiles * tm, D_MODEL), BF16)]
                  + [jax.ShapeDtypeStruct((n_seq,) + s, F32) for s in st_shapes],
        scratch_shapes=[pltpu.VMEM((tm + 2 * SUBLANES, D_MODEL), F32), pltpu.VMEM((tm, D_MODEL), F32)]
                       + buf_set * (2 if pipelined else 1),
        compiler_params=pltpu.CompilerParams(dimension_semantics=("parallel", "arbitrary"),
                                             vmem_limit_bytes=VMEM_LIMIT),
        name="mlstm_mix",
    )(xm, z, c0, n0, m0, cv0, *consts)


def _row(v):
    return v.reshape(1, -1).astype(F32)


def _pad_lanes(v, width):
    return jnp.pad(v, [(0, 0)] * (v.ndim - 1) + [(0, width - v.shape[-1])])


def kernel(x_prompt, x_sample, state_s5_re, state_s5_im, state_mlstm_c, state_mlstm_n, state_mlstm_m,
           state_mlstm_conv, meta_tokens, norm_ffn1, ffn1_gate, ffn1_up, ffn1_down, norm_mix, w_in,
           s5_lambda_re, s5_lambda_im, s5_log_dt, s5_b_re, s5_b_im, s5_c_re, s5_c_im, s5_d, s5_glu_w, s5_glu_b,
           ml_conv_w, ml_conv_b, ml_wq, ml_wk, ml_wv, ml_igate_w, ml_igate_b, ml_fgate_w, ml_fgate_b,
           ml_norm_w, ml_skip, out_norm_s5, out_norm_ml, w_out, norm_ffn2, ffn2_gate, ffn2_up, ffn2_down,
           norm_final):
    nb, seq, _ = x_prompt.shape
    ns, dseq, _ = x_sample.shape
    n_meta = meta_tokens.shape[0]
    n_p, n_s = nb * seq, ns * dseq
    tile_p = min(TOKEN_TILE, seq)
    chunk_p = min(ML_CHUNK, tile_p)

    wg1, wu1, wd1 = ffn1_gate[0].astype(BF16), ffn1_up[0].astype(BF16), ffn1_down[0].astype(BF16)
    wg2, wu2, wd2 = ffn2_gate[0].astype(BF16), ffn2_up[0].astype(BF16), ffn2_down[0].astype(BF16)
    win = w_in[0].astype(BF16)
    wo = w_out[0].astype(BF16)
    glu_w = s5_glu_w[0].astype(BF16)

    ldt_gs = jnp.broadcast_to(s5_log_dt[0][:, None], (S5_GROUPS, S5_STATE))

    def state_major(a):
        return jnp.repeat(a.T, S5_GROUP, axis=1)

    pre, pim, bcat, wout = _s5_prep(
        s5_lambda_re[0].reshape(1, S5_LANES), s5_lambda_im[0].reshape(1, S5_LANES), ldt_gs.reshape(1, S5_LANES),
        s5_b_re[0].reshape(S5_LANES, S5_GROUP).T, s5_b_im[0].reshape(S5_LANES, S5_GROUP).T,
        state_major(s5_lambda_re[0]), state_major(s5_lambda_im[0]), state_major(ldt_gs),
        s5_c_re[0].reshape(D_MODEL, S5_STATE).T, s5_c_im[0].reshape(D_MODEL, S5_STATE).T)
    s5_consts = (pre, pim, bcat, wout, _row(s5_d[0]), glu_w, _row(s5_glu_b[0]), _row(out_norm_s5[0]))

    bdq, bdk, bdv = _ml_prep(ml_wq[0].reshape(D_MODEL, ML_QKV_BLOCK), ml_wk[0].reshape(D_MODEL, ML_QKV_BLOCK),
                             ml_wv[0].reshape(D_MODEL, ML_QKV_BLOCK))
    gate_w = _pad_lanes(jnp.concatenate([ml_igate_w[0], ml_fgate_w[0]], axis=1), GATE_LANES).astype(BF16)
    gate_b = _pad_lanes(jnp.concatenate([ml_igate_b[0], ml_fgate_b[0]])[None, :], GATE_LANES)
    ml_consts = (ml_conv_w[0], _row(ml_conv_b[0]), bdq, bdk, bdv, gate_w, gate_b, _row(ml_norm_w[0]),
                 _row(ml_skip[0]), _row(out_norm_ml[0]))

    ffn1 = (_row(norm_ffn1[0]), wg1, wu1, wd1, _row(norm_mix[0]), win)
    seg_p = ((0, tile_p // SUBLANES),)
    seg_s = tuple((g * SUBLANES * dseq, dseq) for g in range(ns // SUBLANES))
    pad_m = max(SUBLANES * S5_T - n_meta, 0)
    tm_m = n_meta + pad_m
    seg_m = ((n_s, tm_m // SUBLANES),)
    x1_p, u_p, xm_p, z_p = _ffn_in(x_prompt.reshape(n_p, D_MODEL), *ffn1, tm=tile_p, seg_layout=seg_p)
    small = jnp.concatenate([x_sample.reshape(n_s, D_MODEL), jnp.zeros((pad_m, D_MODEL), F32), meta_tokens], axis=0)
    x1_s, u_s, xm_s, z_s = _ffn_in(small, *ffn1, tm=small.shape[0], seg_layout=seg_s + seg_m)

    zs5 = jnp.zeros((1, 1, 1, S5_LANES), F32)
    _, mre, mim = _s5(u_s[n_s:], zs5, zs5, s5_consts, n_outer=1, n_tiles=1, steps=1, tm=tm_m)
    y5_p, pre_s, pim_s = _s5(u_p, mre, mim, s5_consts, n_outer=1, n_tiles=nb, steps=seq // tile_p, tm=tile_p,
                             shared_init=True)
    n_grp = ns // SUBLANES
    y5_s, sre_s, sim_s = _s5(u_s[:n_s], state_s5_re[0].reshape(1, n_grp, SUBLANES, S5_LANES),
                             state_s5_im[0].reshape(1, n_grp, SUBLANES, S5_LANES), s5_consts,
                             n_outer=1, n_tiles=n_grp, steps=1, tm=SUBLANES * dseq, independent=True)

    zc = jnp.zeros((1, ML_HEADS, ML_HEAD_DIM, ML_HEAD_DIM), F32)
    zn = jnp.zeros((1, ML_HEADS, ML_HEAD_DIM), F32)
    zm = jnp.zeros((1, 1, ML_HEADS), F32)
    zcv = jnp.zeros((1, ML_CONV - 1, D_MODEL), F32)
    _, c_m, n_m, m_m, cv_m = _mlstm(xm_s, z_s, zc, zn, zm, zcv, ml_consts, n_seq=1, tiles=1, tm=n_meta,
                                    tc=n_meta, row_block0=(n_s + pad_m) // n_meta)
    ym_p, c_p, nn_p, m_p, cv_p = _mlstm(xm_p, z_p, c_m, n_m, m_m, cv_m, ml_consts, n_seq=nb,
                                        tiles=seq // tile_p, tm=tile_p, tc=chunk_p, shared_init=True,
                                        pipelined=True)
    ym_s, c_s, nn_s, m_s, cv_s = _mlstm(xm_s, z_s, state_mlstm_c[0], state_mlstm_n[0],
                                        state_mlstm_m[0][:, None, :], state_mlstm_conv[0], ml_consts,
                                        n_seq=ns, tiles=1, tm=dseq, tc=dseq)

    ffn2 = (wo, _row(norm_ffn2[0]), wg2, wu2, wd2, _row(norm_final))
    y_p = _out_ffn(x1_p, y5_p, ym_p, *ffn2, rows=n_p, tm=tile_p, seg_layout=seg_p)
    y_s = _out_ffn(x1_s, y5_s, ym_s, *ffn2, rows=n_s, tm=n_s, seg_layout=seg_s)

    def s5_state(s, n):
        return s.reshape(1, n, S5_GROUPS, S5_STATE)

    return (y_p.reshape(nb, seq, D_MODEL), y_s.reshape(ns, dseq, D_MODEL),
            s5_state(pre_s, nb), s5_state(pim_s, nb), c_p[None], nn_p[None], m_p[:, 0][None], cv_p[None],
            s5_state(sre_s, ns), s5_state(sim_s, ns), c_s[None], nn_s[None], m_s[:, 0][None], cv_s[None])
```

```python
import functools

import jax
import jax.numpy as jnp
from jax import lax
from jax.experimental import pallas as pl
from jax.experimental.pallas import tpu as pltpu

F32 = jnp.float32
BF16 = jnp.bfloat16

D_MODEL = 1024
D_FF = 2816
S5_GROUPS = 64
S5_GROUP = 16
S5_STATE = 64
S5_LANES = S5_GROUPS * S5_STATE
MXU_TILE = 256
S5_BLOCKS = 8
S5_BLOCK_GROUPS = S5_GROUPS // S5_BLOCKS
S5_BLOCK_IN = D_MODEL // S5_BLOCKS
S5_BLOCK_ST = S5_LANES // S5_BLOCKS
ML_HEADS = 4
ML_HEAD_DIM = 256
ML_CONV = 4
ML_QKV_BLOCK = 4
EPS = 1e-6
SUBLANES = 8
GATE_LANES = 128
POW_ROWS = 64
S5_T = 2
VMEM_LIMIT = 56 * 1024 * 1024

TOKEN_TILE = 512
ML_CHUNK = 256
ML_TILE = 256
ML_SAMPLE_GROUP = 4


def _rms(x, g):
    return x * lax.rsqrt(jnp.mean(x * x, axis=-1, keepdims=True) + EPS) * g


def _silu(x):
    h = 0.5 * x
    return h + h * jnp.tanh(h)


def _swiglu(h, wg_ref, wu_ref, wd_ref, acc_ref, ff_chunk):
    for c in range(D_FF // ff_chunk):
        sl = slice(c * ff_chunk, (c + 1) * ff_chunk)
        g = jnp.dot(h, wg_ref[:, sl], preferred_element_type=F32)
        u = jnp.dot(h, wu_ref[:, sl], preferred_element_type=F32)
        a = (_silu(g) * u).astype(BF16)
        d = jnp.dot(a, wd_ref[sl, :], preferred_element_type=F32)
        if c == 0:
            acc_ref[...] = d
        else:
            acc_ref[...] += d
    return acc_ref[...]


def _const_spec(shape):
    nd = len(shape)
    return pl.BlockSpec(shape, lambda *_: (0,) * nd, pipeline_mode=pl.Buffered(1))


def _segment_rows(x, layout, inverse=False):
    parts = []
    for row0, r in layout:
        n = SUBLANES * r
        g = x[row0:row0 + n]
        if r % SUBLANES == 0:
            shape = (r, SUBLANES) if inverse else (SUBLANES, r)
            g = jnp.swapaxes(g.reshape(shape + g.shape[1:]), 0, 1).reshape(g.shape)
        elif inverse:
            g = jnp.concatenate([g[i * SUBLANES + a:i * SUBLANES + a + 1]
                                 for a in range(SUBLANES) for i in range(r)], axis=0)
        else:
            g = jnp.concatenate([g[a * r + i:a * r + i + 1]
                                 for i in range(r) for a in range(SUBLANES)], axis=0)
        parts.append(g)
    return parts[0] if len(parts) == 1 else jnp.concatenate(parts, axis=0)


def _ffn_in_kernel(x_ref, g1_ref, wg_ref, wu_ref, wd_ref, g2_ref, win_ref,
                   x1_ref, u_ref, xm_ref, z_ref, acc_ref, *, ff_chunk, seg_layout):
    x = x_ref[...]
    h = _rms(x, g1_ref[...]).astype(BF16)
    x1 = x + 0.5 * _swiglu(h, wg_ref, wu_ref, wd_ref, acc_ref, ff_chunk)
    x1_ref[...] = x1
    h2f = _rms(x1, g2_ref[...])
    h2 = h2f.astype(BF16)
    h2s = _segment_rows(h2f, seg_layout).astype(BF16)
    u_ref[...] = jnp.dot(h2s, win_ref[:, 0:D_MODEL], preferred_element_type=F32)
    xm_ref[...] = jnp.dot(h2, win_ref[:, D_MODEL:2 * D_MODEL], preferred_element_type=F32)
    z_ref[...] = jnp.dot(h2, win_ref[:, 2 * D_MODEL:3 * D_MODEL], preferred_element_type=F32)


def _ffn_in(x, g1, wg, wu, wd, g2, win, *, tm, seg_layout, ff_chunk=256):
    rows = x.shape[0]
    row_spec = pl.BlockSpec((tm, D_MODEL), lambda i: (i, 0))
    out = jax.ShapeDtypeStruct((rows, D_MODEL), F32)
    return pl.pallas_call(
        functools.partial(_ffn_in_kernel, ff_chunk=ff_chunk, seg_layout=seg_layout),
        grid=(rows // tm,),
        in_specs=[row_spec, _const_spec(g1.shape), _const_spec(wg.shape), _const_spec(wu.shape),
                  _const_spec(wd.shape), _const_spec(g2.shape), _const_spec(win.shape)],
        out_specs=[row_spec] * 4,
        out_shape=[out] * 4,
        scratch_shapes=[pltpu.VMEM((tm, D_MODEL), F32)],
        compiler_params=pltpu.CompilerParams(dimension_semantics=("parallel",), vmem_limit_bytes=VMEM_LIMIT),
        name="ffn_in",
    )(x, g1, wg, wu, wd, g2, win)


def _out_ffn_kernel(x1_ref, y5_ref, ym_ref, wo_ref, g_ref, wg_ref, wu_ref, wd_ref, gf_ref,
                    o_ref, acc_ref, *, ff_chunk, seg_layout):
    p5 = jnp.dot(y5_ref[...], wo_ref[0:D_MODEL, :], preferred_element_type=F32)
    x2 = (x1_ref[...] + _segment_rows(p5, seg_layout, inverse=True)
          + jnp.dot(ym_ref[...], wo_ref[D_MODEL:2 * D_MODEL, :], preferred_element_type=F32))
    h = _rms(x2, g_ref[...]).astype(BF16)
    x3 = x2 + 0.5 * _swiglu(h, wg_ref, wu_ref, wd_ref, acc_ref, ff_chunk)
    o_ref[...] = _rms(x3, gf_ref[...])


def _out_ffn(x1, y5, ym, wo, g, wg, wu, wd, gf, *, rows, tm, seg_layout, ff_chunk=256):
    row_spec = pl.BlockSpec((tm, D_MODEL), lambda i: (i, 0))
    return pl.pallas_call(
        functools.partial(_out_ffn_kernel, ff_chunk=ff_chunk, seg_layout=seg_layout),
        grid=(rows // tm,),
        in_specs=[row_spec, row_spec, row_spec, _const_spec(wo.shape), _const_spec(g.shape),
                  _const_spec(wg.shape), _const_spec(wu.shape), _const_spec(wd.shape), _const_spec(gf.shape)],
        out_specs=row_spec,
        out_shape=jax.ShapeDtypeStruct((rows, D_MODEL), F32),
        scratch_shapes=[pltpu.VMEM((tm, D_MODEL), F32)],
        compiler_params=pltpu.CompilerParams(dimension_semantics=("parallel",), vmem_limit_bytes=VMEM_LIMIT),
        name="out_ffn",
    )(x1, y5, ym, wo, g, wg, wu, wd, gf)


def _lam_bar(lr, li, ldt):
    dt = jnp.exp(ldt)
    mag = jnp.exp(lr * dt)
    th = li * dt
    return mag * jnp.cos(th), mag * jnp.sin(th), dt


def _cmul(ar, ai, br, bi):
    return ar * br - ai * bi, ar * bi + ai * br


def _dot_split(a, b):
    a_hi = a.astype(BF16)
    b_hi = b.astype(BF16)
    a_lo = (a - a_hi.astype(F32)).astype(BF16)
    b_lo = (b - b_hi.astype(F32)).astype(BF16)
    return (jnp.dot(a_hi, b_hi, preferred_element_type=F32) + jnp.dot(a_hi, b_lo, preferred_element_type=F32)
            + jnp.dot(a_lo, b_hi, preferred_element_type=F32))


def _s5_prep_kernel(lre_ref, lim_ref, ldt_ref, bre_ref, bim_ref, lre_t_ref, lim_t_ref, ldt_t_ref, cre_ref, cim_ref,
                    pre_ref, pim_ref, bcat_ref, wout_ref):
    lr = lre_ref[...]
    li = lim_ref[...]
    ar, ai, _ = _lam_bar(lr, li, ldt_ref[...])
    nr = ar - 1.0
    den = lr * lr + li * li
    cr = (nr * lr + ai * li) / den
    ci = (ai * lr - nr * li) / den
    bs = [_cmul(cr, ci, bre_ref[...], bim_ref[...])]
    for _ in range(1, S5_T):
        bs.append(_cmul(ar, ai, *bs[-1]))

    pr, pi = ar, ai
    pre_ref[0:1, :] = pr
    pim_ref[0:1, :] = pi
    for j in range(1, POW_ROWS):
        pr, pi = _cmul(pr, pi, ar, ai)
        pre_ref[j:j + 1, :] = pr
        pim_ref[j:j + 1, :] = pi

    tr, ti, _ = _lam_bar(lre_t_ref[...], lim_t_ref[...], ldt_t_ref[...])
    cs = [(cre_ref[...], cim_ref[...])]
    for _ in range(S5_T):
        cs.append(_cmul(tr, ti, *cs[-1]))

    g_shift, s_shift = S5_GROUP.bit_length() - 1, S5_STATE.bit_length() - 1
    b_shape = (S5_BLOCK_IN, S5_BLOCK_ST)
    b_mask = (lax.broadcasted_iota(jnp.int32, b_shape, 0) >> g_shift) == (
        lax.broadcasted_iota(jnp.int32, b_shape, 1) >> s_shift)
    c_shape = (S5_BLOCK_ST, S5_BLOCK_IN)
    c_mask = (lax.broadcasted_iota(jnp.int32, c_shape, 0) >> s_shift) == (
        lax.broadcasted_iota(jnp.int32, c_shape, 1) >> g_shift)

    def b_block(br, bi, blk):
        parts = []
        for x in (br, bi):
            x = jnp.concatenate([x[:, blk * S5_BLOCK_ST:(blk + 1) * S5_BLOCK_ST]] * S5_BLOCK_GROUPS, axis=0)
            parts.append(jnp.where(b_mask, x, 0.0))
        return jnp.concatenate(parts, axis=1)

    def c_block(xr, xi, blk):
        parts = []
        for x in (xr, -xi):
            x = jnp.concatenate([x[:, blk * S5_BLOCK_IN:(blk + 1) * S5_BLOCK_IN]] * S5_BLOCK_GROUPS, axis=0)
            parts.append(jnp.where(c_mask, x, 0.0))
        return jnp.concatenate(parts, axis=0)

    zero = jnp.zeros((S5_BLOCK_IN, S5_BLOCK_IN), F32)
    z_rows = 2 * S5_BLOCK_ST
    for blk in range(S5_BLOCKS):
        bms = [b_block(br, bi, blk) for br, bi in bs]
        cms = [c_block(xr, xi, blk) for xr, xi in cs]
        kcat = _dot_split(bms[0], jnp.concatenate(cms[:S5_T], axis=1))
        ks = [kcat[:, j * S5_BLOCK_IN:(j + 1) * S5_BLOCK_IN] for j in range(S5_T)]
        for f in range(S5_T):
            bcat_ref[blk, f * S5_BLOCK_IN:(f + 1) * S5_BLOCK_IN, :] = bms[S5_T - 1 - f].astype(BF16)
            row0 = z_rows + f * S5_BLOCK_IN
            wout_ref[blk, row0:row0 + S5_BLOCK_IN, :] = jnp.concatenate(
                [ks[g - f] if g >= f else zero for g in range(S5_T)], axis=1).astype(BF16)
        wout_ref[blk, 0:z_rows, :] = jnp.concatenate(cms[1:], axis=1).astype(BF16)


def _s5_prep(lre, lim, ldt, bre_t, bim_t, lre_t, lim_t, ldt_t, cre_t, cim_t):
    pow_shape = jax.ShapeDtypeStruct((POW_ROWS, S5_LANES), F32)
    return pl.pallas_call(
        _s5_prep_kernel,
        out_shape=[pow_shape, pow_shape,
                   jax.ShapeDtypeStruct((S5_BLOCKS, S5_T * S5_BLOCK_IN, 2 * S5_BLOCK_ST), BF16),
                   jax.ShapeDtypeStruct((S5_BLOCKS, 2 * S5_BLOCK_ST + S5_T * S5_BLOCK_IN, S5_T * S5_BLOCK_IN), BF16)],
        compiler_params=pltpu.CompilerParams(vmem_limit_bytes=VMEM_LIMIT),
        name="s5_prep",
    )(lre, lim, ldt, bre_t, bim_t, lre_t, lim_t, ldt_t, cre_t, cim_t)


def _ml_prep_kernel(wq_ref, wk_ref, wv_ref, oq_ref, ok_ref, ov_ref):
    shape = (MXU_TILE, MXU_TILE)
    shift = ML_QKV_BLOCK.bit_length() - 1
    row = lax.broadcasted_iota(jnp.int32, shape, 0)
    col = lax.broadcasted_iota(jnp.int32, shape, 1)
    same = (row >> shift) == (col >> shift)
    sel = col & (ML_QKV_BLOCK - 1)
    for w_ref, o_ref in ((wq_ref, oq_ref), (wk_ref, ok_ref), (wv_ref, ov_ref)):
        for c in range(D_MODEL // MXU_TILE):
            w = w_ref[c * MXU_TILE:(c + 1) * MXU_TILE, :]
            acc = jnp.zeros(shape, F32)
            for o in range(ML_QKV_BLOCK):
                acc = jnp.where(sel == o, w[:, o:o + 1], acc)
            o_ref[c] = jnp.where(same, acc, 0.0).astype(BF16)


def _ml_prep(wq, wk, wv):
    out = jax.ShapeDtypeStruct((D_MODEL // MXU_TILE, MXU_TILE, MXU_TILE), BF16)
    return pl.pallas_call(_ml_prep_kernel, out_shape=[out] * 3, name="ml_prep")(wq, wk, wv)


def _block_diag_dot(x, w_ref):
    return jnp.concatenate(
        [jnp.dot(x[:, c * MXU_TILE:(c + 1) * MXU_TILE], w_ref[c], preferred_element_type=F32)
         for c in range(w_ref.shape[0])], axis=1)


def _s5_kernel(u_ref, sre0_ref, sim0_ref, pre_ref, pim_ref, bcat_ref, wout_ref, d_ref, gw_ref, gb_ref, on_ref,
               y_ref, sre_ref, sim_ref, s_ref, *, tm, independent, n_tiles):
    r = tm // SUBLANES
    nc = r // S5_T
    rows_c = nc * SUBLANES
    t = pl.program_id(1)

    @pl.when(t == 0)
    def _():
        sre_ref[...] = jnp.broadcast_to(sre0_ref[...], sre_ref.shape)
        sim_ref[...] = jnp.broadcast_to(sim0_ref[...], sim_ref.shape)

    ups = [u_ref[g] for g in range(n_tiles)]
    us = []
    for f in range(S5_T):
        us.append(jnp.concatenate(
            [up.reshape(nc, S5_T * SUBLANES, D_MODEL)[:, f * SUBLANES:(f + 1) * SUBLANES, :].reshape(rows_c, D_MODEL)
             for up in ups], axis=0).astype(BF16))

    ys = [[] for _ in range(S5_T)]
    for b in range(S5_BLOCKS):
        lanes = slice(b * S5_BLOCK_ST, (b + 1) * S5_BLOCK_ST)
        chans = slice(b * S5_BLOCK_IN, (b + 1) * S5_BLOCK_IN)
        re = slice(0, S5_BLOCK_ST)
        im = slice(S5_BLOCK_ST, 2 * S5_BLOCK_ST)
        ucat = jnp.concatenate([u[:, chans] for u in us], axis=1)
        s_ref[...] = jnp.dot(ucat, bcat_ref[b], preferred_element_type=F32)
        ltr = jnp.broadcast_to(pre_ref[S5_T - 1:S5_T, lanes], (SUBLANES, S5_BLOCK_ST))
        lti = jnp.broadcast_to(pim_ref[S5_T - 1:S5_T, lanes], (SUBLANES, S5_BLOCK_ST))

        def scan(cr, ci, row0):
            for c in range(nc):
                rows = slice(row0 + c * SUBLANES, row0 + (c + 1) * SUBLANES)
                nr = ltr * cr - lti * ci + s_ref[rows, re]
                ni = ltr * ci + lti * cr + s_ref[rows, im]
                s_ref[rows, re] = cr
                s_ref[rows, im] = ci
                cr, ci = nr, ni
            return cr, ci

        for g in range(n_tiles):
            row0 = g * rows_c
            if independent:
                fr, fi = scan(sre_ref[g, :, lanes], sim_ref[g, :, lanes], row0)
                sre_ref[g, :, lanes] = fr
                sim_ref[g, :, lanes] = fi
            else:
                zero = jnp.zeros((SUBLANES, S5_BLOCK_ST), F32)
                fr, fi = scan(zero, zero, row0)
                rr = pre_ref[r - 1:r, lanes]
                ri = pim_ref[r - 1:r, lanes]
                cr = sre_ref[g, :, lanes]
                ci = sim_ref[g, :, lanes]
                rows_r, rows_i = [], []
                for a in range(SUBLANES):
                    rows_r.append(cr)
                    rows_i.append(ci)
                    cr, ci = rr * cr - ri * ci + fr[a:a + 1], rr * ci + ri * cr + fi[a:a + 1]
                sre_ref[g, :, lanes] = cr
                sim_ref[g, :, lanes] = ci
                cin_r = jnp.concatenate(rows_r, axis=0)
                cin_i = jnp.concatenate(rows_i, axis=0)
                s_ref[row0:row0 + SUBLANES, re] = cin_r
                s_ref[row0:row0 + SUBLANES, im] = cin_i
                for c in range(1, nc):
                    rows = slice(row0 + c * SUBLANES, row0 + (c + 1) * SUBLANES)
                    pr = pre_ref[S5_T * c - 1:S5_T * c, lanes]
                    pi = pim_ref[S5_T * c - 1:S5_T * c, lanes]
                    s_ref[rows, re] += pr * cin_r - pi * cin_i
                    s_ref[rows, im] += pr * cin_i + pi * cin_r
        yb = jnp.dot(jnp.concatenate([s_ref[...].astype(BF16), ucat], axis=1), wout_ref[b],
                     preferred_element_type=F32)
        for f in range(S5_T):
            ys[f].append(yb[:, f * S5_BLOCK_IN:(f + 1) * S5_BLOCK_IN])

    yf = [jnp.concatenate(ys[f], axis=1) for f in range(S5_T)]
    for g in range(n_tiles):
        rows = slice(g * rows_c, (g + 1) * rows_c)
        y = jnp.concatenate([y[rows].reshape(nc, SUBLANES, D_MODEL) for y in yf], axis=1).reshape(tm, D_MODEL)
        y = y + d_ref[...] * ups[g]
        gl = jax.nn.gelu(y)
        o = gl * jax.nn.sigmoid(jnp.dot(gl.astype(BF16), gw_ref[...], preferred_element_type=F32) + gb_ref[...])
        y_ref[g] = _rms(o, on_ref[...]).astype(BF16)


def _s5(u, sre0, sim0, consts, *, n_outer, n_tiles, steps, tm, independent=False, shared_init=False):
    srows = SUBLANES if independent else 1
    u5 = u.reshape(n_outer, n_tiles, steps, tm, D_MODEL)
    seq_spec = pl.BlockSpec((None, n_tiles, None, tm, D_MODEL), lambda o, t: (o, 0, t, 0, 0))
    if shared_init:
        st_in = pl.BlockSpec((None, 1, srows, S5_LANES), lambda o, t: (0, 0, 0, 0))
    else:
        st_in = pl.BlockSpec((None, n_tiles, srows, S5_LANES), lambda o, t: (o, 0, 0, 0))
    st_out = pl.BlockSpec((None, n_tiles, srows, S5_LANES), lambda o, t: (o, 0, 0, 0))
    st_shape = jax.ShapeDtypeStruct((n_outer, n_tiles, srows, S5_LANES), F32)
    y, sre, sim = pl.pallas_call(
        functools.partial(_s5_kernel, tm=tm, independent=independent, n_tiles=n_tiles),
        grid=(n_outer, steps),
        in_specs=[seq_spec, st_in, st_in] + [_const_spec(a.shape) for a in consts],
        out_specs=[seq_spec, st_out, st_out],
        out_shape=[jax.ShapeDtypeStruct(u5.shape, BF16), st_shape, st_shape],
        scratch_shapes=[pltpu.VMEM((n_tiles * tm // S5_T, 2 * S5_BLOCK_ST), F32)],
        compiler_params=pltpu.CompilerParams(dimension_semantics=("parallel", "arbitrary"),
                                             vmem_limit_bytes=VMEM_LIMIT),
        name="s5_mix",
    )(u5, sre0, sim0, *consts)
    return y.reshape(u.shape), sre, sim


def _ml_front(xm_ref, cv_ref, xbuf_ref, consts, bufs, tm, n_tiles):
    cw_ref, cb_ref, bdq_ref, bdk_ref, bdv_ref, gw_ref, gb_ref = consts
    xc_ref, q_ref, k_ref, v_ref, vf_ref, g_ref = bufs
    pre = ML_CONV - 1
    cw = cw_ref[...]
    xms, xcs = [], []
    for g in range(n_tiles):
        xm = xm_ref[g]
        xbuf_ref[g, SUBLANES - pre:SUBLANES, :] = cv_ref[g]
        xbuf_ref[g, SUBLANES:SUBLANES + tm, :] = xm
        xc = cb_ref[...] + cw[pre:pre + 1] * xm
        for j in range(pre):
            xc = xc + cw[j:j + 1] * xbuf_ref[g, SUBLANES - pre + j:SUBLANES - pre + j + tm, :]
        cv_ref[g] = xbuf_ref[g, SUBLANES + tm - pre:SUBLANES + tm, :]
        xms.append(xm)
        xcs.append(_silu(xc))
    xm = xms[0] if n_tiles == 1 else jnp.concatenate(xms, axis=0)
    xc = xcs[0] if n_tiles == 1 else jnp.concatenate(xcs, axis=0)
    xc_ref[...] = xc

    xcb = xc.astype(BF16)
    q = _block_diag_dot(xcb, bdq_ref)
    k = _block_diag_dot(xcb, bdk_ref)
    v = _block_diag_dot(xm.astype(BF16), bdv_ref)
    qb = q.astype(BF16)
    vb = v.astype(BF16)
    g_ref[...] = jnp.dot(jnp.concatenate([qb, k.astype(BF16), vb], axis=1), gw_ref[...],
                         preferred_element_type=F32) + gb_ref[...]
    q_ref[...] = qb
    k_ref[...] = (k * (ML_HEAD_DIM ** -0.5)).astype(BF16)
    v_ref[...] = vb
    vf_ref[...] = v


def _ml_back(bufs, z_ref, state, consts, y_ref, h_ref, tm, tc, n_tiles):
    xc_ref, q_ref, k_ref, v_ref, vf_ref, g_ref = bufs
    c_ref, n_ref, m_ref = state
    nw_ref, sk_ref, on_ref = consts
    lane = lax.broadcasted_iota(jnp.int32, (tc, GATE_LANES), 1)
    row = lax.broadcasted_iota(jnp.int32, (tc, tc), 0)
    col = lax.broadcasted_iota(jnp.int32, (tc, tc), 1)
    causal = row >= col
    tril = causal.astype(F32)

    def chunk(rows, g):
        gates = g_ref[rows, :]
        lf = jnp.minimum(gates, 0.0) - jnp.log1p(jnp.exp(-jnp.abs(gates)))
        lf = jnp.where((lane >= ML_HEADS) & (lane < 2 * ML_HEADS), lf, 0.0)
        cum = jnp.dot(tril, lf, preferred_element_type=F32, precision=lax.Precision.HIGHEST)
        arr = jnp.where(lane < ML_HEADS, gates, cum)
        arr_t = arr.T
        for h in range(ML_HEADS):
            hs = slice(h * ML_HEAD_DIM, (h + 1) * ML_HEAD_DIM)
            qh = q_ref[rows, hs]
            kh = k_ref[rows, hs]
            vh = v_ref[rows, hs]
            ig_col = arr[:, h:h + 1]
            b_col = arr[:, ML_HEADS + h:ML_HEADS + h + 1]
            ig_row = arr_t[h:h + 1, :]
            b_row = arr_t[ML_HEADS + h:ML_HEADS + h + 1, :]
            m_prev = m_ref[g, :, h:h + 1]
            c_prev = c_ref[g, h]
            n_prev = n_ref[g, h:h + 1, :]

            logw = jnp.where(causal, b_col - b_row + ig_row, -jnp.inf)
            log_inter = b_col + m_prev
            m_t = jnp.maximum(log_inter, jnp.max(logw, axis=-1, keepdims=True))
            w = jnp.exp(logw - m_t)
            a_inter = jnp.exp(log_inter - m_t)
            s = lax.dot_general(qh, kh, (((1,), (1,)), ((), ())), preferred_element_type=F32) * w
            inter = lax.dot_general(qh, c_prev.astype(BF16), (((1,), (1,)), ((), ())),
                                    preferred_element_type=F32)
            num = a_inter * inter + jnp.dot(s.astype(BF16), vh, preferred_element_type=F32)
            den = (a_inter * jnp.sum(qh.astype(F32) * n_prev, axis=-1, keepdims=True)
                   + jnp.sum(s, axis=-1, keepdims=True))
            hh = num / jnp.maximum(jnp.abs(den), jnp.exp(-m_t))
            mu = jnp.mean(hh, axis=-1, keepdims=True)
            hc = hh - mu
            var = jnp.mean(hc * hc, axis=-1, keepdims=True)
            h_ref[rows, hs] = hc * lax.rsqrt(var + EPS)

            b_last = b_col[tc - 1:tc, :]
            m_new = m_t[tc - 1:tc, :]
            g_state = jnp.exp(b_last + m_prev - m_new)
            g_src = jnp.exp(b_last - b_col + ig_col - m_new)
            vs = (vf_ref[rows, hs] * g_src).astype(BF16)
            c_ref[g, h] = g_state * c_prev + lax.dot_general(vs, kh, (((0,), (0,)), ((), ())),
                                                             preferred_element_type=F32)
            n_ref[g, h:h + 1, :] = g_state * n_prev + jnp.sum(g_src * kh.astype(F32), axis=0, keepdims=True)
            m_ref[g, :, h:h + 1] = m_new

    for j in range(tm // tc):
        for g in range(n_tiles):
            chunk(slice(g * tm + j * tc, g * tm + (j + 1) * tc), g)

    for g in range(n_tiles):
        rows = slice(g * tm, (g + 1) * tm)
        out = (h_ref[rows, :] * nw_ref[...] + sk_ref[...] * xc_ref[rows, :]) * _silu(z_ref[g])
        y_ref[g] = _rms(out, on_ref[...]).astype(BF16)


def _mlstm_kernel(xm_ref, z_ref, c0_ref, n0_ref, m0_ref, cv0_ref, cw_ref, cb_ref, bdq_ref, bdk_ref, bdv_ref,
                  gw_ref, gb_ref, nw_ref, sk_ref, on_ref,
                  y_ref, c_ref, n_ref, m_ref, cv_ref, xbuf_ref, h_ref, *bufs, tm, tc, n_tiles, pipelined):
    t = pl.program_id(1)
    front_consts = (cw_ref, cb_ref, bdq_ref, bdk_ref, bdv_ref, gw_ref, gb_ref)
    back_consts = (nw_ref, sk_ref, on_ref)
    state = (c_ref, n_ref, m_ref)
    n_buf = len(bufs) // 2 if pipelined else len(bufs)

    @pl.when(t == 0)
    def _():
        cv_ref[...] = jnp.broadcast_to(cv0_ref[...], cv_ref.shape)
        if pipelined:
            for ref in bufs[n_buf:]:
                ref[...] = jnp.zeros(ref.shape, ref.dtype)

    @pl.when(t <= (1 if pipelined else 0))
    def _():
        c_ref[...] = jnp.broadcast_to(c0_ref[...], c_ref.shape)
        n_ref[...] = jnp.broadcast_to(n0_ref[...], n_ref.shape)
        m_ref[...] = jnp.broadcast_to(m0_ref[...], m_ref.shape)

    def step(front_set, back_set):
        _ml_front(xm_ref, cv_ref, xbuf_ref, front_consts, front_set, tm, n_tiles)
        _ml_back(back_set, z_ref, state, back_consts, y_ref, h_ref, tm, tc, n_tiles)

    if pipelined:
        pl.when(t % 2 == 0)(lambda: step(bufs[:n_buf], bufs[n_buf:]))
        pl.when(t % 2 == 1)(lambda: step(bufs[n_buf:], bufs[:n_buf]))
    else:
        step(bufs, bufs)


def _mlstm(xm, z, c0, n0, m0, cv0, consts, *, n_outer, n_tiles, steps, tm, tc, shared_init=False, pipelined=False):
    shape5 = (n_outer, n_tiles, steps, tm, D_MODEL)
    block = (None, n_tiles, None, tm, D_MODEL)
    if pipelined:
        grid_steps = steps + 1
        front_spec = pl.BlockSpec(block, lambda o, t: (o, 0, jnp.minimum(t, steps - 1), 0, 0))
        back_spec = pl.BlockSpec(block, lambda o, t: (o, 0, jnp.maximum(t - 1, 0), 0, 0))
    else:
        grid_steps = steps
        front_spec = back_spec = pl.BlockSpec(block, lambda o, t: (o, 0, t, 0, 0))

    def st(shape, shared):
        zeros = (0,) * len(shape)
        if shared:
            return pl.BlockSpec((None, 1) + shape, lambda o, t: (0, 0) + zeros)
        return pl.BlockSpec((None, n_tiles) + shape, lambda o, t: (o, 0) + zeros)

    st_shapes = ((ML_HEADS, ML_HEAD_DIM, ML_HEAD_DIM), (ML_HEADS, ML_HEAD_DIM), (1, ML_HEADS),
                 (ML_CONV - 1, D_MODEL))
    rows = n_tiles * tm
    buf_set = [pltpu.VMEM((rows, D_MODEL), F32), pltpu.VMEM((rows, D_MODEL), BF16), pltpu.VMEM((rows, D_MODEL), BF16),
               pltpu.VMEM((rows, D_MODEL), BF16), pltpu.VMEM((rows, D_MODEL), F32), pltpu.VMEM((rows, GATE_LANES), F32)]
    outs = pl.pallas_call(
        functools.partial(_mlstm_kernel, tm=tm, tc=tc, n_tiles=n_tiles, pipelined=pipelined),
        grid=(n_outer, grid_steps),
        in_specs=[front_spec, back_spec] + [st(s, shared_init) for s in st_shapes]
                 + [_const_spec(a.shape) for a in consts],
        out_specs=[back_spec] + [st(s, False) for s in st_shapes],
        out_shape=[jax.ShapeDtypeStruct(shape5, BF16)]
                  + [jax.ShapeDtypeStruct((n_outer, n_tiles) + s, F32) for s in st_shapes],
        scratch_shapes=[pltpu.VMEM((n_tiles, tm + 2 * SUBLANES, D_MODEL), F32), pltpu.VMEM((rows, D_MODEL), F32)]
                       + buf_set * (2 if pipelined else 1),
        compiler_params=pltpu.CompilerParams(dimension_semantics=("parallel", "arbitrary"),
                                             vmem_limit_bytes=VMEM_LIMIT),
        name="mlstm_mix",
    )(xm.reshape(shape5), z.reshape(shape5), c0, n0, m0, cv0, *consts)
    n_streams = n_outer * n_tiles
    return (outs[0].reshape(xm.shape),) + tuple(o.reshape((n_streams,) + o.shape[2:]) for o in outs[1:])


def _row(v):
    return v.reshape(1, -1).astype(F32)


def _pad_lanes(v, width):
    return jnp.pad(v, [(0, 0)] * (v.ndim - 1) + [(0, width - v.shape[-1])])


def kernel(x_prompt, x_sample, state_s5_re, state_s5_im, state_mlstm_c, state_mlstm_n, state_mlstm_m,
           state_mlstm_conv, meta_tokens, norm_ffn1, ffn1_gate, ffn1_up, ffn1_down, norm_mix, w_in,
           s5_lambda_re, s5_lambda_im, s5_log_dt, s5_b_re, s5_b_im, s5_c_re, s5_c_im, s5_d, s5_glu_w, s5_glu_b,
           ml_conv_w, ml_conv_b, ml_wq, ml_wk, ml_wv, ml_igate_w, ml_igate_b, ml_fgate_w, ml_fgate_b,
           ml_norm_w, ml_skip, out_norm_s5, out_norm_ml, w_out, norm_ffn2, ffn2_gate, ffn2_up, ffn2_down,
           norm_final):
    nb, seq, _ = x_prompt.shape
    ns, dseq, _ = x_sample.shape
    n_meta = meta_tokens.shape[0]
    n_p, n_s = nb * seq, ns * dseq
    tile_p = min(TOKEN_TILE, seq)

    wg1, wu1, wd1 = ffn1_gate[0].astype(BF16), ffn1_up[0].astype(BF16), ffn1_down[0].astype(BF16)
    wg2, wu2, wd2 = ffn2_gate[0].astype(BF16), ffn2_up[0].astype(BF16), ffn2_down[0].astype(BF16)
    win = w_in[0].astype(BF16)
    wo = w_out[0].astype(BF16)
    glu_w = s5_glu_w[0].astype(BF16)

    ldt_gs = jnp.broadcast_to(s5_log_dt[0][:, None], (S5_GROUPS, S5_STATE))

    def state_major(a):
        return jnp.repeat(a.T, S5_GROUP, axis=1)

    pre, pim, bcat, wout = _s5_prep(
        s5_lambda_re[0].reshape(1, S5_LANES), s5_lambda_im[0].reshape(1, S5_LANES), ldt_gs.reshape(1, S5_LANES),
        s5_b_re[0].reshape(S5_LANES, S5_GROUP).T, s5_b_im[0].reshape(S5_LANES, S5_GROUP).T,
        state_major(s5_lambda_re[0]), state_major(s5_lambda_im[0]), state_major(ldt_gs),
        s5_c_re[0].reshape(D_MODEL, S5_STATE).T, s5_c_im[0].reshape(D_MODEL, S5_STATE).T)
    s5_consts = (pre, pim, bcat, wout, _row(s5_d[0]), glu_w, _row(s5_glu_b[0]), _row(out_norm_s5[0]))

    bdq, bdk, bdv = _ml_prep(ml_wq[0].reshape(D_MODEL, ML_QKV_BLOCK), ml_wk[0].reshape(D_MODEL, ML_QKV_BLOCK),
                             ml_wv[0].reshape(D_MODEL, ML_QKV_BLOCK))
    gate_w = _pad_lanes(jnp.concatenate([ml_igate_w[0], ml_fgate_w[0]], axis=1), GATE_LANES).astype(BF16)
    gate_b = _pad_lanes(jnp.concatenate([ml_igate_b[0], ml_fgate_b[0]])[None, :], GATE_LANES)
    ml_consts = (ml_conv_w[0], _row(ml_conv_b[0]), bdq, bdk, bdv, gate_w, gate_b, _row(ml_norm_w[0]),
                 _row(ml_skip[0]), _row(out_norm_ml[0]))

    ffn1 = (_row(norm_ffn1[0]), wg1, wu1, wd1, _row(norm_mix[0]), win)
    seg_p = ((0, tile_p // SUBLANES),)
    seg_s = tuple((g * SUBLANES * dseq, dseq) for g in range(ns // SUBLANES))
    pad_m = max(SUBLANES * S5_T - n_meta, 0)
    tm_m = n_meta + pad_m
    seg_m = ((n_s, tm_m // SUBLANES),)
    x1_p, u_p, xm_p, z_p = _ffn_in(x_prompt.reshape(n_p, D_MODEL), *ffn1, tm=tile_p, seg_layout=seg_p)
    small = jnp.concatenate([x_sample.reshape(n_s, D_MODEL), jnp.zeros((pad_m, D_MODEL), F32), meta_tokens], axis=0)
    x1_s, u_s, xm_s, z_s = _ffn_in(small, *ffn1, tm=small.shape[0], seg_layout=seg_s + seg_m)

    zs5 = jnp.zeros((1, 1, 1, S5_LANES), F32)
    _, mre, mim = _s5(u_s[n_s:], zs5, zs5, s5_consts, n_outer=1, n_tiles=1, steps=1, tm=tm_m)
    y5_p, pre_s, pim_s = _s5(u_p, mre, mim, s5_consts, n_outer=1, n_tiles=nb, steps=seq // tile_p, tm=tile_p,
                             shared_init=True)
    n_grp = ns // SUBLANES
    y5_s, sre_s, sim_s = _s5(u_s[:n_s], state_s5_re[0].reshape(1, n_grp, SUBLANES, S5_LANES),
                             state_s5_im[0].reshape(1, n_grp, SUBLANES, S5_LANES), s5_consts,
                             n_outer=1, n_tiles=n_grp, steps=1, tm=SUBLANES * dseq, independent=True)

    ml_tile_p = min(ML_TILE, seq)
    zc = jnp.zeros((1, 1, ML_HEADS, ML_HEAD_DIM, ML_HEAD_DIM), F32)
    zn = jnp.zeros((1, 1, ML_HEADS, ML_HEAD_DIM), F32)
    zm = jnp.zeros((1, 1, 1, ML_HEADS), F32)
    zcv = jnp.zeros((1, 1, ML_CONV - 1, D_MODEL), F32)
    m_rows = slice(n_s + pad_m, n_s + pad_m + n_meta)
    _, c_m, n_m, m_m, cv_m = _mlstm(xm_s[m_rows], z_s[m_rows], zc, zn, zm, zcv, ml_consts, n_outer=1, n_tiles=1,
                                    steps=1, tm=n_meta, tc=n_meta)
    ym_p, c_p, nn_p, m_p, cv_p = _mlstm(xm_p, z_p, c_m[None], n_m[None], m_m[None], cv_m[None], ml_consts,
                                        n_outer=1, n_tiles=nb, steps=seq // ml_tile_p, tm=ml_tile_p,
                                        tc=min(ML_CHUNK, ml_tile_p), shared_init=True, pipelined=True)
    grp = ML_SAMPLE_GROUP
    st5 = lambda a: a.reshape((ns // grp, grp) + a.shape[1:])
    ym_s, c_s, nn_s, m_s, cv_s = _mlstm(xm_s[:n_s], z_s[:n_s], st5(state_mlstm_c[0]), st5(state_mlstm_n[0]),
                                        st5(state_mlstm_m[0][:, None, :]), st5(state_mlstm_conv[0]), ml_consts,
                                        n_outer=ns // grp, n_tiles=grp, steps=1, tm=dseq, tc=dseq)

    ffn2 = (wo, _row(norm_ffn2[0]), wg2, wu2, wd2, _row(norm_final))
    y_p = _out_ffn(x1_p, y5_p, ym_p, *ffn2, rows=n_p, tm=tile_p, seg_layout=seg_p)
    y_s = _out_ffn(x1_s, y5_s, ym_s, *ffn2, rows=n_s, tm=n_s, seg_layout=seg_s)

    def s5_state(s, n):
        return s.reshape(1, n, S5_GROUPS, S5_STATE)

    return (y_p.reshape(nb, seq, D_MODEL), y_s.reshape(ns, dseq, D_MODEL),
            s5_state(pre_s, nb), s5_state(pim_s, nb), c_p[None], nn_p[None], m_p[:, 0][None], cv_p[None],
            s5_state(sre_s, ns), s5_state(sim_s, ns), c_s[None], nn_s[None], m_s[:, 0][None], cv_s[None])
```

```python
import functools

import jax
import jax.numpy as jnp
from jax import lax
from jax.experimental import pallas as pl
from jax.experimental.pallas import tpu as pltpu

F32 = jnp.float32
BF16 = jnp.bfloat16

D_MODEL = 1024
D_FF = 2816
S5_GROUPS = 64
S5_GROUP = 16
S5_STATE = 64
S5_LANES = S5_GROUPS * S5_STATE
MXU_TILE = 256
S5_BLOCKS = 8
S5_BLOCK_GROUPS = S5_GROUPS // S5_BLOCKS
S5_BLOCK_IN = D_MODEL // S5_BLOCKS
S5_BLOCK_ST = S5_LANES // S5_BLOCKS
ML_HEADS = 4
ML_HEAD_DIM = 256
ML_CONV = 4
ML_QKV_BLOCK = 4
EPS = 1e-6
SUBLANES = 8
GATE_LANES = 128
POW_ROWS = 64
S5_T = 2
VMEM_LIMIT = 56 * 1024 * 1024

TOKEN_TILE = 512
ML_CHUNK = 256
ML_TILE = 256
ML_SAMPLE_GROUP = 4
SMALL_TILE_ALIGN = 16


def _rms(x, g):
    return x * lax.rsqrt(jnp.mean(x * x, axis=-1, keepdims=True) + EPS) * g


def _silu(x):
    h = 0.5 * x
    return h + h * jnp.tanh(h)


def _swiglu(h, wg_ref, wu_ref, wd_ref, acc_ref, ff_chunk):
    for c in range(D_FF // ff_chunk):
        sl = slice(c * ff_chunk, (c + 1) * ff_chunk)
        g = jnp.dot(h, wg_ref[:, sl], preferred_element_type=F32)
        u = jnp.dot(h, wu_ref[:, sl], preferred_element_type=F32)
        a = (_silu(g) * u).astype(BF16)
        d = jnp.dot(a, wd_ref[sl, :], preferred_element_type=F32)
        if c == 0:
            acc_ref[...] = d
        else:
            acc_ref[...] += d
    return acc_ref[...]


def _const_spec(shape):
    nd = len(shape)
    return pl.BlockSpec(shape, lambda *_: (0,) * nd, pipeline_mode=pl.Buffered(1))


def _segment_rows(x, layout, inverse=False):
    if not layout:
        return x
    parts = []
    for row0, r in layout:
        n = SUBLANES * r
        g = x[row0:row0 + n]
        if r % SUBLANES == 0:
            shape = (r, SUBLANES) if inverse else (SUBLANES, r)
            g = jnp.swapaxes(g.reshape(shape + g.shape[1:]), 0, 1).reshape(g.shape)
        elif inverse:
            g = jnp.concatenate([g[i * SUBLANES + a:i * SUBLANES + a + 1]
                                 for a in range(SUBLANES) for i in range(r)], axis=0)
        else:
            g = jnp.concatenate([g[a * r + i:a * r + i + 1]
                                 for i in range(r) for a in range(SUBLANES)], axis=0)
        parts.append(g)
    return parts[0] if len(parts) == 1 else jnp.concatenate(parts, axis=0)


def _ffn_in_tile(x, consts, acc_ref, ff_chunk, seg_layout):
    g1_ref, wg_ref, wu_ref, wd_ref, g2_ref, win_ref = consts
    h = _rms(x, g1_ref[...]).astype(BF16)
    x1 = x + 0.5 * _swiglu(h, wg_ref, wu_ref, wd_ref, acc_ref, ff_chunk)
    h2f = _rms(x1, g2_ref[...])
    h2 = h2f.astype(BF16)
    h2s = _segment_rows(h2f, seg_layout).astype(BF16)
    u = jnp.dot(h2s, win_ref[:, 0:D_MODEL], preferred_element_type=F32)
    xm = jnp.dot(h2, win_ref[:, D_MODEL:2 * D_MODEL], preferred_element_type=F32)
    z = jnp.dot(h2, win_ref[:, 2 * D_MODEL:3 * D_MODEL], preferred_element_type=F32)
    return x1, u, xm, z


def _ffn_in_kernel(xp_ref, xs_ref, *refs, ff_chunk, seg_layout, steps_p):
    consts, outs_p, outs_s, (acc_p_ref, acc_s_ref) = refs[:6], refs[6:10], refs[10:14], refs[14:]
    i = pl.program_id(0)

    @pl.when(i < steps_p)
    def _():
        for o_ref, v in zip(outs_p, _ffn_in_tile(xp_ref[...], consts, acc_p_ref, ff_chunk, seg_layout)):
            o_ref[...] = v

    @pl.when(i >= steps_p)
    def _():
        for o_ref, v in zip(outs_s, _ffn_in_tile(xs_ref[...], consts, acc_s_ref, ff_chunk, ())):
            o_ref[...] = v


def _ffn_in(xp, xs, g1, wg, wu, wd, g2, win, *, tm, tm_s, seg_layout, ff_chunk=256):
    steps_p, steps_s = xp.shape[0] // tm, xs.shape[0] // tm_s
    spec_p = pl.BlockSpec((tm, D_MODEL), lambda i: (jnp.minimum(i, steps_p - 1), 0))
    spec_s = pl.BlockSpec((tm_s, D_MODEL), lambda i: (jnp.maximum(i - steps_p, 0), 0))
    consts = (g1, wg, wu, wd, g2, win)
    outs = pl.pallas_call(
        functools.partial(_ffn_in_kernel, ff_chunk=ff_chunk, seg_layout=seg_layout, steps_p=steps_p),
        grid=(steps_p + steps_s,),
        in_specs=[spec_p, spec_s] + [_const_spec(a.shape) for a in consts],
        out_specs=[spec_p] * 4 + [spec_s] * 4,
        out_shape=[jax.ShapeDtypeStruct(xp.shape, F32)] * 4 + [jax.ShapeDtypeStruct(xs.shape, F32)] * 4,
        scratch_shapes=[pltpu.VMEM((tm, D_MODEL), F32), pltpu.VMEM((tm_s, D_MODEL), F32)],
        compiler_params=pltpu.CompilerParams(dimension_semantics=("arbitrary",), vmem_limit_bytes=VMEM_LIMIT),
        name="ffn_in",
    )(xp, xs, *consts)
    return outs[:4], outs[4:]


def _out_ffn_tile(x1, y5, ym, consts, acc_ref, ff_chunk, seg_layout):
    wo_ref, g_ref, wg_ref, wu_ref, wd_ref, gf_ref = consts
    p5 = jnp.dot(y5, wo_ref[0:D_MODEL, :], preferred_element_type=F32)
    x2 = (x1 + _segment_rows(p5, seg_layout, inverse=True)
          + jnp.dot(ym, wo_ref[D_MODEL:2 * D_MODEL, :], preferred_element_type=F32))
    h = _rms(x2, g_ref[...]).astype(BF16)
    x3 = x2 + 0.5 * _swiglu(h, wg_ref, wu_ref, wd_ref, acc_ref, ff_chunk)
    return _rms(x3, gf_ref[...])


def _out_ffn_kernel(x1p_ref, y5p_ref, ymp_ref, x1s_ref, y5s_ref, yms_ref, *refs, ff_chunk, seg_layout, steps_p):
    consts, (op_ref, os_ref, acc_p_ref, acc_s_ref) = refs[:6], refs[6:]
    i = pl.program_id(0)

    @pl.when(i < steps_p)
    def _():
        op_ref[...] = _out_ffn_tile(x1p_ref[...], y5p_ref[...], ymp_ref[...], consts, acc_p_ref, ff_chunk, seg_layout)

    @pl.when(i >= steps_p)
    def _():
        os_ref[...] = _out_ffn_tile(x1s_ref[...], y5s_ref[...], yms_ref[...], consts, acc_s_ref, ff_chunk, ())


def _out_ffn(x1p, y5p, ymp, x1s, y5s, yms, wo, g, wg, wu, wd, gf, *, tm, tm_s, seg_layout, ff_chunk=256):
    steps_p, steps_s = x1p.shape[0] // tm, y5s.shape[0] // tm_s
    spec_p = pl.BlockSpec((tm, D_MODEL), lambda i: (jnp.minimum(i, steps_p - 1), 0))
    spec_s = pl.BlockSpec((tm_s, D_MODEL), lambda i: (jnp.maximum(i - steps_p, 0), 0))
    consts = (wo, g, wg, wu, wd, gf)
    return pl.pallas_call(
        functools.partial(_out_ffn_kernel, ff_chunk=ff_chunk, seg_layout=seg_layout, steps_p=steps_p),
        grid=(steps_p + steps_s,),
        in_specs=[spec_p] * 3 + [spec_s] * 3 + [_const_spec(a.shape) for a in consts],
        out_specs=[spec_p, spec_s],
        out_shape=[jax.ShapeDtypeStruct(x1p.shape, F32), jax.ShapeDtypeStruct(y5s.shape, F32)],
        scratch_shapes=[pltpu.VMEM((tm, D_MODEL), F32), pltpu.VMEM((tm_s, D_MODEL), F32)],
        compiler_params=pltpu.CompilerParams(dimension_semantics=("arbitrary",), vmem_limit_bytes=VMEM_LIMIT),
        name="out_ffn",
    )(x1p, y5p, ymp, x1s, y5s, yms, *consts)


def _lam_bar(lr, li, ldt):
    dt = jnp.exp(ldt)
    mag = jnp.exp(lr * dt)
    th = li * dt
    return mag * jnp.cos(th), mag * jnp.sin(th), dt


def _cmul(ar, ai, br, bi):
    return ar * br - ai * bi, ar * bi + ai * br


def _dot_split(a, b):
    a_hi = a.astype(BF16)
    b_hi = b.astype(BF16)
    a_lo = (a - a_hi.astype(F32)).astype(BF16)
    b_lo = (b - b_hi.astype(F32)).astype(BF16)
    return (jnp.dot(a_hi, b_hi, preferred_element_type=F32) + jnp.dot(a_hi, b_lo, preferred_element_type=F32)
            + jnp.dot(a_lo, b_hi, preferred_element_type=F32))


def _s5_prep_kernel(lre_ref, lim_ref, ldt_ref, bre_ref, bim_ref, lre_t_ref, lim_t_ref, ldt_t_ref, cre_ref, cim_ref,
                    pre_ref, pim_ref, bcat_ref, wout_ref):
    lr = lre_ref[...]
    li = lim_ref[...]
    ar, ai, _ = _lam_bar(lr, li, ldt_ref[...])
    nr = ar - 1.0
    den = lr * lr + li * li
    cr = (nr * lr + ai * li) / den
    ci = (ai * lr - nr * li) / den
    bs = [_cmul(cr, ci, bre_ref[...], bim_ref[...])]
    for _ in range(1, S5_T):
        bs.append(_cmul(ar, ai, *bs[-1]))

    pr, pi = ar, ai
    pre_ref[0:1, :] = pr
    pim_ref[0:1, :] = pi
    for j in range(1, POW_ROWS):
        pr, pi = _cmul(pr, pi, ar, ai)
        pre_ref[j:j + 1, :] = pr
        pim_ref[j:j + 1, :] = pi

    tr, ti, _ = _lam_bar(lre_t_ref[...], lim_t_ref[...], ldt_t_ref[...])
    cs = [(cre_ref[...], cim_ref[...])]
    for _ in range(S5_T):
        cs.append(_cmul(tr, ti, *cs[-1]))

    g_shift, s_shift = S5_GROUP.bit_length() - 1, S5_STATE.bit_length() - 1
    b_shape = (S5_BLOCK_IN, S5_BLOCK_ST)
    b_mask = (lax.broadcasted_iota(jnp.int32, b_shape, 0) >> g_shift) == (
        lax.broadcasted_iota(jnp.int32, b_shape, 1) >> s_shift)
    c_shape = (S5_BLOCK_ST, S5_BLOCK_IN)
    c_mask = (lax.broadcasted_iota(jnp.int32, c_shape, 0) >> s_shift) == (
        lax.broadcasted_iota(jnp.int32, c_shape, 1) >> g_shift)

    def b_block(br, bi, blk):
        parts = []
        for x in (br, bi):
            x = jnp.concatenate([x[:, blk * S5_BLOCK_ST:(blk + 1) * S5_BLOCK_ST]] * S5_BLOCK_GROUPS, axis=0)
            parts.append(jnp.where(b_mask, x, 0.0))
        return jnp.concatenate(parts, axis=1)

    def c_block(xr, xi, blk):
        parts = []
        for x in (xr, -xi):
            x = jnp.concatenate([x[:, blk * S5_BLOCK_IN:(blk + 1) * S5_BLOCK_IN]] * S5_BLOCK_GROUPS, axis=0)
            parts.append(jnp.where(c_mask, x, 0.0))
        return jnp.concatenate(parts, axis=0)

    zero = jnp.zeros((S5_BLOCK_IN, S5_BLOCK_IN), F32)
    z_rows = 2 * S5_BLOCK_ST
    for blk in range(S5_BLOCKS):
        bms = [b_block(br, bi, blk) for br, bi in bs]
        cms = [c_block(xr, xi, blk) for xr, xi in cs]
        kcat = _dot_split(bms[0], jnp.concatenate(cms[:S5_T], axis=1))
        ks = [kcat[:, j * S5_BLOCK_IN:(j + 1) * S5_BLOCK_IN] for j in range(S5_T)]
        for f in range(S5_T):
            bcat_ref[blk, f * S5_BLOCK_IN:(f + 1) * S5_BLOCK_IN, :] = bms[S5_T - 1 - f].astype(BF16)
            row0 = z_rows + f * S5_BLOCK_IN
            wout_ref[blk, row0:row0 + S5_BLOCK_IN, :] = jnp.concatenate(
                [ks[g - f] if g >= f else zero for g in range(S5_T)], axis=1).astype(BF16)
        wout_ref[blk, 0:z_rows, :] = jnp.concatenate(cms[1:], axis=1).astype(BF16)


def _s5_prep(lre, lim, ldt, bre_t, bim_t, lre_t, lim_t, ldt_t, cre_t, cim_t):
    pow_shape = jax.ShapeDtypeStruct((POW_ROWS, S5_LANES), F32)
    return pl.pallas_call(
        _s5_prep_kernel,
        out_shape=[pow_shape, pow_shape,
                   jax.ShapeDtypeStruct((S5_BLOCKS, S5_T * S5_BLOCK_IN, 2 * S5_BLOCK_ST), BF16),
                   jax.ShapeDtypeStruct((S5_BLOCKS, 2 * S5_BLOCK_ST + S5_T * S5_BLOCK_IN, S5_T * S5_BLOCK_IN), BF16)],
        compiler_params=pltpu.CompilerParams(vmem_limit_bytes=VMEM_LIMIT),
        name="s5_prep",
    )(lre, lim, ldt, bre_t, bim_t, lre_t, lim_t, ldt_t, cre_t, cim_t)


def _ml_prep_kernel(wq_ref, wk_ref, wv_ref, oq_ref, ok_ref, ov_ref):
    shape = (MXU_TILE, MXU_TILE)
    shift = ML_QKV_BLOCK.bit_length() - 1
    row = lax.broadcasted_iota(jnp.int32, shape, 0)
    col = lax.broadcasted_iota(jnp.int32, shape, 1)
    same = (row >> shift) == (col >> shift)
    sel = col & (ML_QKV_BLOCK - 1)
    for w_ref, o_ref in ((wq_ref, oq_ref), (wk_ref, ok_ref), (wv_ref, ov_ref)):
        for c in range(D_MODEL // MXU_TILE):
            w = w_ref[c * MXU_TILE:(c + 1) * MXU_TILE, :]
            acc = jnp.zeros(shape, F32)
            for o in range(ML_QKV_BLOCK):
                acc = jnp.where(sel == o, w[:, o:o + 1], acc)
            o_ref[c] = jnp.where(same, acc, 0.0).astype(BF16)


def _ml_prep(wq, wk, wv):
    out = jax.ShapeDtypeStruct((D_MODEL // MXU_TILE, MXU_TILE, MXU_TILE), BF16)
    return pl.pallas_call(_ml_prep_kernel, out_shape=[out] * 3, name="ml_prep")(wq, wk, wv)


def _block_diag_dot(x, w_ref):
    return jnp.concatenate(
        [jnp.dot(x[:, c * MXU_TILE:(c + 1) * MXU_TILE], w_ref[c], preferred_element_type=F32)
         for c in range(w_ref.shape[0])], axis=1)


def _s5_kernel(u_ref, sre0_ref, sim0_ref, pre_ref, pim_ref, bcat_ref, wout_ref, d_ref, gw_ref, gb_ref, on_ref,
               y_ref, sre_ref, sim_ref, s_ref, *, tm, independent, n_tiles, frame_order_io):
    r = tm // SUBLANES
    io_layout = ((0, r),) if frame_order_io else ()
    nc = r // S5_T
    rows_c = nc * SUBLANES
    t = pl.program_id(1)

    @pl.when(t == 0)
    def _():
        sre_ref[...] = jnp.broadcast_to(sre0_ref[...], sre_ref.shape)
        sim_ref[...] = jnp.broadcast_to(sim0_ref[...], sim_ref.shape)

    ups = [_segment_rows(u_ref[g], io_layout) for g in range(n_tiles)]
    us = []
    for f in range(S5_T):
        us.append(jnp.concatenate(
            [up.reshape(nc, S5_T * SUBLANES, D_MODEL)[:, f * SUBLANES:(f + 1) * SUBLANES, :].reshape(rows_c, D_MODEL)
             for up in ups], axis=0).astype(BF16))

    ys = [[] for _ in range(S5_T)]
    for b in range(S5_BLOCKS):
        lanes = slice(b * S5_BLOCK_ST, (b + 1) * S5_BLOCK_ST)
        chans = slice(b * S5_BLOCK_IN, (b + 1) * S5_BLOCK_IN)
        re = slice(0, S5_BLOCK_ST)
        im = slice(S5_BLOCK_ST, 2 * S5_BLOCK_ST)
        ucat = jnp.concatenate([u[:, chans] for u in us], axis=1)
        s_ref[...] = jnp.dot(ucat, bcat_ref[b], preferred_element_type=F32)
        ltr = jnp.broadcast_to(pre_ref[S5_T - 1:S5_T, lanes], (SUBLANES, S5_BLOCK_ST))
        lti = jnp.broadcast_to(pim_ref[S5_T - 1:S5_T, lanes], (SUBLANES, S5_BLOCK_ST))

        def scan(cr, ci, row0):
            for c in range(nc):
                rows = slice(row0 + c * SUBLANES, row0 + (c + 1) * SUBLANES)
                nr = ltr * cr - lti * ci + s_ref[rows, re]
                ni = ltr * ci + lti * cr + s_ref[rows, im]
                s_ref[rows, re] = cr
                s_ref[rows, im] = ci
                cr, ci = nr, ni
            return cr, ci

        for g in range(n_tiles):
            row0 = g * rows_c
            if independent:
                fr, fi = scan(sre_ref[g, :, lanes], sim_ref[g, :, lanes], row0)
                sre_ref[g, :, lanes] = fr
                sim_ref[g, :, lanes] = fi
            else:
                zero = jnp.zeros((SUBLANES, S5_BLOCK_ST), F32)
                fr, fi = scan(zero, zero, row0)
                rr = pre_ref[r - 1:r, lanes]
                ri = pim_ref[r - 1:r, lanes]
                cr = sre_ref[g, :, lanes]
                ci = sim_ref[g, :, lanes]
                rows_r, rows_i = [], []
                for a in range(SUBLANES):
                    rows_r.append(cr)
                    rows_i.append(ci)
                    cr, ci = rr * cr - ri * ci + fr[a:a + 1], rr * ci + ri * cr + fi[a:a + 1]
                sre_ref[g, :, lanes] = cr
                sim_ref[g, :, lanes] = ci
                cin_r = jnp.concatenate(rows_r, axis=0)
                cin_i = jnp.concatenate(rows_i, axis=0)
                s_ref[row0:row0 + SUBLANES, re] = cin_r
                s_ref[row0:row0 + SUBLANES, im] = cin_i
                for c in range(1, nc):
                    rows = slice(row0 + c * SUBLANES, row0 + (c + 1) * SUBLANES)
                    pr = pre_ref[S5_T * c - 1:S5_T * c, lanes]
                    pi = pim_ref[S5_T * c - 1:S5_T * c, lanes]
                    s_ref[rows, re] += pr * cin_r - pi * cin_i
                    s_ref[rows, im] += pr * cin_i + pi * cin_r
        yb = jnp.dot(jnp.concatenate([s_ref[...].astype(BF16), ucat], axis=1), wout_ref[b],
                     preferred_element_type=F32)
        for f in range(S5_T):
            ys[f].append(yb[:, f * S5_BLOCK_IN:(f + 1) * S5_BLOCK_IN])

    yf = [jnp.concatenate(ys[f], axis=1) for f in range(S5_T)]
    for g in range(n_tiles):
        rows = slice(g * rows_c, (g + 1) * rows_c)
        y = jnp.concatenate([y[rows].reshape(nc, SUBLANES, D_MODEL) for y in yf], axis=1).reshape(tm, D_MODEL)
        y = y + d_ref[...] * ups[g]
        gl = jax.nn.gelu(y)
        o = gl * jax.nn.sigmoid(jnp.dot(gl.astype(BF16), gw_ref[...], preferred_element_type=F32) + gb_ref[...])
        y_ref[g] = _segment_rows(_rms(o, on_ref[...]), io_layout, inverse=True).astype(BF16)


def _s5(u, sre0, sim0, consts, *, n_outer, n_tiles, steps, tm, independent=False, shared_init=False,
        frame_order_io=False):
    srows = SUBLANES if independent else 1
    u5 = u.reshape(n_outer, n_tiles, steps, tm, D_MODEL)
    seq_spec = pl.BlockSpec((None, n_tiles, None, tm, D_MODEL), lambda o, t: (o, 0, t, 0, 0))
    if shared_init:
        st_in = pl.BlockSpec((None, 1, srows, S5_LANES), lambda o, t: (0, 0, 0, 0))
    else:
        st_in = pl.BlockSpec((None, n_tiles, srows, S5_LANES), lambda o, t: (o, 0, 0, 0))
    st_out = pl.BlockSpec((None, n_tiles, srows, S5_LANES), lambda o, t: (o, 0, 0, 0))
    st_shape = jax.ShapeDtypeStruct((n_outer, n_tiles, srows, S5_LANES), F32)
    y, sre, sim = pl.pallas_call(
        functools.partial(_s5_kernel, tm=tm, independent=independent, n_tiles=n_tiles,
                          frame_order_io=frame_order_io),
        grid=(n_outer, steps),
        in_specs=[seq_spec, st_in, st_in] + [_const_spec(a.shape) for a in consts],
        out_specs=[seq_spec, st_out, st_out],
        out_shape=[jax.ShapeDtypeStruct(u5.shape, BF16), st_shape, st_shape],
        scratch_shapes=[pltpu.VMEM((n_tiles * tm // S5_T, 2 * S5_BLOCK_ST), F32)],
        compiler_params=pltpu.CompilerParams(dimension_semantics=("parallel", "arbitrary"),
                                             vmem_limit_bytes=VMEM_LIMIT),
        name="s5_mix",
    )(u5, sre0, sim0, *consts)
    return y.reshape(u.shape), sre, sim


def _ml_front(xm_ref, cv_ref, xbuf_ref, consts, bufs, tm, n_tiles):
    cw_ref, cb_ref, bdq_ref, bdk_ref, bdv_ref, gw_ref, gb_ref = consts
    xc_ref, q_ref, k_ref, v_ref, vf_ref, g_ref = bufs
    pre = ML_CONV - 1
    cw = cw_ref[...]
    xms, xcs = [], []
    for g in range(n_tiles):
        xm = xm_ref[g]
        xbuf_ref[g, SUBLANES - pre:SUBLANES, :] = cv_ref[g]
        xbuf_ref[g, SUBLANES:SUBLANES + tm, :] = xm
        xc = cb_ref[...] + cw[pre:pre + 1] * xm
        for j in range(pre):
            xc = xc + cw[j:j + 1] * xbuf_ref[g, SUBLANES - pre + j:SUBLANES - pre + j + tm, :]
        cv_ref[g] = xbuf_ref[g, SUBLANES + tm - pre:SUBLANES + tm, :]
        xms.append(xm)
        xcs.append(_silu(xc))
    xm = xms[0] if n_tiles == 1 else jnp.concatenate(xms, axis=0)
    xc = xcs[0] if n_tiles == 1 else jnp.concatenate(xcs, axis=0)
    xc_ref[...] = xc

    xcb = xc.astype(BF16)
    q = _block_diag_dot(xcb, bdq_ref)
    k = _block_diag_dot(xcb, bdk_ref)
    v = _block_diag_dot(xm.astype(BF16), bdv_ref)
    qb = q.astype(BF16)
    vb = v.astype(BF16)
    g_ref[...] = jnp.dot(jnp.concatenate([qb, k.astype(BF16), vb], axis=1), gw_ref[...],
                         preferred_element_type=F32) + gb_ref[...]
    q_ref[...] = qb
    k_ref[...] = (k * (ML_HEAD_DIM ** -0.5)).astype(BF16)
    v_ref[...] = vb
    vf_ref[...] = v


def _ml_back(bufs, z_ref, state, consts, y_ref, h_ref, tm, tc, n_tiles):
    xc_ref, q_ref, k_ref, v_ref, vf_ref, g_ref = bufs
    c_ref, n_ref, m_ref = state
    nw_ref, sk_ref, on_ref = consts
    lane = lax.broadcasted_iota(jnp.int32, (tc, GATE_LANES), 1)
    row = lax.broadcasted_iota(jnp.int32, (tc, tc), 0)
    col = lax.broadcasted_iota(jnp.int32, (tc, tc), 1)
    causal = row >= col
    tril = causal.astype(F32)

    def chunk(rows, g):
        gates = g_ref[rows, :]
        lf = jnp.minimum(gates, 0.0) - jnp.log1p(jnp.exp(-jnp.abs(gates)))
        lf = jnp.where((lane >= ML_HEADS) & (lane < 2 * ML_HEADS), lf, 0.0)
        cum = jnp.dot(tril, lf, preferred_element_type=F32, precision=lax.Precision.HIGHEST)
        arr = jnp.where(lane < ML_HEADS, gates, cum)
        arr_t = arr.T
        for h in range(ML_HEADS):
            hs = slice(h * ML_HEAD_DIM, (h + 1) * ML_HEAD_DIM)
            qh = q_ref[rows, hs]
            kh = k_ref[rows, hs]
            vh = v_ref[rows, hs]
            ig_col = arr[:, h:h + 1]
            b_col = arr[:, ML_HEADS + h:ML_HEADS + h + 1]
            ig_row = arr_t[h:h + 1, :]
            b_row = arr_t[ML_HEADS + h:ML_HEADS + h + 1, :]
            m_prev = m_ref[g, :, h:h + 1]
            c_prev = c_ref[g, h]
            n_prev = n_ref[g, h:h + 1, :]

            logw = jnp.where(causal, b_col - b_row + ig_row, -jnp.inf)
            log_inter = b_col + m_prev
            m_t = jnp.maximum(log_inter, jnp.max(logw, axis=-1, keepdims=True))
            w = jnp.exp(logw - m_t)
            a_inter = jnp.exp(log_inter - m_t)
            s = lax.dot_general(qh, kh, (((1,), (1,)), ((), ())), preferred_element_type=F32) * w
            inter = lax.dot_general(qh, c_prev.astype(BF16), (((1,), (1,)), ((), ())),
                                    preferred_element_type=F32)
            num = a_inter * inter + jnp.dot(s.astype(BF16), vh, preferred_element_type=F32)
            den = (a_inter * jnp.sum(qh.astype(F32) * n_prev, axis=-1, keepdims=True)
                   + jnp.sum(s, axis=-1, keepdims=True))
            hh = num / jnp.maximum(jnp.abs(den), jnp.exp(-m_t))
            mu = jnp.mean(hh, axis=-1, keepdims=True)
            hc = hh - mu
            var = jnp.mean(hc * hc, axis=-1, keepdims=True)
            h_ref[rows, hs] = hc * lax.rsqrt(var + EPS)

            b_last = b_col[tc - 1:tc, :]
            m_new = m_t[tc - 1:tc, :]
            g_state = jnp.exp(b_last + m_prev - m_new)
            g_src = jnp.exp(b_last - b_col + ig_col - m_new)
            vs = (vf_ref[rows, hs] * g_src).astype(BF16)
            c_ref[g, h] = g_state * c_prev + lax.dot_general(vs, kh, (((0,), (0,)), ((), ())),
                                                             preferred_element_type=F32)
            n_ref[g, h:h + 1, :] = g_state * n_prev + jnp.sum(g_src * kh.astype(F32), axis=0, keepdims=True)
            m_ref[g, :, h:h + 1] = m_new

    for j in range(tm // tc):
        for g in range(n_tiles):
            chunk(slice(g * tm + j * tc, g * tm + (j + 1) * tc), g)

    for g in range(n_tiles):
        rows = slice(g * tm, (g + 1) * tm)
        out = (h_ref[rows, :] * nw_ref[...] + sk_ref[...] * xc_ref[rows, :]) * _silu(z_ref[g])
        y_ref[g] = _rms(out, on_ref[...]).astype(BF16)


def _mlstm_kernel(xm_ref, z_ref, c0_ref, n0_ref, m0_ref, cv0_ref, cw_ref, cb_ref, bdq_ref, bdk_ref, bdv_ref,
                  gw_ref, gb_ref, nw_ref, sk_ref, on_ref,
                  y_ref, c_ref, n_ref, m_ref, cv_ref, xbuf_ref, h_ref, *bufs, tm, tc, n_tiles, pipelined):
    t = pl.program_id(1)
    front_consts = (cw_ref, cb_ref, bdq_ref, bdk_ref, bdv_ref, gw_ref, gb_ref)
    back_consts = (nw_ref, sk_ref, on_ref)
    state = (c_ref, n_ref, m_ref)
    n_buf = len(bufs) // 2 if pipelined else len(bufs)

    @pl.when(t == 0)
    def _():
        cv_ref[...] = jnp.broadcast_to(cv0_ref[...], cv_ref.shape)
        if pipelined:
            for ref in bufs[n_buf:]:
                ref[...] = jnp.zeros(ref.shape, ref.dtype)

    @pl.when(t <= (1 if pipelined else 0))
    def _():
        c_ref[...] = jnp.broadcast_to(c0_ref[...], c_ref.shape)
        n_ref[...] = jnp.broadcast_to(n0_ref[...], n_ref.shape)
        m_ref[...] = jnp.broadcast_to(m0_ref[...], m_ref.shape)

    def step(front_set, back_set):
        _ml_front(xm_ref, cv_ref, xbuf_ref, front_consts, front_set, tm, n_tiles)
        _ml_back(back_set, z_ref, state, back_consts, y_ref, h_ref, tm, tc, n_tiles)

    if pipelined:
        pl.when(t % 2 == 0)(lambda: step(bufs[:n_buf], bufs[n_buf:]))
        pl.when(t % 2 == 1)(lambda: step(bufs[n_buf:], bufs[:n_buf]))
    else:
        step(bufs, bufs)


def _mlstm(xm, z, c0, n0, m0, cv0, consts, *, n_outer, n_tiles, steps, tm, tc, shared_init=False, pipelined=False):
    shape5 = (n_outer, n_tiles, steps, tm, D_MODEL)
    block = (None, n_tiles, None, tm, D_MODEL)
    if pipelined:
        grid_steps = steps + 1
        front_spec = pl.BlockSpec(block, lambda o, t: (o, 0, jnp.minimum(t, steps - 1), 0, 0))
        back_spec = pl.BlockSpec(block, lambda o, t: (o, 0, jnp.maximum(t - 1, 0), 0, 0))
    else:
        grid_steps = steps
        front_spec = back_spec = pl.BlockSpec(block, lambda o, t: (o, 0, t, 0, 0))

    def st(shape, shared):
        zeros = (0,) * len(shape)
        if shared:
            return pl.BlockSpec((None, 1) + shape, lambda o, t: (0, 0) + zeros)
        return pl.BlockSpec((None, n_tiles) + shape, lambda o, t: (o, 0) + zeros)

    st_shapes = ((ML_HEADS, ML_HEAD_DIM, ML_HEAD_DIM), (ML_HEADS, ML_HEAD_DIM), (1, ML_HEADS),
                 (ML_CONV - 1, D_MODEL))
    rows = n_tiles * tm
    buf_set = [pltpu.VMEM((rows, D_MODEL), F32), pltpu.VMEM((rows, D_MODEL), BF16), pltpu.VMEM((rows, D_MODEL), BF16),
               pltpu.VMEM((rows, D_MODEL), BF16), pltpu.VMEM((rows, D_MODEL), F32), pltpu.VMEM((rows, GATE_LANES), F32)]
    outs = pl.pallas_call(
        functools.partial(_mlstm_kernel, tm=tm, tc=tc, n_tiles=n_tiles, pipelined=pipelined),
        grid=(n_outer, grid_steps),
        in_specs=[front_spec, back_spec] + [st(s, shared_init) for s in st_shapes]
                 + [_const_spec(a.shape) for a in consts],
        out_specs=[back_spec] + [st(s, False) for s in st_shapes],
        out_shape=[jax.ShapeDtypeStruct(shape5, BF16)]
                  + [jax.ShapeDtypeStruct((n_outer, n_tiles) + s, F32) for s in st_shapes],
        scratch_shapes=[pltpu.VMEM((n_tiles, tm + 2 * SUBLANES, D_MODEL), F32), pltpu.VMEM((rows, D_MODEL), F32)]
                       + buf_set * (2 if pipelined else 1),
        compiler_params=pltpu.CompilerParams(dimension_semantics=("parallel", "arbitrary"),
                                             vmem_limit_bytes=VMEM_LIMIT),
        name="mlstm_mix",
    )(xm.reshape(shape5), z.reshape(shape5), c0, n0, m0, cv0, *consts)
    n_streams = n_outer * n_tiles
    return (outs[0].reshape(xm.shape),) + tuple(o.reshape((n_streams,) + o.shape[2:]) for o in outs[1:])


def _row(v):
    return v.reshape(1, -1).astype(F32)


def _pad_lanes(v, width):
    return jnp.pad(v, [(0, 0)] * (v.ndim - 1) + [(0, width - v.shape[-1])])


def kernel(x_prompt, x_sample, state_s5_re, state_s5_im, state_mlstm_c, state_mlstm_n, state_mlstm_m,
           state_mlstm_conv, meta_tokens, norm_ffn1, ffn1_gate, ffn1_up, ffn1_down, norm_mix, w_in,
           s5_lambda_re, s5_lambda_im, s5_log_dt, s5_b_re, s5_b_im, s5_c_re, s5_c_im, s5_d, s5_glu_w, s5_glu_b,
           ml_conv_w, ml_conv_b, ml_wq, ml_wk, ml_wv, ml_igate_w, ml_igate_b, ml_fgate_w, ml_fgate_b,
           ml_norm_w, ml_skip, out_norm_s5, out_norm_ml, w_out, norm_ffn2, ffn2_gate, ffn2_up, ffn2_down,
           norm_final):
    nb, seq, _ = x_prompt.shape
    ns, dseq, _ = x_sample.shape
    n_meta = meta_tokens.shape[0]
    n_p, n_s = nb * seq, ns * dseq
    tile_p = min(TOKEN_TILE, seq)

    wg1, wu1, wd1 = ffn1_gate[0].astype(BF16), ffn1_up[0].astype(BF16), ffn1_down[0].astype(BF16)
    wg2, wu2, wd2 = ffn2_gate[0].astype(BF16), ffn2_up[0].astype(BF16), ffn2_down[0].astype(BF16)
    win = w_in[0].astype(BF16)
    wo = w_out[0].astype(BF16)
    glu_w = s5_glu_w[0].astype(BF16)

    ldt_gs = jnp.broadcast_to(s5_log_dt[0][:, None], (S5_GROUPS, S5_STATE))

    def state_major(a):
        return jnp.repeat(a.T, S5_GROUP, axis=1)

    pre, pim, bcat, wout = _s5_prep(
        s5_lambda_re[0].reshape(1, S5_LANES), s5_lambda_im[0].reshape(1, S5_LANES), ldt_gs.reshape(1, S5_LANES),
        s5_b_re[0].reshape(S5_LANES, S5_GROUP).T, s5_b_im[0].reshape(S5_LANES, S5_GROUP).T,
        state_major(s5_lambda_re[0]), state_major(s5_lambda_im[0]), state_major(ldt_gs),
        s5_c_re[0].reshape(D_MODEL, S5_STATE).T, s5_c_im[0].reshape(D_MODEL, S5_STATE).T)
    s5_consts = (pre, pim, bcat, wout, _row(s5_d[0]), glu_w, _row(s5_glu_b[0]), _row(out_norm_s5[0]))

    bdq, bdk, bdv = _ml_prep(ml_wq[0].reshape(D_MODEL, ML_QKV_BLOCK), ml_wk[0].reshape(D_MODEL, ML_QKV_BLOCK),
                             ml_wv[0].reshape(D_MODEL, ML_QKV_BLOCK))
    gate_w = _pad_lanes(jnp.concatenate([ml_igate_w[0], ml_fgate_w[0]], axis=1), GATE_LANES).astype(BF16)
    gate_b = _pad_lanes(jnp.concatenate([ml_igate_b[0], ml_fgate_b[0]])[None, :], GATE_LANES)
    ml_consts = (ml_conv_w[0], _row(ml_conv_b[0]), bdq, bdk, bdv, gate_w, gate_b, _row(ml_norm_w[0]),
                 _row(ml_skip[0]), _row(out_norm_ml[0]))

    ffn1 = (_row(norm_ffn1[0]), wg1, wu1, wd1, _row(norm_mix[0]), win)
    seg_p = ((0, tile_p // SUBLANES),)
    pad_m = max(SUBLANES * S5_T - n_meta, 0)
    tm_m = n_meta + pad_m
    n_small = -(-(n_s + tm_m) // (2 * SMALL_TILE_ALIGN)) * (2 * SMALL_TILE_ALIGN)
    small = jnp.concatenate([x_sample.reshape(n_s, D_MODEL), jnp.zeros((pad_m, D_MODEL), F32), meta_tokens,
                             jnp.zeros((n_small - n_s - tm_m, D_MODEL), F32)], axis=0)
    (x1_p, u_p, xm_p, z_p), (x1_s, u_s, xm_s, z_s) = _ffn_in(
        x_prompt.reshape(n_p, D_MODEL), small, *ffn1, tm=tile_p, tm_s=n_small // 2, seg_layout=seg_p)

    zs5 = jnp.zeros((1, 1, 1, S5_LANES), F32)
    _, mre, mim = _s5(u_s[n_s:n_s + tm_m], zs5, zs5, s5_consts, n_outer=1, n_tiles=1, steps=1, tm=tm_m,
                      frame_order_io=True)
    y5_p, pre_s, pim_s = _s5(u_p, mre, mim, s5_consts, n_outer=1, n_tiles=nb, steps=seq // tile_p, tm=tile_p,
                             shared_init=True)
    n_grp = ns // SUBLANES
    y5_s, sre_s, sim_s = _s5(u_s[:n_s], state_s5_re[0].reshape(1, n_grp, SUBLANES, S5_LANES),
                             state_s5_im[0].reshape(1, n_grp, SUBLANES, S5_LANES), s5_consts,
                             n_outer=1, n_tiles=n_grp, steps=1, tm=SUBLANES * dseq, independent=True,
                             frame_order_io=True)

    ml_tile_p = min(ML_TILE, seq)
    zc = jnp.zeros((1, 1, ML_HEADS, ML_HEAD_DIM, ML_HEAD_DIM), F32)
    zn = jnp.zeros((1, 1, ML_HEADS, ML_HEAD_DIM), F32)
    zm = jnp.zeros((1, 1, 1, ML_HEADS), F32)
    zcv = jnp.zeros((1, 1, ML_CONV - 1, D_MODEL), F32)
    m_rows = slice(n_s + pad_m, n_s + pad_m + n_meta)
    _, c_m, n_m, m_m, cv_m = _mlstm(xm_s[m_rows], z_s[m_rows], zc, zn, zm, zcv, ml_consts, n_outer=1, n_tiles=1,
                                    steps=1, tm=n_meta, tc=n_meta)
    ym_p, c_p, nn_p, m_p, cv_p = _mlstm(xm_p, z_p, c_m[None], n_m[None], m_m[None], cv_m[None], ml_consts,
                                        n_outer=1, n_tiles=nb, steps=seq // ml_tile_p, tm=ml_tile_p,
                                        tc=min(ML_CHUNK, ml_tile_p), shared_init=True, pipelined=True)
    grp = ML_SAMPLE_GROUP
    st5 = lambda a: a.reshape((ns // grp, grp) + a.shape[1:])
    ym_s, c_s, nn_s, m_s, cv_s = _mlstm(xm_s[:n_s], z_s[:n_s], st5(state_mlstm_c[0]), st5(state_mlstm_n[0]),
                                        st5(state_mlstm_m[0][:, None, :]), st5(state_mlstm_conv[0]), ml_consts,
                                        n_outer=ns // grp, n_tiles=grp, steps=1, tm=dseq, tc=dseq)

    ffn2 = (wo, _row(norm_ffn2[0]), wg2, wu2, wd2, _row(norm_final))
    y_p, y_s = _out_ffn(x1_p, y5_p, ym_p, x1_s, y5_s, ym_s, *ffn2, tm=tile_p, tm_s=n_s // 2, seg_layout=seg_p)

    def s5_state(s, n):
        return s.reshape(1, n, S5_GROUPS, S5_STATE)

    return (y_p.reshape(nb, seq, D_MODEL), y_s.reshape(ns, dseq, D_MODEL),
            s5_state(pre_s, nb), s5_state(pim_s, nb), c_p[None], nn_p[None], m_p[:, 0][None], cv_p[None],
            s5_state(sre_s, ns), s5_state(sim_s, ns), c_s[None], nn_s[None], m_s[:, 0][None], cv_s[None])
```

```python
import functools

import jax
import jax.numpy as jnp
from jax import lax
from jax.experimental import pallas as pl
from jax.experimental.pallas import tpu as pltpu

F32 = jnp.float32
BF16 = jnp.bfloat16

D_MODEL = 1024
D_FF = 2816
S5_GROUPS = 64
S5_GROUP = 16
S5_STATE = 64
S5_LANES = S5_GROUPS * S5_STATE
MXU_TILE = 256
S5_BLOCKS = 8
S5_BLOCK_GROUPS = S5_GROUPS // S5_BLOCKS
S5_BLOCK_IN = D_MODEL // S5_BLOCKS
S5_BLOCK_ST = S5_LANES // S5_BLOCKS
ML_HEADS = 4
ML_HEAD_DIM = 256
ML_CONV = 4
ML_QKV_BLOCK = 4
EPS = 1e-6
SUBLANES = 8
GATE_LANES = 128
POW_ROWS = 64
S5_T = 2
VMEM_LIMIT = 56 * 1024 * 1024

TOKEN_TILE = 512
OUT_TILE = 1024
ML_CHUNK = 256
ML_TILE = 256
ML_SAMPLE_GROUP = 4
SMALL_TILE_ALIGN = 16


def _rms(x, g):
    return x * lax.rsqrt(jnp.mean(x * x, axis=-1, keepdims=True) + EPS) * g


def _silu(x):
    h = 0.5 * x
    return h + h * jnp.tanh(h)


def _swiglu(h, wg_ref, wu_ref, wd_ref, acc_ref, ff_chunk):
    for c in range(D_FF // ff_chunk):
        sl = slice(c * ff_chunk, (c + 1) * ff_chunk)
        g = jnp.dot(h, wg_ref[:, sl], preferred_element_type=F32)
        u = jnp.dot(h, wu_ref[:, sl], preferred_element_type=F32)
        a = (_silu(g) * u).astype(BF16)
        d = jnp.dot(a, wd_ref[sl, :], preferred_element_type=F32)
        if c == 0:
            acc_ref[...] = d
        else:
            acc_ref[...] += d
    return acc_ref[...]


def _const_spec(shape):
    nd = len(shape)
    return pl.BlockSpec(shape, lambda *_: (0,) * nd, pipeline_mode=pl.Buffered(1))


def _segment_rows(x, layout, inverse=False):
    if not layout:
        return x
    parts = []
    for row0, r in layout:
        n = SUBLANES * r
        g = x[row0:row0 + n]
        if r % SUBLANES == 0:
            shape = (r, SUBLANES) if inverse else (SUBLANES, r)
            g = jnp.swapaxes(g.reshape(shape + g.shape[1:]), 0, 1).reshape(g.shape)
        elif inverse:
            g = jnp.concatenate([g[i * SUBLANES + a:i * SUBLANES + a + 1]
                                 for a in range(SUBLANES) for i in range(r)], axis=0)
        else:
            g = jnp.concatenate([g[a * r + i:a * r + i + 1]
                                 for i in range(r) for a in range(SUBLANES)], axis=0)
        parts.append(g)
    return parts[0] if len(parts) == 1 else jnp.concatenate(parts, axis=0)


def _ffn_in_tile(x, consts, acc_ref, ff_chunk, seg_layout):
    g1_ref, wg_ref, wu_ref, wd_ref, g2_ref, win_ref = consts
    h = _rms(x, g1_ref[...]).astype(BF16)
    x1 = x + 0.5 * _swiglu(h, wg_ref, wu_ref, wd_ref, acc_ref, ff_chunk)
    h2f = _rms(x1, g2_ref[...])
    h2 = h2f.astype(BF16)
    h2s = _segment_rows(h2f, seg_layout).astype(BF16)
    u = jnp.dot(h2s, win_ref[:, 0:D_MODEL], preferred_element_type=F32)
    xm = jnp.dot(h2, win_ref[:, D_MODEL:2 * D_MODEL], preferred_element_type=F32)
    z = jnp.dot(h2, win_ref[:, 2 * D_MODEL:3 * D_MODEL], preferred_element_type=F32)
    return x1, u, xm, z


def _ffn_in_kernel(x_ref, *refs, ff_chunk, seg_layout):
    consts, outs, acc_ref = refs[:6], refs[6:10], refs[10]
    for o_ref, v in zip(outs, _ffn_in_tile(x_ref[...], consts, acc_ref, ff_chunk, seg_layout)):
        o_ref[...] = v


def _ffn_in(x, g1, wg, wu, wd, g2, win, *, tm, seg_layout, ff_chunk=256):
    rows = x.shape[0]
    row_spec = pl.BlockSpec((tm, D_MODEL), lambda i: (i, 0))
    consts = (g1, wg, wu, wd, g2, win)
    return pl.pallas_call(
        functools.partial(_ffn_in_kernel, ff_chunk=ff_chunk, seg_layout=seg_layout),
        grid=(rows // tm,),
        in_specs=[row_spec] + [_const_spec(a.shape) for a in consts],
        out_specs=[row_spec] * 4,
        out_shape=[jax.ShapeDtypeStruct(x.shape, F32)] * 4,
        scratch_shapes=[pltpu.VMEM((tm, D_MODEL), F32)],
        compiler_params=pltpu.CompilerParams(dimension_semantics=("parallel",), vmem_limit_bytes=VMEM_LIMIT),
        name="ffn_in",
    )(x, *consts)


def _out_ffn_tile(x1, y5, ym, consts, acc_ref, ff_chunk, seg_layout):
    wo_ref, g_ref, wg_ref, wu_ref, wd_ref, gf_ref = consts
    p5 = jnp.dot(y5, wo_ref[0:D_MODEL, :], preferred_element_type=F32)
    x2 = (x1 + _segment_rows(p5, seg_layout, inverse=True)
          + jnp.dot(ym, wo_ref[D_MODEL:2 * D_MODEL, :], preferred_element_type=F32))
    h = _rms(x2, g_ref[...]).astype(BF16)
    x3 = x2 + 0.5 * _swiglu(h, wg_ref, wu_ref, wd_ref, acc_ref, ff_chunk)
    return _rms(x3, gf_ref[...])


def _out_ffn_kernel(x1_ref, y5_ref, ym_ref, *refs, ff_chunk, seg_layout):
    consts, (o_ref, acc_ref) = refs[:6], refs[6:]
    o_ref[...] = _out_ffn_tile(x1_ref[...], y5_ref[...], ym_ref[...], consts, acc_ref, ff_chunk, seg_layout)


def _out_ffn(x1, y5, ym, wo, g, wg, wu, wd, gf, *, tm, seg_layout, ff_chunk=256):
    rows = y5.shape[0]
    row_spec = pl.BlockSpec((tm, D_MODEL), lambda i: (i, 0))
    consts = (wo, g, wg, wu, wd, gf)
    return pl.pallas_call(
        functools.partial(_out_ffn_kernel, ff_chunk=ff_chunk, seg_layout=seg_layout),
        grid=(rows // tm,),
        in_specs=[row_spec] * 3 + [_const_spec(a.shape) for a in consts],
        out_specs=row_spec,
        out_shape=jax.ShapeDtypeStruct((rows, D_MODEL), F32),
        scratch_shapes=[pltpu.VMEM((tm, D_MODEL), F32)],
        compiler_params=pltpu.CompilerParams(dimension_semantics=("parallel",), vmem_limit_bytes=VMEM_LIMIT),
        name="out_ffn",
    )(x1, y5, ym, *consts)


def _lam_bar(lr, li, ldt):
    dt = jnp.exp(ldt)
    mag = jnp.exp(lr * dt)
    th = li * dt
    return mag * jnp.cos(th), mag * jnp.sin(th), dt


def _cmul(ar, ai, br, bi):
    return ar * br - ai * bi, ar * bi + ai * br


def _dot_split(a, b):
    a_hi = a.astype(BF16)
    b_hi = b.astype(BF16)
    a_lo = (a - a_hi.astype(F32)).astype(BF16)
    b_lo = (b - b_hi.astype(F32)).astype(BF16)
    return (jnp.dot(a_hi, b_hi, preferred_element_type=F32) + jnp.dot(a_hi, b_lo, preferred_element_type=F32)
            + jnp.dot(a_lo, b_hi, preferred_element_type=F32))


def _s5_prep_kernel(lre_ref, lim_ref, ldt_ref, bre_ref, bim_ref, lre_t_ref, lim_t_ref, ldt_t_ref, cre_ref, cim_ref,
                    pre_ref, pim_ref, bcat_ref, wout_ref):
    lr = lre_ref[...]
    li = lim_ref[...]
    ar, ai, _ = _lam_bar(lr, li, ldt_ref[...])
    nr = ar - 1.0
    den = lr * lr + li * li
    cr = (nr * lr + ai * li) / den
    ci = (ai * lr - nr * li) / den
    bs = [_cmul(cr, ci, bre_ref[...], bim_ref[...])]
    for _ in range(1, S5_T):
        bs.append(_cmul(ar, ai, *bs[-1]))

    pr, pi = ar, ai
    pre_ref[0:1, :] = pr
    pim_ref[0:1, :] = pi
    for j in range(1, POW_ROWS):
        pr, pi = _cmul(pr, pi, ar, ai)
        pre_ref[j:j + 1, :] = pr
        pim_ref[j:j + 1, :] = pi

    tr, ti, _ = _lam_bar(lre_t_ref[...], lim_t_ref[...], ldt_t_ref[...])
    cs = [(cre_ref[...], cim_ref[...])]
    for _ in range(S5_T):
        cs.append(_cmul(tr, ti, *cs[-1]))

    g_shift, s_shift = S5_GROUP.bit_length() - 1, S5_STATE.bit_length() - 1
    b_shape = (S5_BLOCK_IN, S5_BLOCK_ST)
    b_mask = (lax.broadcasted_iota(jnp.int32, b_shape, 0) >> g_shift) == (
        lax.broadcasted_iota(jnp.int32, b_shape, 1) >> s_shift)
    c_shape = (S5_BLOCK_ST, S5_BLOCK_IN)
    c_mask = (lax.broadcasted_iota(jnp.int32, c_shape, 0) >> s_shift) == (
        lax.broadcasted_iota(jnp.int32, c_shape, 1) >> g_shift)

    def b_block(br, bi, blk):
        parts = []
        for x in (br, bi):
            x = jnp.concatenate([x[:, blk * S5_BLOCK_ST:(blk + 1) * S5_BLOCK_ST]] * S5_BLOCK_GROUPS, axis=0)
            parts.append(jnp.where(b_mask, x, 0.0))
        return jnp.concatenate(parts, axis=1)

    def c_block(xr, xi, blk):
        parts = []
        for x in (xr, -xi):
            x = jnp.concatenate([x[:, blk * S5_BLOCK_IN:(blk + 1) * S5_BLOCK_IN]] * S5_BLOCK_GROUPS, axis=0)
            parts.append(jnp.where(c_mask, x, 0.0))
        return jnp.concatenate(parts, axis=0)

    zero = jnp.zeros((S5_BLOCK_IN, S5_BLOCK_IN), F32)
    z_rows = 2 * S5_BLOCK_ST
    for blk in range(S5_BLOCKS):
        bms = [b_block(br, bi, blk) for br, bi in bs]
        cms = [c_block(xr, xi, blk) for xr, xi in cs]
        kcat = _dot_split(bms[0], jnp.concatenate(cms[:S5_T], axis=1))
        ks = [kcat[:, j * S5_BLOCK_IN:(j + 1) * S5_BLOCK_IN] for j in range(S5_T)]
        for f in range(S5_T):
            bcat_ref[blk, f * S5_BLOCK_IN:(f + 1) * S5_BLOCK_IN, :] = bms[S5_T - 1 - f].astype(BF16)
            row0 = z_rows + f * S5_BLOCK_IN
            wout_ref[blk, row0:row0 + S5_BLOCK_IN, :] = jnp.concatenate(
                [ks[g - f] if g >= f else zero for g in range(S5_T)], axis=1).astype(BF16)
        wout_ref[blk, 0:z_rows, :] = jnp.concatenate(cms[1:], axis=1).astype(BF16)


def _s5_prep(lre, lim, ldt, bre_t, bim_t, lre_t, lim_t, ldt_t, cre_t, cim_t):
    pow_shape = jax.ShapeDtypeStruct((POW_ROWS, S5_LANES), F32)
    return pl.pallas_call(
        _s5_prep_kernel,
        out_shape=[pow_shape, pow_shape,
                   jax.ShapeDtypeStruct((S5_BLOCKS, S5_T * S5_BLOCK_IN, 2 * S5_BLOCK_ST), BF16),
                   jax.ShapeDtypeStruct((S5_BLOCKS, 2 * S5_BLOCK_ST + S5_T * S5_BLOCK_IN, S5_T * S5_BLOCK_IN), BF16)],
        compiler_params=pltpu.CompilerParams(vmem_limit_bytes=VMEM_LIMIT),
        name="s5_prep",
    )(lre, lim, ldt, bre_t, bim_t, lre_t, lim_t, ldt_t, cre_t, cim_t)


def _ml_prep_kernel(wq_ref, wk_ref, wv_ref, oq_ref, ok_ref, ov_ref):
    shape = (MXU_TILE, MXU_TILE)
    shift = ML_QKV_BLOCK.bit_length() - 1
    row = lax.broadcasted_iota(jnp.int32, shape, 0)
    col = lax.broadcasted_iota(jnp.int32, shape, 1)
    same = (row >> shift) == (col >> shift)
    sel = col & (ML_QKV_BLOCK - 1)
    for w_ref, o_ref in ((wq_ref, oq_ref), (wk_ref, ok_ref), (wv_ref, ov_ref)):
        for c in range(D_MODEL // MXU_TILE):
            w = w_ref[c * MXU_TILE:(c + 1) * MXU_TILE, :]
            acc = jnp.zeros(shape, F32)
            for o in range(ML_QKV_BLOCK):
                acc = jnp.where(sel == o, w[:, o:o + 1], acc)
            o_ref[c] = jnp.where(same, acc, 0.0).astype(BF16)


def _ml_prep(wq, wk, wv):
    out = jax.ShapeDtypeStruct((D_MODEL // MXU_TILE, MXU_TILE, MXU_TILE), BF16)
    return pl.pallas_call(_ml_prep_kernel, out_shape=[out] * 3, name="ml_prep")(wq, wk, wv)


def _block_diag_dot(x, w_ref):
    return jnp.concatenate(
        [jnp.dot(x[:, c * MXU_TILE:(c + 1) * MXU_TILE], w_ref[c], preferred_element_type=F32)
         for c in range(w_ref.shape[0])], axis=1)


def _s5_kernel(u_ref, sre0_ref, sim0_ref, pre_ref, pim_ref, bcat_ref, wout_ref, d_ref, gw_ref, gb_ref, on_ref,
               y_ref, sre_ref, sim_ref, s_ref, *, tm, independent, n_tiles, frame_order_io):
    r = tm // SUBLANES
    io_layout = ((0, r),) if frame_order_io else ()
    nc = r // S5_T
    rows_c = nc * SUBLANES
    t = pl.program_id(1)

    @pl.when(t == 0)
    def _():
        sre_ref[...] = jnp.broadcast_to(sre0_ref[...], sre_ref.shape)
        sim_ref[...] = jnp.broadcast_to(sim0_ref[...], sim_ref.shape)

    ups = [_segment_rows(u_ref[g], io_layout) for g in range(n_tiles)]
    us = []
    for f in range(S5_T):
        us.append(jnp.concatenate(
            [up.reshape(nc, S5_T * SUBLANES, D_MODEL)[:, f * SUBLANES:(f + 1) * SUBLANES, :].reshape(rows_c, D_MODEL)
             for up in ups], axis=0).astype(BF16))

    ys = [[] for _ in range(S5_T)]
    for b in range(S5_BLOCKS):
        lanes = slice(b * S5_BLOCK_ST, (b + 1) * S5_BLOCK_ST)
        chans = slice(b * S5_BLOCK_IN, (b + 1) * S5_BLOCK_IN)
        re = slice(0, S5_BLOCK_ST)
        im = slice(S5_BLOCK_ST, 2 * S5_BLOCK_ST)
        ucat = jnp.concatenate([u[:, chans] for u in us], axis=1)
        s_ref[...] = jnp.dot(ucat, bcat_ref[b], preferred_element_type=F32)
        ltr = jnp.broadcast_to(pre_ref[S5_T - 1:S5_T, lanes], (SUBLANES, S5_BLOCK_ST))
        lti = jnp.broadcast_to(pim_ref[S5_T - 1:S5_T, lanes], (SUBLANES, S5_BLOCK_ST))

        def scan(cr, ci, row0):
            for c in range(nc):
                rows = slice(row0 + c * SUBLANES, row0 + (c + 1) * SUBLANES)
                nr = ltr * cr - lti * ci + s_ref[rows, re]
                ni = ltr * ci + lti * cr + s_ref[rows, im]
                s_ref[rows, re] = cr
                s_ref[rows, im] = ci
                cr, ci = nr, ni
            return cr, ci

        for g in range(n_tiles):
            row0 = g * rows_c
            if independent:
                fr, fi = scan(sre_ref[g, :, lanes], sim_ref[g, :, lanes], row0)
                sre_ref[g, :, lanes] = fr
                sim_ref[g, :, lanes] = fi
            else:
                zero = jnp.zeros((SUBLANES, S5_BLOCK_ST), F32)
                fr, fi = scan(zero, zero, row0)
                rr = pre_ref[r - 1:r, lanes]
                ri = pim_ref[r - 1:r, lanes]
                cr = sre_ref[g, :, lanes]
                ci = sim_ref[g, :, lanes]
                rows_r, rows_i = [], []
                for a in range(SUBLANES):
                    rows_r.append(cr)
                    rows_i.append(ci)
                    cr, ci = rr * cr - ri * ci + fr[a:a + 1], rr * ci + ri * cr + fi[a:a + 1]
                sre_ref[g, :, lanes] = cr
                sim_ref[g, :, lanes] = ci
                cin_r = jnp.concatenate(rows_r, axis=0)
                cin_i = jnp.concatenate(rows_i, axis=0)
                s_ref[row0:row0 + SUBLANES, re] = cin_r
                s_ref[row0:row0 + SUBLANES, im] = cin_i
                for c in range(1, nc):
                    rows = slice(row0 + c * SUBLANES, row0 + (c + 1) * SUBLANES)
                    pr = pre_ref[S5_T * c - 1:S5_T * c, lanes]
                    pi = pim_ref[S5_T * c - 1:S5_T * c, lanes]
                    s_ref[rows, re] += pr * cin_r - pi * cin_i
                    s_ref[rows, im] += pr * cin_i + pi * cin_r
        yb = jnp.dot(jnp.concatenate([s_ref[...].astype(BF16), ucat], axis=1), wout_ref[b],
                     preferred_element_type=F32)
        for f in range(S5_T):
            ys[f].append(yb[:, f * S5_BLOCK_IN:(f + 1) * S5_BLOCK_IN])

    yf = [jnp.concatenate(ys[f], axis=1) for f in range(S5_T)]
    for g in range(n_tiles):
        rows = slice(g * rows_c, (g + 1) * rows_c)
        y = jnp.concatenate([y[rows].reshape(nc, SUBLANES, D_MODEL) for y in yf], axis=1).reshape(tm, D_MODEL)
        y = y + d_ref[...] * ups[g]
        gl = jax.nn.gelu(y)
        o = gl * jax.nn.sigmoid(jnp.dot(gl.astype(BF16), gw_ref[...], preferred_element_type=F32) + gb_ref[...])
        y_ref[g] = _segment_rows(_rms(o, on_ref[...]), io_layout, inverse=True).astype(BF16)


def _s5(u, sre0, sim0, consts, *, n_outer, n_tiles, steps, tm, independent=False, shared_init=False,
        frame_order_io=False):
    srows = SUBLANES if independent else 1
    u5 = u.reshape(n_outer, n_tiles, steps, tm, D_MODEL)
    seq_spec = pl.BlockSpec((None, n_tiles, None, tm, D_MODEL), lambda o, t: (o, 0, t, 0, 0))
    if shared_init:
        st_in = pl.BlockSpec((None, 1, srows, S5_LANES), lambda o, t: (0, 0, 0, 0))
    else:
        st_in = pl.BlockSpec((None, n_tiles, srows, S5_LANES), lambda o, t: (o, 0, 0, 0))
    st_out = pl.BlockSpec((None, n_tiles, srows, S5_LANES), lambda o, t: (o, 0, 0, 0))
    st_shape = jax.ShapeDtypeStruct((n_outer, n_tiles, srows, S5_LANES), F32)
    y, sre, sim = pl.pallas_call(
        functools.partial(_s5_kernel, tm=tm, independent=independent, n_tiles=n_tiles,
                          frame_order_io=frame_order_io),
        grid=(n_outer, steps),
        in_specs=[seq_spec, st_in, st_in] + [_const_spec(a.shape) for a in consts],
        out_specs=[seq_spec, st_out, st_out],
        out_shape=[jax.ShapeDtypeStruct(u5.shape, BF16), st_shape, st_shape],
        scratch_shapes=[pltpu.VMEM((n_tiles * tm // S5_T, 2 * S5_BLOCK_ST), F32)],
        compiler_params=pltpu.CompilerParams(dimension_semantics=("parallel", "arbitrary"),
                                             vmem_limit_bytes=VMEM_LIMIT),
        name="s5_mix",
    )(u5, sre0, sim0, *consts)
    return y.reshape(u.shape), sre, sim


def _ml_front(xm_ref, cv_ref, xbuf_ref, consts, bufs, tm, n_tiles):
    cw_ref, cb_ref, bdq_ref, bdk_ref, bdv_ref, gw_ref, gb_ref = consts
    xc_ref, q_ref, k_ref, v_ref, vf_ref, g_ref = bufs
    pre = ML_CONV - 1
    cw = cw_ref[...]
    xms, xcs = [], []
    for g in range(n_tiles):
        xm = xm_ref[g]
        xbuf_ref[g, SUBLANES - pre:SUBLANES, :] = cv_ref[g]
        xbuf_ref[g, SUBLANES:SUBLANES + tm, :] = xm
        xc = cb_ref[...] + cw[pre:pre + 1] * xm
        for j in range(pre):
            xc = xc + cw[j:j + 1] * xbuf_ref[g, SUBLANES - pre + j:SUBLANES - pre + j + tm, :]
        cv_ref[g] = xbuf_ref[g, SUBLANES + tm - pre:SUBLANES + tm, :]
        xms.append(xm)
        xcs.append(_silu(xc))
    xm = xms[0] if n_tiles == 1 else jnp.concatenate(xms, axis=0)
    xc = xcs[0] if n_tiles == 1 else jnp.concatenate(xcs, axis=0)
    xc_ref[...] = xc

    xcb = xc.astype(BF16)
    q = _block_diag_dot(xcb, bdq_ref)
    k = _block_diag_dot(xcb, bdk_ref)
    v = _block_diag_dot(xm.astype(BF16), bdv_ref)
    qb = q.astype(BF16)
    vb = v.astype(BF16)
    g_ref[...] = jnp.dot(jnp.concatenate([qb, k.astype(BF16), vb], axis=1), gw_ref[...],
                         preferred_element_type=F32) + gb_ref[...]
    q_ref[...] = qb
    k_ref[...] = (k * (ML_HEAD_DIM ** -0.5)).astype(BF16)
    v_ref[...] = vb
    vf_ref[...] = v


def _ml_back(bufs, z_ref, state, consts, y_ref, h_ref, tm, tc, n_tiles):
    xc_ref, q_ref, k_ref, v_ref, vf_ref, g_ref = bufs
    c_ref, n_ref, m_ref = state
    nw_ref, sk_ref, on_ref = consts
    lane = lax.broadcasted_iota(jnp.int32, (tc, GATE_LANES), 1)
    row = lax.broadcasted_iota(jnp.int32, (tc, tc), 0)
    col = lax.broadcasted_iota(jnp.int32, (tc, tc), 1)
    causal = row >= col
    tril = causal.astype(F32)

    def chunk(rows, g):
        gates = g_ref[rows, :]
        lf = jnp.minimum(gates, 0.0) - jnp.log1p(jnp.exp(-jnp.abs(gates)))
        lf = jnp.where((lane >= ML_HEADS) & (lane < 2 * ML_HEADS), lf, 0.0)
        cum = jnp.dot(tril, lf, preferred_element_type=F32, precision=lax.Precision.HIGHEST)
        arr = jnp.where(lane < ML_HEADS, gates, cum)
        arr_t = arr.T
        for h in range(ML_HEADS):
            hs = slice(h * ML_HEAD_DIM, (h + 1) * ML_HEAD_DIM)
            qh = q_ref[rows, hs]
            kh = k_ref[rows, hs]
            vh = v_ref[rows, hs]
            ig_col = arr[:, h:h + 1]
            b_col = arr[:, ML_HEADS + h:ML_HEADS + h + 1]
            ig_row = arr_t[h:h + 1, :]
            b_row = arr_t[ML_HEADS + h:ML_HEADS + h + 1, :]
            m_prev = m_ref[g, :, h:h + 1]
            c_prev = c_ref[g, h]
            n_prev = n_ref[g, h:h + 1, :]

            logw = jnp.where(causal, b_col - b_row + ig_row, -jnp.inf)
            log_inter = b_col + m_prev
            m_t = jnp.maximum(log_inter, jnp.max(logw, axis=-1, keepdims=True))
            w = jnp.exp(logw - m_t)
            a_inter = jnp.exp(log_inter - m_t)
            s = lax.dot_general(qh, kh, (((1,), (1,)), ((), ())), preferred_element_type=F32) * w
            inter = lax.dot_general(qh, c_prev.astype(BF16), (((1,), (1,)), ((), ())),
                                    preferred_element_type=F32)
            num = a_inter * inter + jnp.dot(s.astype(BF16), vh, preferred_element_type=F32)
            den = (a_inter * jnp.sum(qh.astype(F32) * n_prev, axis=-1, keepdims=True)
                   + jnp.sum(s, axis=-1, keepdims=True))
            hh = num / jnp.maximum(jnp.abs(den), jnp.exp(-m_t))
            mu = jnp.mean(hh, axis=-1, keepdims=True)
            hc = hh - mu
            var = jnp.mean(hc * hc, axis=-1, keepdims=True)
            h_ref[rows, hs] = hc * lax.rsqrt(var + EPS)

            b_last = b_col[tc - 1:tc, :]
            m_new = m_t[tc - 1:tc, :]
            g_state = jnp.exp(b_last + m_prev - m_new)
            g_src = jnp.exp(b_last - b_col + ig_col - m_new)
            vs = (vf_ref[rows, hs] * g_src).astype(BF16)
            c_ref[g, h] = g_state * c_prev + lax.dot_general(vs, kh, (((0,), (0,)), ((), ())),
                                                             preferred_element_type=F32)
            n_ref[g, h:h + 1, :] = g_state * n_prev + jnp.sum(g_src * kh.astype(F32), axis=0, keepdims=True)
            m_ref[g, :, h:h + 1] = m_new

    for j in range(tm // tc):
        for g in range(n_tiles):
            chunk(slice(g * tm + j * tc, g * tm + (j + 1) * tc), g)

    for g in range(n_tiles):
        rows = slice(g * tm, (g + 1) * tm)
        out = (h_ref[rows, :] * nw_ref[...] + sk_ref[...] * xc_ref[rows, :]) * _silu(z_ref[g])
        y_ref[g] = _rms(out, on_ref[...]).astype(BF16)


def _mlstm_kernel(xm_ref, z_ref, c0_ref, n0_ref, m0_ref, cv0_ref, cw_ref, cb_ref, bdq_ref, bdk_ref, bdv_ref,
                  gw_ref, gb_ref, nw_ref, sk_ref, on_ref,
                  y_ref, c_ref, n_ref, m_ref, cv_ref, xbuf_ref, h_ref, *bufs, tm, tc, n_tiles, pipelined):
    t = pl.program_id(1)
    front_consts = (cw_ref, cb_ref, bdq_ref, bdk_ref, bdv_ref, gw_ref, gb_ref)
    back_consts = (nw_ref, sk_ref, on_ref)
    state = (c_ref, n_ref, m_ref)
    n_buf = len(bufs) // 2 if pipelined else len(bufs)

    @pl.when(t == 0)
    def _():
        cv_ref[...] = jnp.broadcast_to(cv0_ref[...], cv_ref.shape)
        if pipelined:
            for ref in bufs[n_buf:]:
                ref[...] = jnp.zeros(ref.shape, ref.dtype)

    @pl.when(t <= (1 if pipelined else 0))
    def _():
        c_ref[...] = jnp.broadcast_to(c0_ref[...], c_ref.shape)
        n_ref[...] = jnp.broadcast_to(n0_ref[...], n_ref.shape)
        m_ref[...] = jnp.broadcast_to(m0_ref[...], m_ref.shape)

    def step(front_set, back_set):
        _ml_front(xm_ref, cv_ref, xbuf_ref, front_consts, front_set, tm, n_tiles)
        _ml_back(back_set, z_ref, state, back_consts, y_ref, h_ref, tm, tc, n_tiles)

    if pipelined:
        pl.when(t % 2 == 0)(lambda: step(bufs[:n_buf], bufs[n_buf:]))
        pl.when(t % 2 == 1)(lambda: step(bufs[n_buf:], bufs[:n_buf]))
    else:
        step(bufs, bufs)


def _mlstm(xm, z, c0, n0, m0, cv0, consts, *, n_outer, n_tiles, steps, tm, tc, shared_init=False, pipelined=False):
    shape5 = (n_outer, n_tiles, steps, tm, D_MODEL)
    block = (None, n_tiles, None, tm, D_MODEL)
    if pipelined:
        grid_steps = steps + 1
        front_spec = pl.BlockSpec(block, lambda o, t: (o, 0, jnp.minimum(t, steps - 1), 0, 0))
        back_spec = pl.BlockSpec(block, lambda o, t: (o, 0, jnp.maximum(t - 1, 0), 0, 0))
    else:
        grid_steps = steps
        front_spec = back_spec = pl.BlockSpec(block, lambda o, t: (o, 0, t, 0, 0))

    def st(shape, shared):
        zeros = (0,) * len(shape)
        if shared:
            return pl.BlockSpec((None, 1) + shape, lambda o, t: (0, 0) + zeros)
        return pl.BlockSpec((None, n_tiles) + shape, lambda o, t: (o, 0) + zeros)

    st_shapes = ((ML_HEADS, ML_HEAD_DIM, ML_HEAD_DIM), (ML_HEADS, ML_HEAD_DIM), (1, ML_HEADS),
                 (ML_CONV - 1, D_MODEL))
    rows = n_tiles * tm
    buf_set = [pltpu.VMEM((rows, D_MODEL), F32), pltpu.VMEM((rows, D_MODEL), BF16), pltpu.VMEM((rows, D_MODEL), BF16),
               pltpu.VMEM((rows, D_MODEL), BF16), pltpu.VMEM((rows, D_MODEL), F32), pltpu.VMEM((rows, GATE_LANES), F32)]
    outs = pl.pallas_call(
        functools.partial(_mlstm_kernel, tm=tm, tc=tc, n_tiles=n_tiles, pipelined=pipelined),
        grid=(n_outer, grid_steps),
        in_specs=[front_spec, back_spec] + [st(s, shared_init) for s in st_shapes]
                 + [_const_spec(a.shape) for a in consts],
        out_specs=[back_spec] + [st(s, False) for s in st_shapes],
        out_shape=[jax.ShapeDtypeStruct(shape5, BF16)]
                  + [jax.ShapeDtypeStruct((n_outer, n_tiles) + s, F32) for s in st_shapes],
        scratch_shapes=[pltpu.VMEM((n_tiles, tm + 2 * SUBLANES, D_MODEL), F32), pltpu.VMEM((rows, D_MODEL), F32)]
                       + buf_set * (2 if pipelined else 1),
        compiler_params=pltpu.CompilerParams(dimension_semantics=("parallel", "arbitrary"),
                                             vmem_limit_bytes=VMEM_LIMIT),
        name="mlstm_mix",
    )(xm.reshape(shape5), z.reshape(shape5), c0, n0, m0, cv0, *consts)
    n_streams = n_outer * n_tiles
    return (outs[0].reshape(xm.shape),) + tuple(o.reshape((n_streams,) + o.shape[2:]) for o in outs[1:])


def _row(v):
    return v.reshape(1, -1).astype(F32)


def _pad_lanes(v, width):
    return jnp.pad(v, [(0, 0)] * (v.ndim - 1) + [(0, width - v.shape[-1])])


def kernel(x_prompt, x_sample, state_s5_re, state_s5_im, state_mlstm_c, state_mlstm_n, state_mlstm_m,
           state_mlstm_conv, meta_tokens, norm_ffn1, ffn1_gate, ffn1_up, ffn1_down, norm_mix, w_in,
           s5_lambda_re, s5_lambda_im, s5_log_dt, s5_b_re, s5_b_im, s5_c_re, s5_c_im, s5_d, s5_glu_w, s5_glu_b,
           ml_conv_w, ml_conv_b, ml_wq, ml_wk, ml_wv, ml_igate_w, ml_igate_b, ml_fgate_w, ml_fgate_b,
           ml_norm_w, ml_skip, out_norm_s5, out_norm_ml, w_out, norm_ffn2, ffn2_gate, ffn2_up, ffn2_down,
           norm_final):
    nb, seq, _ = x_prompt.shape
    ns, dseq, _ = x_sample.shape
    n_meta = meta_tokens.shape[0]
    n_p, n_s = nb * seq, ns * dseq
    tile_p = min(TOKEN_TILE, seq)

    wg1, wu1, wd1 = ffn1_gate[0].astype(BF16), ffn1_up[0].astype(BF16), ffn1_down[0].astype(BF16)
    wg2, wu2, wd2 = ffn2_gate[0].astype(BF16), ffn2_up[0].astype(BF16), ffn2_down[0].astype(BF16)
    win = w_in[0].astype(BF16)
    wo = w_out[0].astype(BF16)
    glu_w = s5_glu_w[0].astype(BF16)

    ldt_gs = jnp.broadcast_to(s5_log_dt[0][:, None], (S5_GROUPS, S5_STATE))

    def state_major(a):
        return jnp.repeat(a.T, S5_GROUP, axis=1)

    pre, pim, bcat, wout = _s5_prep(
        s5_lambda_re[0].reshape(1, S5_LANES), s5_lambda_im[0].reshape(1, S5_LANES), ldt_gs.reshape(1, S5_LANES),
        s5_b_re[0].reshape(S5_LANES, S5_GROUP).T, s5_b_im[0].reshape(S5_LANES, S5_GROUP).T,
        state_major(s5_lambda_re[0]), state_major(s5_lambda_im[0]), state_major(ldt_gs),
        s5_c_re[0].reshape(D_MODEL, S5_STATE).T, s5_c_im[0].reshape(D_MODEL, S5_STATE).T)
    s5_consts = (pre, pim, bcat, wout, _row(s5_d[0]), glu_w, _row(s5_glu_b[0]), _row(out_norm_s5[0]))

    bdq, bdk, bdv = _ml_prep(ml_wq[0].reshape(D_MODEL, ML_QKV_BLOCK), ml_wk[0].reshape(D_MODEL, ML_QKV_BLOCK),
                             ml_wv[0].reshape(D_MODEL, ML_QKV_BLOCK))
    gate_w = _pad_lanes(jnp.concatenate([ml_igate_w[0], ml_fgate_w[0]], axis=1), GATE_LANES).astype(BF16)
    gate_b = _pad_lanes(jnp.concatenate([ml_igate_b[0], ml_fgate_b[0]])[None, :], GATE_LANES)
    ml_consts = (ml_conv_w[0], _row(ml_conv_b[0]), bdq, bdk, bdv, gate_w, gate_b, _row(ml_norm_w[0]),
                 _row(ml_skip[0]), _row(out_norm_ml[0]))

    ffn1 = (_row(norm_ffn1[0]), wg1, wu1, wd1, _row(norm_mix[0]), win)
    seg_p = ((0, tile_p // SUBLANES),)
    pad_m = max(SUBLANES * S5_T - n_meta, 0)
    tm_m = n_meta + pad_m
    n_small = -(-(n_s + tm_m) // SMALL_TILE_ALIGN) * SMALL_TILE_ALIGN
    small = jnp.concatenate([x_sample.reshape(n_s, D_MODEL), jnp.zeros((pad_m, D_MODEL), F32), meta_tokens,
                             jnp.zeros((n_small - n_s - tm_m, D_MODEL), F32)], axis=0)
    x1_p, u_p, xm_p, z_p = _ffn_in(x_prompt.reshape(n_p, D_MODEL), *ffn1, tm=tile_p, seg_layout=seg_p)
    x1_s, u_s, xm_s, z_s = _ffn_in(small, *ffn1, tm=n_small, seg_layout=())

    zs5 = jnp.zeros((1, 1, 1, S5_LANES), F32)
    _, mre, mim = _s5(u_s[n_s:n_s + tm_m], zs5, zs5, s5_consts, n_outer=1, n_tiles=1, steps=1, tm=tm_m,
                      frame_order_io=True)
    y5_p, pre_s, pim_s = _s5(u_p, mre, mim, s5_consts, n_outer=1, n_tiles=nb, steps=seq // tile_p, tm=tile_p,
                             shared_init=True)
    n_grp = ns // SUBLANES
    y5_s, sre_s, sim_s = _s5(u_s[:n_s], state_s5_re[0].reshape(1, n_grp, SUBLANES, S5_LANES),
                             state_s5_im[0].reshape(1, n_grp, SUBLANES, S5_LANES), s5_consts,
                             n_outer=1, n_tiles=n_grp, steps=1, tm=SUBLANES * dseq, independent=True,
                             frame_order_io=True)

    ml_tile_p = min(ML_TILE, seq)
    zc = jnp.zeros((1, 1, ML_HEADS, ML_HEAD_DIM, ML_HEAD_DIM), F32)
    zn = jnp.zeros((1, 1, ML_HEADS, ML_HEAD_DIM), F32)
    zm = jnp.zeros((1, 1, 1, ML_HEADS), F32)
    zcv = jnp.zeros((1, 1, ML_CONV - 1, D_MODEL), F32)
    m_rows = slice(n_s + pad_m, n_s + pad_m + n_meta)
    _, c_m, n_m, m_m, cv_m = _mlstm(xm_s[m_rows], z_s[m_rows], zc, zn, zm, zcv, ml_consts, n_outer=1, n_tiles=1,
                                    steps=1, tm=n_meta, tc=n_meta)
    ym_p, c_p, nn_p, m_p, cv_p = _mlstm(xm_p, z_p, c_m[None], n_m[None], m_m[None], cv_m[None], ml_consts,
                                        n_outer=1, n_tiles=nb, steps=seq // ml_tile_p, tm=ml_tile_p,
                                        tc=min(ML_CHUNK, ml_tile_p), shared_init=True, pipelined=True)
    grp = ML_SAMPLE_GROUP
    st5 = lambda a: a.reshape((ns // grp, grp) + a.shape[1:])
    ym_s, c_s, nn_s, m_s, cv_s = _mlstm(xm_s[:n_s], z_s[:n_s], st5(state_mlstm_c[0]), st5(state_mlstm_n[0]),
                                        st5(state_mlstm_m[0][:, None, :]), st5(state_mlstm_conv[0]), ml_consts,
                                        n_outer=ns // grp, n_tiles=grp, steps=1, tm=dseq, tc=dseq)

    ffn2 = (wo, _row(norm_ffn2[0]), wg2, wu2, wd2, _row(norm_final))
    out_tile = min(OUT_TILE, seq)
    seg_o = tuple((k * tile_p, tile_p // SUBLANES) for k in range(out_tile // tile_p))
    y_p = _out_ffn(x1_p, y5_p, ym_p, *ffn2, tm=out_tile, seg_layout=seg_o)
    y_s = _out_ffn(x1_s, y5_s, ym_s, *ffn2, tm=n_s, seg_layout=())

    def s5_state(s, n):
        return s.reshape(1, n, S5_GROUPS, S5_STATE)

    return (y_p.reshape(nb, seq, D_MODEL), y_s.reshape(ns, dseq, D_MODEL),
            s5_state(pre_s, nb), s5_state(pim_s, nb), c_p[None], nn_p[None], m_p[:, 0][None], cv_p[None],
            s5_state(sre_s, ns), s5_state(sim_s, ns), c_s[None], nn_s[None], m_s[:, 0][None], cv_s[None])
```

```python
import functools

import jax
import jax.numpy as jnp
from jax import lax
from jax.experimental import pallas as pl
from jax.experimental.pallas import tpu as pltpu

F32 = jnp.float32
BF16 = jnp.bfloat16

D_MODEL = 1024
D_FF = 2816
S5_GROUPS = 64
S5_GROUP = 16
S5_STATE = 64
S5_LANES = S5_GROUPS * S5_STATE
MXU_TILE = 256
S5_BLOCKS = 8
S5_BLOCK_GROUPS = S5_GROUPS // S5_BLOCKS
S5_BLOCK_IN = D_MODEL // S5_BLOCKS
S5_BLOCK_ST = S5_LANES // S5_BLOCKS
ML_HEADS = 4
ML_HEAD_DIM = 256
ML_CONV = 4
ML_QKV_BLOCK = 4
EPS = 1e-6
SUBLANES = 8
GATE_LANES = 128
POW_ROWS = 64
S5_T = 2
VMEM_LIMIT = 56 * 1024 * 1024
SINGLE_BUFFER_BYTES = 2 * 1024 * 1024

TOKEN_TILE = 512
CAST_STEPS = 16
ML_CHUNK = 256
ML_TILE = 256
ML_SAMPLE_GROUP = 8
SMALL_TILE_ALIGN = 16


def _rms(x, g):
    return x * lax.rsqrt(jnp.mean(x * x, axis=-1, keepdims=True) + EPS) * g


def _silu(x):
    h = 0.5 * x
    return h + h * jnp.tanh(h)


def _swiglu(h, wg_ref, wu_ref, wd_ref, acc_ref, ff_chunk):
    for c in range(D_FF // ff_chunk):
        sl = slice(c * ff_chunk, (c + 1) * ff_chunk)
        g = jnp.dot(h, wg_ref[:, sl], preferred_element_type=F32)
        u = jnp.dot(h, wu_ref[:, sl], preferred_element_type=F32)
        a = (_silu(g) * u).astype(BF16)
        d = jnp.dot(a, wd_ref[sl, :], preferred_element_type=F32)
        if c == 0:
            acc_ref[...] = d
        else:
            acc_ref[...] += d
    return acc_ref[...]


def _const_spec(a):
    nd = a.ndim
    if a.size * a.dtype.itemsize >= SINGLE_BUFFER_BYTES:
        return pl.BlockSpec(a.shape, lambda *_: (0,) * nd, pipeline_mode=pl.Buffered(1))
    return pl.BlockSpec(a.shape, lambda *_: (0,) * nd)


def _segment_rows(x, layout, inverse=False):
    if not layout:
        return x
    parts = []
    for row0, r in layout:
        n = SUBLANES * r
        g = x[row0:row0 + n]
        if r % SUBLANES == 0:
            shape = (r, SUBLANES) if inverse else (SUBLANES, r)
            g = jnp.swapaxes(g.reshape(shape + g.shape[1:]), 0, 1).reshape(g.shape)
        elif inverse:
            g = jnp.concatenate([g[i * SUBLANES + a:i * SUBLANES + a + 1]
                                 for a in range(SUBLANES) for i in range(r)], axis=0)
        else:
            g = jnp.concatenate([g[a * r + i:a * r + i + 1]
                                 for i in range(r) for a in range(SUBLANES)], axis=0)
        parts.append(g)
    return parts[0] if len(parts) == 1 else jnp.concatenate(parts, axis=0)


def _ffn_in_tile(x, consts, acc_ref, ff_chunk, seg_layout):
    g1_ref, wg_ref, wu_ref, wd_ref, g2_ref, win_ref = consts
    h = _rms(x, g1_ref[...]).astype(BF16)
    x1 = x + 0.5 * _swiglu(h, wg_ref, wu_ref, wd_ref, acc_ref, ff_chunk)
    h2f = _rms(x1, g2_ref[...])
    h2 = h2f.astype(BF16)
    h2s = _segment_rows(h2f, seg_layout).astype(BF16)
    u = jnp.dot(h2s, win_ref[:, 0:D_MODEL], preferred_element_type=F32)
    xm = jnp.dot(h2, win_ref[:, D_MODEL:2 * D_MODEL], preferred_element_type=F32)
    z = jnp.dot(h2, win_ref[:, 2 * D_MODEL:3 * D_MODEL], preferred_element_type=F32)
    return x1, u, xm, z


def _ffn_in_kernel(x_ref, *refs, ff_chunk, seg_layout):
    consts, outs, acc_ref = refs[:6], refs[6:10], refs[10]
    for o_ref, v in zip(outs, _ffn_in_tile(x_ref[...], consts, acc_ref, ff_chunk, seg_layout)):
        o_ref[...] = v


def _ffn_in(x, g1, wg, wu, wd, g2, win, *, tm, seg_layout, ff_chunk=256):
    rows = x.shape[0]
    row_spec = pl.BlockSpec((tm, D_MODEL), lambda i: (i, 0))
    consts = (g1, wg, wu, wd, g2, win)
    return pl.pallas_call(
        functools.partial(_ffn_in_kernel, ff_chunk=ff_chunk, seg_layout=seg_layout),
        grid=(rows // tm,),
        in_specs=[row_spec] + [_const_spec(a) for a in consts],
        out_specs=[row_spec] * 4,
        out_shape=[jax.ShapeDtypeStruct(x.shape, F32)] * 4,
        scratch_shapes=[pltpu.VMEM((tm, D_MODEL), F32)],
        compiler_params=pltpu.CompilerParams(dimension_semantics=("parallel",), vmem_limit_bytes=VMEM_LIMIT),
        name="ffn_in",
    )(x, *consts)


def _out_ffn_tile(x1, y5, ym, consts, acc_ref, ff_chunk, seg_layout):
    wo_ref, g_ref, wg_ref, wu_ref, wd_ref, gf_ref = consts
    p5 = jnp.dot(y5, wo_ref[0:D_MODEL, :], preferred_element_type=F32)
    x2 = (x1 + _segment_rows(p5, seg_layout, inverse=True)
          + jnp.dot(ym, wo_ref[D_MODEL:2 * D_MODEL, :], preferred_element_type=F32))
    h = _rms(x2, g_ref[...]).astype(BF16)
    x3 = x2 + 0.5 * _swiglu(h, wg_ref, wu_ref, wd_ref, acc_ref, ff_chunk)
    return _rms(x3, gf_ref[...])


def _out_ffn_kernel(x1p_ref, y5p_ref, ymp_ref, x1s_ref, y5s_ref, yms_ref, wo_ref, wg_ref, wu_ref, wd_ref, g_ref, gf_ref,
                    op_ref, os_ref, wo_b_ref, wg_b_ref, wu_b_ref, wd_b_ref, acc_p_ref, acc_s_ref, *,
                    ff_chunk, seg_layout, steps_p):
    i = pl.program_id(0)
    consts = (wo_b_ref, g_ref, wg_b_ref, wu_b_ref, wd_b_ref, gf_ref)

    @pl.when(i < CAST_STEPS)
    def _():
        for src_ref, dst_ref in ((wo_ref, wo_b_ref), (wg_ref, wg_b_ref), (wu_ref, wu_b_ref), (wd_ref, wd_b_ref)):
            rows = src_ref.shape[0]
            dst_ref[pl.ds(pl.multiple_of(i * rows, rows), rows), :] = src_ref[...].astype(BF16)

    @pl.when((i >= CAST_STEPS) & (i < CAST_STEPS + steps_p))
    def _():
        op_ref[...] = _out_ffn_tile(x1p_ref[...], y5p_ref[...], ymp_ref[...], consts, acc_p_ref, ff_chunk, seg_layout)

    @pl.when(i >= CAST_STEPS + steps_p)
    def _():
        os_ref[...] = _out_ffn_tile(x1s_ref[...], y5s_ref[...], yms_ref[...], consts, acc_s_ref, ff_chunk, ())


def _out_ffn(x1p, y5p, ymp, x1s, y5s, yms, wo, g, wg, wu, wd, gf, *, tm, tm_s, seg_layout, ff_chunk=256):
    steps_p, steps_s = x1p.shape[0] // tm, y5s.shape[0] // tm_s
    spec_p = pl.BlockSpec((tm, D_MODEL), lambda i: (jnp.clip(i - CAST_STEPS, 0, steps_p - 1), 0))
    spec_s = pl.BlockSpec((tm_s, D_MODEL), lambda i: (jnp.clip(i - CAST_STEPS - steps_p, 0, steps_s - 1), 0))

    def chunk_spec(w):
        return pl.BlockSpec((w.shape[0] // CAST_STEPS, w.shape[1]), lambda i: (jnp.minimum(i, CAST_STEPS - 1), 0))

    return pl.pallas_call(
        functools.partial(_out_ffn_kernel, ff_chunk=ff_chunk, seg_layout=seg_layout, steps_p=steps_p),
        grid=(CAST_STEPS + steps_p + steps_s,),
        in_specs=[spec_p] * 3 + [spec_s] * 3 + [chunk_spec(w) for w in (wo, wg, wu, wd)]
                 + [_const_spec(g), _const_spec(gf)],
        out_specs=[spec_p, spec_s],
        out_shape=[jax.ShapeDtypeStruct(x1p.shape, F32), jax.ShapeDtypeStruct(y5s.shape, F32)],
        scratch_shapes=[pltpu.VMEM(w.shape, BF16) for w in (wo, wg, wu, wd)]
                       + [pltpu.VMEM((tm, D_MODEL), F32), pltpu.VMEM((tm_s, D_MODEL), F32)],
        compiler_params=pltpu.CompilerParams(dimension_semantics=("arbitrary",), vmem_limit_bytes=VMEM_LIMIT),
        name="out_ffn",
    )(x1p, y5p, ymp, x1s, y5s, yms, wo, wg, wu, wd, g, gf)


def _lam_bar(lr, li, ldt):
    dt = jnp.exp(ldt)
    mag = jnp.exp(lr * dt)
    th = li * dt
    return mag * jnp.cos(th), mag * jnp.sin(th), dt


def _cmul(ar, ai, br, bi):
    return ar * br - ai * bi, ar * bi + ai * br


def _dot_split(a, b):
    a_hi = a.astype(BF16)
    b_hi = b.astype(BF16)
    a_lo = (a - a_hi.astype(F32)).astype(BF16)
    b_lo = (b - b_hi.astype(F32)).astype(BF16)
    return (jnp.dot(a_hi, b_hi, preferred_element_type=F32) + jnp.dot(a_hi, b_lo, preferred_element_type=F32)
            + jnp.dot(a_lo, b_hi, preferred_element_type=F32))


def _s5_prep_kernel(lre_ref, lim_ref, ldt_ref, bre_ref, bim_ref, lre_t_ref, lim_t_ref, ldt_t_ref, cre_ref, cim_ref,
                    pre_ref, pim_ref, bcat_ref, wout_ref):
    lr = lre_ref[...]
    li = lim_ref[...]
    ar, ai, _ = _lam_bar(lr, li, ldt_ref[...])
    nr = ar - 1.0
    den = lr * lr + li * li
    cr = (nr * lr + ai * li) / den
    ci = (ai * lr - nr * li) / den
    bs = [_cmul(cr, ci, bre_ref[...], bim_ref[...])]
    for _ in range(1, S5_T):
        bs.append(_cmul(ar, ai, *bs[-1]))

    pr, pi = ar, ai
    pre_ref[0:1, :] = pr
    pim_ref[0:1, :] = pi
    for j in range(1, POW_ROWS):
        pr, pi = _cmul(pr, pi, ar, ai)
        pre_ref[j:j + 1, :] = pr
        pim_ref[j:j + 1, :] = pi

    tr, ti, _ = _lam_bar(lre_t_ref[...], lim_t_ref[...], ldt_t_ref[...])
    cs = [(cre_ref[...], cim_ref[...])]
    for _ in range(S5_T):
        cs.append(_cmul(tr, ti, *cs[-1]))

    g_shift, s_shift = S5_GROUP.bit_length() - 1, S5_STATE.bit_length() - 1
    b_shape = (S5_BLOCK_IN, S5_BLOCK_ST)
    b_mask = (lax.broadcasted_iota(jnp.int32, b_shape, 0) >> g_shift) == (
        lax.broadcasted_iota(jnp.int32, b_shape, 1) >> s_shift)
    c_shape = (S5_BLOCK_ST, S5_BLOCK_IN)
    c_mask = (lax.broadcasted_iota(jnp.int32, c_shape, 0) >> s_shift) == (
        lax.broadcasted_iota(jnp.int32, c_shape, 1) >> g_shift)

    def b_block(br, bi, blk):
        parts = []
        for x in (br, bi):
            x = jnp.concatenate([x[:, blk * S5_BLOCK_ST:(blk + 1) * S5_BLOCK_ST]] * S5_BLOCK_GROUPS, axis=0)
            parts.append(jnp.where(b_mask, x, 0.0))
        return jnp.concatenate(parts, axis=1)

    def c_block(xr, xi, blk):
        parts = []
        for x in (xr, -xi):
            x = jnp.concatenate([x[:, blk * S5_BLOCK_IN:(blk + 1) * S5_BLOCK_IN]] * S5_BLOCK_GROUPS, axis=0)
            parts.append(jnp.where(c_mask, x, 0.0))
        return jnp.concatenate(parts, axis=0)

    zero = jnp.zeros((S5_BLOCK_IN, S5_BLOCK_IN), F32)
    z_rows = 2 * S5_BLOCK_ST
    for blk in range(S5_BLOCKS):
        bms = [b_block(br, bi, blk) for br, bi in bs]
        cms = [c_block(xr, xi, blk) for xr, xi in cs]
        kcat = _dot_split(bms[0], jnp.concatenate(cms[:S5_T], axis=1))
        ks = [kcat[:, j * S5_BLOCK_IN:(j + 1) * S5_BLOCK_IN] for j in range(S5_T)]
        for f in range(S5_T):
            bcat_ref[blk, f * S5_BLOCK_IN:(f + 1) * S5_BLOCK_IN, :] = bms[S5_T - 1 - f].astype(BF16)
            row0 = z_rows + f * S5_BLOCK_IN
            wout_ref[blk, row0:row0 + S5_BLOCK_IN, :] = jnp.concatenate(
                [ks[g - f] if g >= f else zero for g in range(S5_T)], axis=1).astype(BF16)
        wout_ref[blk, 0:z_rows, :] = jnp.concatenate(cms[1:], axis=1).astype(BF16)


def _s5_prep(lre, lim, ldt, bre_t, bim_t, lre_t, lim_t, ldt_t, cre_t, cim_t):
    pow_shape = jax.ShapeDtypeStruct((POW_ROWS, S5_LANES), F32)
    return pl.pallas_call(
        _s5_prep_kernel,
        out_shape=[pow_shape, pow_shape,
                   jax.ShapeDtypeStruct((S5_BLOCKS, S5_T * S5_BLOCK_IN, 2 * S5_BLOCK_ST), BF16),
                   jax.ShapeDtypeStruct((S5_BLOCKS, 2 * S5_BLOCK_ST + S5_T * S5_BLOCK_IN, S5_T * S5_BLOCK_IN), BF16)],
        compiler_params=pltpu.CompilerParams(vmem_limit_bytes=VMEM_LIMIT),
        name="s5_prep",
    )(lre, lim, ldt, bre_t, bim_t, lre_t, lim_t, ldt_t, cre_t, cim_t)


def _ml_prep_kernel(wq_ref, wk_ref, wv_ref, oq_ref, ok_ref, ov_ref):
    shape = (MXU_TILE, MXU_TILE)
    shift = ML_QKV_BLOCK.bit_length() - 1
    row = lax.broadcasted_iota(jnp.int32, shape, 0)
    col = lax.broadcasted_iota(jnp.int32, shape, 1)
    same = (row >> shift) == (col >> shift)
    sel = col & (ML_QKV_BLOCK - 1)
    for w_ref, o_ref in ((wq_ref, oq_ref), (wk_ref, ok_ref), (wv_ref, ov_ref)):
        for c in range(D_MODEL // MXU_TILE):
            w = w_ref[c * MXU_TILE:(c + 1) * MXU_TILE, :]
            acc = jnp.zeros(shape, F32)
            for o in range(ML_QKV_BLOCK):
                acc = jnp.where(sel == o, w[:, o:o + 1], acc)
            o_ref[c] = jnp.where(same, acc, 0.0).astype(BF16)


def _ml_prep(wq, wk, wv):
    out = jax.ShapeDtypeStruct((D_MODEL // MXU_TILE, MXU_TILE, MXU_TILE), BF16)
    return pl.pallas_call(_ml_prep_kernel, out_shape=[out] * 3, name="ml_prep")(wq, wk, wv)


def _block_diag_dot(x, w_ref):
    return jnp.concatenate(
        [jnp.dot(x[:, c * MXU_TILE:(c + 1) * MXU_TILE], w_ref[c], preferred_element_type=F32)
         for c in range(w_ref.shape[0])], axis=1)


def _s5_kernel(u_ref, sre0_ref, sim0_ref, pre_ref, pim_ref, bcat_ref, wout_ref, d_ref, gw_ref, gb_ref, on_ref,
               y_ref, sre_ref, sim_ref, s_ref, *, tm, independent, n_tiles, frame_order_io):
    r = tm // SUBLANES
    io_layout = ((0, r),) if frame_order_io else ()
    nc = r // S5_T
    rows_c = nc * SUBLANES
    t = pl.program_id(1)

    @pl.when(t == 0)
    def _():
        sre_ref[...] = jnp.broadcast_to(sre0_ref[...], sre_ref.shape)
        sim_ref[...] = jnp.broadcast_to(sim0_ref[...], sim_ref.shape)

    ups = [_segment_rows(u_ref[g], io_layout) for g in range(n_tiles)]
    us = []
    for f in range(S5_T):
        us.append(jnp.concatenate(
            [up.reshape(nc, S5_T * SUBLANES, D_MODEL)[:, f * SUBLANES:(f + 1) * SUBLANES, :].reshape(rows_c, D_MODEL)
             for up in ups], axis=0).astype(BF16))

    ys = [[] for _ in range(S5_T)]
    for b in range(S5_BLOCKS):
        lanes = slice(b * S5_BLOCK_ST, (b + 1) * S5_BLOCK_ST)
        chans = slice(b * S5_BLOCK_IN, (b + 1) * S5_BLOCK_IN)
        re = slice(0, S5_BLOCK_ST)
        im = slice(S5_BLOCK_ST, 2 * S5_BLOCK_ST)
        ucat = jnp.concatenate([u[:, chans] for u in us], axis=1)
        s_ref[...] = jnp.dot(ucat, bcat_ref[b], preferred_element_type=F32)
        ltr = jnp.broadcast_to(pre_ref[S5_T - 1:S5_T, lanes], (SUBLANES, S5_BLOCK_ST))
        lti = jnp.broadcast_to(pim_ref[S5_T - 1:S5_T, lanes], (SUBLANES, S5_BLOCK_ST))

        def scan(cr, ci, row0):
            for c in range(nc):
                rows = slice(row0 + c * SUBLANES, row0 + (c + 1) * SUBLANES)
                nr = ltr * cr - lti * ci + s_ref[rows, re]
                ni = ltr * ci + lti * cr + s_ref[rows, im]
                s_ref[rows, re] = cr
                s_ref[rows, im] = ci
                cr, ci = nr, ni
            return cr, ci

        for g in range(n_tiles):
            row0 = g * rows_c
            if independent:
                fr, fi = scan(sre_ref[g, :, lanes], sim_ref[g, :, lanes], row0)
                sre_ref[g, :, lanes] = fr
                sim_ref[g, :, lanes] = fi
            else:
                zero = jnp.zeros((SUBLANES, S5_BLOCK_ST), F32)
                fr, fi = scan(zero, zero, row0)
                rr = pre_ref[r - 1:r, lanes]
                ri = pim_ref[r - 1:r, lanes]
                cr = sre_ref[g, :, lanes]
                ci = sim_ref[g, :, lanes]
                rows_r, rows_i = [], []
                for a in range(SUBLANES):
                    rows_r.append(cr)
                    rows_i.append(ci)
                    cr, ci = rr * cr - ri * ci + fr[a:a + 1], rr * ci + ri * cr + fi[a:a + 1]
                sre_ref[g, :, lanes] = cr
                sim_ref[g, :, lanes] = ci
                cin_r = jnp.concatenate(rows_r, axis=0)
                cin_i = jnp.concatenate(rows_i, axis=0)
                s_ref[row0:row0 + SUBLANES, re] = cin_r
                s_ref[row0:row0 + SUBLANES, im] = cin_i
                for c in range(1, nc):
                    rows = slice(row0 + c * SUBLANES, row0 + (c + 1) * SUBLANES)
                    pr = pre_ref[S5_T * c - 1:S5_T * c, lanes]
                    pi = pim_ref[S5_T * c - 1:S5_T * c, lanes]
                    s_ref[rows, re] += pr * cin_r - pi * cin_i
                    s_ref[rows, im] += pr * cin_i + pi * cin_r
        yb = jnp.dot(jnp.concatenate([s_ref[...].astype(BF16), ucat], axis=1), wout_ref[b],
                     preferred_element_type=F32)
        for f in range(S5_T):
            ys[f].append(yb[:, f * S5_BLOCK_IN:(f + 1) * S5_BLOCK_IN])

    yf = [jnp.concatenate(ys[f], axis=1) for f in range(S5_T)]
    for g in range(n_tiles):
        rows = slice(g * rows_c, (g + 1) * rows_c)
        y = jnp.concatenate([y[rows].reshape(nc, SUBLANES, D_MODEL) for y in yf], axis=1).reshape(tm, D_MODEL)
        y = y + d_ref[...] * ups[g]
        gl = jax.nn.gelu(y)
        o = gl * jax.nn.sigmoid(jnp.dot(gl.astype(BF16), gw_ref[...], preferred_element_type=F32) + gb_ref[...])
        y_ref[g] = _segment_rows(_rms(o, on_ref[...]), io_layout, inverse=True).astype(BF16)


def _s5(u, sre0, sim0, consts, *, n_outer, n_tiles, steps, tm, independent=False, shared_init=False,
        frame_order_io=False):
    srows = SUBLANES if independent else 1
    u5 = u.reshape(n_outer, n_tiles, steps, tm, D_MODEL)
    seq_spec = pl.BlockSpec((None, n_tiles, None, tm, D_MODEL), lambda o, t: (o, 0, t, 0, 0))
    if shared_init:
        st_in = pl.BlockSpec((None, 1, srows, S5_LANES), lambda o, t: (0, 0, 0, 0))
    else:
        st_in = pl.BlockSpec((None, n_tiles, srows, S5_LANES), lambda o, t: (o, 0, 0, 0))
    st_out = pl.BlockSpec((None, n_tiles, srows, S5_LANES), lambda o, t: (o, 0, 0, 0))
    st_shape = jax.ShapeDtypeStruct((n_outer, n_tiles, srows, S5_LANES), F32)
    y, sre, sim = pl.pallas_call(
        functools.partial(_s5_kernel, tm=tm, independent=independent, n_tiles=n_tiles,
                          frame_order_io=frame_order_io),
        grid=(n_outer, steps),
        in_specs=[seq_spec, st_in, st_in] + [_const_spec(a) for a in consts],
        out_specs=[seq_spec, st_out, st_out],
        out_shape=[jax.ShapeDtypeStruct(u5.shape, BF16), st_shape, st_shape],
        scratch_shapes=[pltpu.VMEM((n_tiles * tm // S5_T, 2 * S5_BLOCK_ST), F32)],
        compiler_params=pltpu.CompilerParams(dimension_semantics=("parallel", "arbitrary"),
                                             vmem_limit_bytes=VMEM_LIMIT),
        name="s5_mix",
    )(u5, sre0, sim0, *consts)
    return y.reshape(u.shape), sre, sim


def _ml_front(xm_ref, cv_ref, xbuf_ref, consts, bufs, tm, n_tiles):
    cw_ref, cb_ref, bdq_ref, bdk_ref, bdv_ref, gw_ref, gb_ref = consts
    xc_ref, q_ref, k_ref, v_ref, vf_ref, g_ref = bufs
    pre = ML_CONV - 1
    cw = cw_ref[...]
    xms, xcs = [], []
    for g in range(n_tiles):
        xm = xm_ref[g]
        xbuf_ref[g, SUBLANES - pre:SUBLANES, :] = cv_ref[g]
        xbuf_ref[g, SUBLANES:SUBLANES + tm, :] = xm
        xc = cb_ref[...] + cw[pre:pre + 1] * xm
        for j in range(pre):
            xc = xc + cw[j:j + 1] * xbuf_ref[g, SUBLANES - pre + j:SUBLANES - pre + j + tm, :]
        cv_ref[g] = xbuf_ref[g, SUBLANES + tm - pre:SUBLANES + tm, :]
        xms.append(xm)
        xcs.append(_silu(xc))
    xm = xms[0] if n_tiles == 1 else jnp.concatenate(xms, axis=0)
    xc = xcs[0] if n_tiles == 1 else jnp.concatenate(xcs, axis=0)
    xc_ref[...] = xc

    xcb = xc.astype(BF16)
    q = _block_diag_dot(xcb, bdq_ref)
    k = _block_diag_dot(xcb, bdk_ref)
    v = _block_diag_dot(xm.astype(BF16), bdv_ref)
    qb = q.astype(BF16)
    vb = v.astype(BF16)
    g_ref[...] = jnp.dot(jnp.concatenate([qb, k.astype(BF16), vb], axis=1), gw_ref[...],
                         preferred_element_type=F32) + gb_ref[...]
    q_ref[...] = qb
    k_ref[...] = (k * (ML_HEAD_DIM ** -0.5)).astype(BF16)
    v_ref[...] = vb
    vf_ref[...] = v


def _ml_back(bufs, z_ref, state, consts, y_ref, h_ref, tm, tc, n_tiles):
    xc_ref, q_ref, k_ref, v_ref, vf_ref, g_ref = bufs
    c_ref, n_ref, m_ref = state
    nw_ref, sk_ref, on_ref = consts
    lane = lax.broadcasted_iota(jnp.int32, (tc, GATE_LANES), 1)
    row = lax.broadcasted_iota(jnp.int32, (tc, tc), 0)
    col = lax.broadcasted_iota(jnp.int32, (tc, tc), 1)
    causal = row >= col
    tril = causal.astype(F32)

    def chunk(rows, g):
        gates = g_ref[rows, :]
        lf = jnp.minimum(gates, 0.0) - jnp.log1p(jnp.exp(-jnp.abs(gates)))
        lf = jnp.where((lane >= ML_HEADS) & (lane < 2 * ML_HEADS), lf, 0.0)
        cum = jnp.dot(tril, lf, preferred_element_type=F32, precision=lax.Precision.HIGHEST)
        arr = jnp.where(lane < ML_HEADS, gates, cum)
        arr_t = arr.T
        for h in range(ML_HEADS):
            hs = slice(h * ML_HEAD_DIM, (h + 1) * ML_HEAD_DIM)
            qh = q_ref[rows, hs]
            kh = k_ref[rows, hs]
            vh = v_ref[rows, hs]
            ig_col = arr[:, h:h + 1]
            b_col = arr[:, ML_HEADS + h:ML_HEADS + h + 1]
            ig_row = arr_t[h:h + 1, :]
            b_row = arr_t[ML_HEADS + h:ML_HEADS + h + 1, :]
            m_prev = m_ref[g, :, h:h + 1]
            c_prev = c_ref[g, h]
            n_prev = n_ref[g, h:h + 1, :]

            logw = jnp.where(causal, b_col - b_row + ig_row, -jnp.inf)
            log_inter = b_col + m_prev
            m_t = jnp.maximum(log_inter, jnp.max(logw, axis=-1, keepdims=True))
            w = jnp.exp(logw - m_t)
            a_inter = jnp.exp(log_inter - m_t)
            s = lax.dot_general(qh, kh, (((1,), (1,)), ((), ())), preferred_element_type=F32) * w
            inter = lax.dot_general(qh, c_prev.astype(BF16), (((1,), (1,)), ((), ())),
                                    preferred_element_type=F32)
            num = a_inter * inter + jnp.dot(s.astype(BF16), vh, preferred_element_type=F32)
            den = (a_inter * jnp.sum(qh.astype(F32) * n_prev, axis=-1, keepdims=True)
                   + jnp.sum(s, axis=-1, keepdims=True))
            hh = num / jnp.maximum(jnp.abs(den), jnp.exp(-m_t))
            mu = jnp.mean(hh, axis=-1, keepdims=True)
            hc = hh - mu
            var = jnp.mean(hc * hc, axis=-1, keepdims=True)
            h_ref[rows, hs] = hc * lax.rsqrt(var + EPS)

            b_last = b_col[tc - 1:tc, :]
            m_new = m_t[tc - 1:tc, :]
            g_state = jnp.exp(b_last + m_prev - m_new)
            g_src = jnp.exp(b_last - b_col + ig_col - m_new)
            vs = (vf_ref[rows, hs] * g_src).astype(BF16)
            c_ref[g, h] = g_state * c_prev + lax.dot_general(vs, kh, (((0,), (0,)), ((), ())),
                                                             preferred_element_type=F32)
            n_ref[g, h:h + 1, :] = g_state * n_prev + jnp.sum(g_src * kh.astype(F32), axis=0, keepdims=True)
            m_ref[g, :, h:h + 1] = m_new

    for j in range(tm // tc):
        for g in range(n_tiles):
            chunk(slice(g * tm + j * tc, g * tm + (j + 1) * tc), g)

    for g in range(n_tiles):
        rows = slice(g * tm, (g + 1) * tm)
        out = (h_ref[rows, :] * nw_ref[...] + sk_ref[...] * xc_ref[rows, :]) * _silu(z_ref[g])
        y_ref[g] = _rms(out, on_ref[...]).astype(BF16)


def _mlstm_kernel(xm_ref, z_ref, c0_ref, n0_ref, m0_ref, cv0_ref, cw_ref, cb_ref, bdq_ref, bdk_ref, bdv_ref,
                  gw_ref, gb_ref, nw_ref, sk_ref, on_ref,
                  y_ref, c_ref, n_ref, m_ref, cv_ref, xbuf_ref, h_ref, *bufs, tm, tc, n_tiles, steps, pipelined):
    t = pl.program_id(1)
    front_consts = (cw_ref, cb_ref, bdq_ref, bdk_ref, bdv_ref, gw_ref, gb_ref)
    back_consts = (nw_ref, sk_ref, on_ref)
    state = (c_ref, n_ref, m_ref)
    n_buf = len(bufs) // 2 if pipelined else len(bufs)

    @pl.when(t == 0)
    def _():
        cv_ref[...] = jnp.broadcast_to(cv0_ref[...], cv_ref.shape)
        c_ref[...] = jnp.broadcast_to(c0_ref[...], c_ref.shape)
        n_ref[...] = jnp.broadcast_to(n0_ref[...], n_ref.shape)
        m_ref[...] = jnp.broadcast_to(m0_ref[...], m_ref.shape)

    def front(buf_set):
        _ml_front(xm_ref, cv_ref, xbuf_ref, front_consts, buf_set, tm, n_tiles)

    def back(buf_set):
        _ml_back(buf_set, z_ref, state, back_consts, y_ref, h_ref, tm, tc, n_tiles)

    if pipelined:
        sets = (bufs[:n_buf], bufs[n_buf:])
        mid = (t > 0) & (t < steps)
        pl.when(t == 0)(lambda: front(sets[0]))
        for parity in range(2):
            @pl.when(mid & (t % 2 == parity))
            def _(parity=parity):
                front(sets[parity])
                back(sets[1 - parity])
        pl.when(t == steps)(lambda: back(sets[(steps - 1) % 2]))
    else:
        front(bufs)
        back(bufs)


def _mlstm(xm, z, c0, n0, m0, cv0, consts, *, n_outer, n_tiles, steps, tm, tc, shared_init=False, pipelined=False):
    shape5 = (n_outer, n_tiles, steps, tm, D_MODEL)
    block = (None, n_tiles, None, tm, D_MODEL)
    if pipelined:
        grid_steps = steps + 1
        front_spec = pl.BlockSpec(block, lambda o, t: (o, 0, jnp.minimum(t, steps - 1), 0, 0))
        back_spec = pl.BlockSpec(block, lambda o, t: (o, 0, jnp.maximum(t - 1, 0), 0, 0))
    else:
        grid_steps = steps
        front_spec = back_spec = pl.BlockSpec(block, lambda o, t: (o, 0, t, 0, 0))

    def st(shape, shared):
        zeros = (0,) * len(shape)
        if shared:
            return pl.BlockSpec((None, 1) + shape, lambda o, t: (0, 0) + zeros)
        return pl.BlockSpec((None, n_tiles) + shape, lambda o, t: (o, 0) + zeros)

    st_shapes = ((ML_HEADS, ML_HEAD_DIM, ML_HEAD_DIM), (ML_HEADS, ML_HEAD_DIM), (1, ML_HEADS),
                 (ML_CONV - 1, D_MODEL))
    rows = n_tiles * tm
    buf_set = [pltpu.VMEM((rows, D_MODEL), F32), pltpu.VMEM((rows, D_MODEL), BF16), pltpu.VMEM((rows, D_MODEL), BF16),
               pltpu.VMEM((rows, D_MODEL), BF16), pltpu.VMEM((rows, D_MODEL), F32), pltpu.VMEM((rows, GATE_LANES), F32)]
    outs = pl.pallas_call(
        functools.partial(_mlstm_kernel, tm=tm, tc=tc, n_tiles=n_tiles, steps=steps, pipelined=pipelined),
        grid=(n_outer, grid_steps),
        in_specs=[front_spec, back_spec] + [st(s, shared_init) for s in st_shapes]
                 + [_const_spec(a) for a in consts],
        out_specs=[back_spec] + [st(s, False) for s in st_shapes],
        out_shape=[jax.ShapeDtypeStruct(shape5, BF16)]
                  + [jax.ShapeDtypeStruct((n_outer, n_tiles) + s, F32) for s in st_shapes],
        scratch_shapes=[pltpu.VMEM((n_tiles, tm + 2 * SUBLANES, D_MODEL), F32), pltpu.VMEM((rows, D_MODEL), F32)]
                       + buf_set * (2 if pipelined else 1),
        compiler_params=pltpu.CompilerParams(dimension_semantics=("parallel", "arbitrary"),
                                             vmem_limit_bytes=VMEM_LIMIT),
        name="mlstm_mix",
    )(xm.reshape(shape5), z.reshape(shape5), c0, n0, m0, cv0, *consts)
    n_streams = n_outer * n_tiles
    return (outs[0].reshape(xm.shape),) + tuple(o.reshape((n_streams,) + o.shape[2:]) for o in outs[1:])


def _row(v):
    return v.reshape(1, -1).astype(F32)


def _pad_lanes(v, width):
    return jnp.pad(v, [(0, 0)] * (v.ndim - 1) + [(0, width - v.shape[-1])])


def kernel(x_prompt, x_sample, state_s5_re, state_s5_im, state_mlstm_c, state_mlstm_n, state_mlstm_m,
           state_mlstm_conv, meta_tokens, norm_ffn1, ffn1_gate, ffn1_up, ffn1_down, norm_mix, w_in,
           s5_lambda_re, s5_lambda_im, s5_log_dt, s5_b_re, s5_b_im, s5_c_re, s5_c_im, s5_d, s5_glu_w, s5_glu_b,
           ml_conv_w, ml_conv_b, ml_wq, ml_wk, ml_wv, ml_igate_w, ml_igate_b, ml_fgate_w, ml_fgate_b,
           ml_norm_w, ml_skip, out_norm_s5, out_norm_ml, w_out, norm_ffn2, ffn2_gate, ffn2_up, ffn2_down,
           norm_final):
    nb, seq, _ = x_prompt.shape
    ns, dseq, _ = x_sample.shape
    n_meta = meta_tokens.shape[0]
    n_p, n_s = nb * seq, ns * dseq
    tile_p = min(TOKEN_TILE, seq)

    wg1, wu1, wd1 = ffn1_gate[0].astype(BF16), ffn1_up[0].astype(BF16), ffn1_down[0].astype(BF16)
    win = w_in[0].astype(BF16)
    glu_w = s5_glu_w[0].astype(BF16)

    ldt_gs = jnp.broadcast_to(s5_log_dt[0][:, None], (S5_GROUPS, S5_STATE))

    def state_major(a):
        return jnp.repeat(a.T, S5_GROUP, axis=1)

    pre, pim, bcat, wout = _s5_prep(
        s5_lambda_re[0].reshape(1, S5_LANES), s5_lambda_im[0].reshape(1, S5_LANES), ldt_gs.reshape(1, S5_LANES),
        s5_b_re[0].reshape(S5_LANES, S5_GROUP).T, s5_b_im[0].reshape(S5_LANES, S5_GROUP).T,
        state_major(s5_lambda_re[0]), state_major(s5_lambda_im[0]), state_major(ldt_gs),
        s5_c_re[0].reshape(D_MODEL, S5_STATE).T, s5_c_im[0].reshape(D_MODEL, S5_STATE).T)
    s5_consts = (pre, pim, bcat, wout, _row(s5_d[0]), glu_w, _row(s5_glu_b[0]), _row(out_norm_s5[0]))

    bdq, bdk, bdv = _ml_prep(ml_wq[0].reshape(D_MODEL, ML_QKV_BLOCK), ml_wk[0].reshape(D_MODEL, ML_QKV_BLOCK),
                             ml_wv[0].reshape(D_MODEL, ML_QKV_BLOCK))
    gate_w = _pad_lanes(jnp.concatenate([ml_igate_w[0], ml_fgate_w[0]], axis=1), GATE_LANES).astype(BF16)
    gate_b = _pad_lanes(jnp.concatenate([ml_igate_b[0], ml_fgate_b[0]])[None, :], GATE_LANES)
    ml_consts = (ml_conv_w[0], _row(ml_conv_b[0]), bdq, bdk, bdv, gate_w, gate_b, _row(ml_norm_w[0]),
                 _row(ml_skip[0]), _row(out_norm_ml[0]))

    ffn1 = (_row(norm_ffn1[0]), wg1, wu1, wd1, _row(norm_mix[0]), win)
    seg_p = ((0, tile_p // SUBLANES),)
    pad_m = max(SUBLANES * S5_T - n_meta, 0)
    tm_m = n_meta + pad_m
    n_small = -(-(n_s + tm_m) // SMALL_TILE_ALIGN) * SMALL_TILE_ALIGN
    small = jnp.concatenate([x_sample.reshape(n_s, D_MODEL), jnp.zeros((pad_m, D_MODEL), F32), meta_tokens,
                             jnp.zeros((n_small - n_s - tm_m, D_MODEL), F32)], axis=0)
    x1_p, u_p, xm_p, z_p = _ffn_in(x_prompt.reshape(n_p, D_MODEL), *ffn1, tm=tile_p, seg_layout=seg_p)
    x1_s, u_s, xm_s, z_s = _ffn_in(small, *ffn1, tm=n_small, seg_layout=())

    zs5 = jnp.zeros((1, 1, 1, S5_LANES), F32)
    _, mre, mim = _s5(u_s[n_s:n_s + tm_m], zs5, zs5, s5_consts, n_outer=1, n_tiles=1, steps=1, tm=tm_m,
                      frame_order_io=True)
    y5_p, pre_s, pim_s = _s5(u_p, mre, mim, s5_consts, n_outer=1, n_tiles=nb, steps=seq // tile_p, tm=tile_p,
                             shared_init=True)
    n_grp = ns // SUBLANES
    y5_s, sre_s, sim_s = _s5(u_s[:n_s], state_s5_re[0].reshape(1, n_grp, SUBLANES, S5_LANES),
                             state_s5_im[0].reshape(1, n_grp, SUBLANES, S5_LANES), s5_consts,
                             n_outer=1, n_tiles=n_grp, steps=1, tm=SUBLANES * dseq, independent=True,
                             frame_order_io=True)

    ml_tile_p = min(ML_TILE, seq)
    zc = jnp.zeros((1, 1, ML_HEADS, ML_HEAD_DIM, ML_HEAD_DIM), F32)
    zn = jnp.zeros((1, 1, ML_HEADS, ML_HEAD_DIM), F32)
    zm = jnp.zeros((1, 1, 1, ML_HEADS), F32)
    zcv = jnp.zeros((1, 1, ML_CONV - 1, D_MODEL), F32)
    m_rows = slice(n_s + pad_m, n_s + pad_m + n_meta)
    _, c_m, n_m, m_m, cv_m = _mlstm(xm_s[m_rows], z_s[m_rows], zc, zn, zm, zcv, ml_consts, n_outer=1, n_tiles=1,
                                    steps=1, tm=n_meta, tc=n_meta)
    ym_p, c_p, nn_p, m_p, cv_p = _mlstm(xm_p, z_p, c_m[None], n_m[None], m_m[None], cv_m[None], ml_consts,
                                        n_outer=1, n_tiles=nb, steps=seq // ml_tile_p, tm=ml_tile_p,
                                        tc=min(ML_CHUNK, ml_tile_p), shared_init=True, pipelined=True)
    grp = ML_SAMPLE_GROUP
    st5 = lambda a: a.reshape((ns // grp, grp) + a.shape[1:])
    ym_s, c_s, nn_s, m_s, cv_s = _mlstm(xm_s[:n_s], z_s[:n_s], st5(state_mlstm_c[0]), st5(state_mlstm_n[0]),
                                        st5(state_mlstm_m[0][:, None, :]), st5(state_mlstm_conv[0]), ml_consts,
                                        n_outer=ns // grp, n_tiles=grp, steps=1, tm=dseq, tc=dseq)

    y_p, y_s = _out_ffn(x1_p, y5_p, ym_p, x1_s, y5_s, ym_s, w_out[0], _row(norm_ffn2[0]), ffn2_gate[0], ffn2_up[0],
                        ffn2_down[0], _row(norm_final), tm=tile_p, tm_s=n_s // 2, seg_layout=seg_p)

    def s5_state(s, n):
        return s.reshape(1, n, S5_GROUPS, S5_STATE)

    return (y_p.reshape(nb, seq, D_MODEL), y_s.reshape(ns, dseq, D_MODEL),
            s5_state(pre_s, nb), s5_state(pim_s, nb), c_p[None], nn_p[None], m_p[:, 0][None], cv_p[None],
            s5_state(sre_s, ns), s5_state(sim_s, ns), c_s[None], nn_s[None], m_s[:, 0][None], cv_s[None])
```

```python
import functools

import jax
import jax.numpy as jnp
from jax import lax
from jax.experimental import pallas as pl
from jax.experimental.pallas import tpu as pltpu

F32 = jnp.float32
BF16 = jnp.bfloat16

D_MODEL = 1024
D_FF = 2816
S5_GROUPS = 64
S5_GROUP = 16
S5_STATE = 64
S5_LANES = S5_GROUPS * S5_STATE
MXU_TILE = 256
S5_BLOCKS = 8
S5_BLOCK_GROUPS = S5_GROUPS // S5_BLOCKS
S5_BLOCK_IN = D_MODEL // S5_BLOCKS
S5_BLOCK_ST = S5_LANES // S5_BLOCKS
ML_HEADS = 4
ML_HEAD_DIM = 256
ML_CONV = 4
ML_QKV_BLOCK = 4
EPS = 1e-6
SUBLANES = 8
GATE_LANES = 128
POW_ROWS = 64
S5_T = 2
VMEM_LIMIT = 56 * 1024 * 1024
SINGLE_BUFFER_BYTES = 2 * 1024 * 1024

TOKEN_TILE = 512
CAST_STEPS = 16
ML_CHUNK = 256
ML_TILE = 256
ML_SAMPLE_GROUP = 4
SMALL_TILE_ALIGN = 16


def _rms(x, g):
    return x * lax.rsqrt(jnp.mean(x * x, axis=-1, keepdims=True) + EPS) * g


def _silu(x):
    h = 0.5 * x
    return h + h * jnp.tanh(h)


def _swiglu(h, wg_ref, wu_ref, wd_ref, acc_ref, ff_chunk):
    for c in range(D_FF // ff_chunk):
        sl = slice(c * ff_chunk, (c + 1) * ff_chunk)
        g = jnp.dot(h, wg_ref[:, sl], preferred_element_type=F32)
        u = jnp.dot(h, wu_ref[:, sl], preferred_element_type=F32)
        a = (_silu(g) * u).astype(BF16)
        d = jnp.dot(a, wd_ref[sl, :], preferred_element_type=F32)
        if c == 0:
            acc_ref[...] = d
        else:
            acc_ref[...] += d
    return acc_ref[...]


def _const_spec(a):
    nd = a.ndim
    if a.size * a.dtype.itemsize >= SINGLE_BUFFER_BYTES:
        return pl.BlockSpec(a.shape, lambda *_: (0,) * nd, pipeline_mode=pl.Buffered(1))
    return pl.BlockSpec(a.shape, lambda *_: (0,) * nd)


def _segment_rows(x, layout, inverse=False):
    if not layout:
        return x
    parts = []
    for row0, r in layout:
        n = SUBLANES * r
        g = x[row0:row0 + n]
        if r % SUBLANES == 0:
            shape = (r, SUBLANES) if inverse else (SUBLANES, r)
            g = jnp.swapaxes(g.reshape(shape + g.shape[1:]), 0, 1).reshape(g.shape)
        elif inverse:
            g = jnp.concatenate([g[i * SUBLANES + a:i * SUBLANES + a + 1]
                                 for a in range(SUBLANES) for i in range(r)], axis=0)
        else:
            g = jnp.concatenate([g[a * r + i:a * r + i + 1]
                                 for i in range(r) for a in range(SUBLANES)], axis=0)
        parts.append(g)
    return parts[0] if len(parts) == 1 else jnp.concatenate(parts, axis=0)


def _ffn_in_tile(x, consts, acc_ref, ff_chunk, seg_layout):
    g1_ref, wg_ref, wu_ref, wd_ref, g2_ref, win_ref = consts
    h = _rms(x, g1_ref[...]).astype(BF16)
    x1 = x + 0.5 * _swiglu(h, wg_ref, wu_ref, wd_ref, acc_ref, ff_chunk)
    h2f = _rms(x1, g2_ref[...])
    h2 = h2f.astype(BF16)
    h2s = _segment_rows(h2f, seg_layout).astype(BF16)
    u = jnp.dot(h2s, win_ref[:, 0:D_MODEL], preferred_element_type=F32)
    xm = jnp.dot(h2, win_ref[:, D_MODEL:2 * D_MODEL], preferred_element_type=F32)
    z = jnp.dot(h2, win_ref[:, 2 * D_MODEL:3 * D_MODEL], preferred_element_type=F32)
    return x1, u, xm, z


def _ffn_in_kernel(x_ref, *refs, ff_chunk, seg_layout):
    consts, outs, acc_ref = refs[:6], refs[6:10], refs[10]
    for o_ref, v in zip(outs, _ffn_in_tile(x_ref[...], consts, acc_ref, ff_chunk, seg_layout)):
        o_ref[...] = v


def _ffn_in(x, g1, wg, wu, wd, g2, win, *, tm, seg_layout, ff_chunk=256):
    rows = x.shape[0]
    row_spec = pl.BlockSpec((tm, D_MODEL), lambda i: (i, 0))
    consts = (g1, wg, wu, wd, g2, win)
    return pl.pallas_call(
        functools.partial(_ffn_in_kernel, ff_chunk=ff_chunk, seg_layout=seg_layout),
        grid=(rows // tm,),
        in_specs=[row_spec] + [_const_spec(a) for a in consts],
        out_specs=[row_spec] * 4,
        out_shape=[jax.ShapeDtypeStruct(x.shape, F32)] * 4,
        scratch_shapes=[pltpu.VMEM((tm, D_MODEL), F32)],
        compiler_params=pltpu.CompilerParams(dimension_semantics=("parallel",), vmem_limit_bytes=VMEM_LIMIT),
        name="ffn_in",
    )(x, *consts)


def _ffn_in_cast_kernel(x_ref, wg_ref, wu_ref, wd_ref, win_ref, g1_ref, g2_ref,
                        x1_ref, u_ref, xm_ref, z_ref, wg_o_ref, wu_o_ref, wd_o_ref, win_o_ref,
                        wg_b_ref, wu_b_ref, wd_b_ref, win_b_ref, acc_ref, *, ff_chunk, seg_layout):
    i = pl.program_id(0)

    @pl.when(i < CAST_STEPS)
    def _():
        for src_ref, dst_ref, out_ref in ((wg_ref, wg_b_ref, wg_o_ref), (wu_ref, wu_b_ref, wu_o_ref),
                                          (wd_ref, wd_b_ref, wd_o_ref), (win_ref, win_b_ref, win_o_ref)):
            rows = src_ref.shape[0]
            w = src_ref[...].astype(BF16)
            dst_ref[pl.ds(pl.multiple_of(i * rows, rows), rows), :] = w
            out_ref[...] = w

    @pl.when(i >= CAST_STEPS)
    def _():
        consts = (g1_ref, wg_b_ref, wu_b_ref, wd_b_ref, g2_ref, win_b_ref)
        for o_ref, v in zip((x1_ref, u_ref, xm_ref, z_ref),
                            _ffn_in_tile(x_ref[...], consts, acc_ref, ff_chunk, seg_layout)):
            o_ref[...] = v


def _ffn_in_cast(x, g1, wg, wu, wd, g2, win, *, tm, seg_layout, ff_chunk=256):
    steps = x.shape[0] // tm
    row_spec = pl.BlockSpec((tm, D_MODEL), lambda i: (jnp.clip(i - CAST_STEPS, 0, steps - 1), 0))
    weights = (wg, wu, wd, win)

    def chunk_spec(w):
        return pl.BlockSpec((w.shape[0] // CAST_STEPS, w.shape[1]), lambda i: (jnp.minimum(i, CAST_STEPS - 1), 0))

    outs = pl.pallas_call(
        functools.partial(_ffn_in_cast_kernel, ff_chunk=ff_chunk, seg_layout=seg_layout),
        grid=(CAST_STEPS + steps,),
        in_specs=[row_spec] + [chunk_spec(w) for w in weights] + [_const_spec(g1), _const_spec(g2)],
        out_specs=[row_spec] * 4 + [chunk_spec(w) for w in weights],
        out_shape=[jax.ShapeDtypeStruct(x.shape, F32)] * 4 + [jax.ShapeDtypeStruct(w.shape, BF16) for w in weights],
        scratch_shapes=[pltpu.VMEM(w.shape, BF16) for w in weights] + [pltpu.VMEM((tm, D_MODEL), F32)],
        compiler_params=pltpu.CompilerParams(dimension_semantics=("arbitrary",), vmem_limit_bytes=VMEM_LIMIT),
        name="ffn_in",
    )(x, *weights, g1, g2)
    return outs[:4], outs[4:]


def _out_ffn_tile(x1, y5, ym, consts, acc_ref, ff_chunk, seg_layout):
    wo_ref, g_ref, wg_ref, wu_ref, wd_ref, gf_ref = consts
    p5 = jnp.dot(y5, wo_ref[0:D_MODEL, :], preferred_element_type=F32)
    x2 = (x1 + _segment_rows(p5, seg_layout, inverse=True)
          + jnp.dot(ym, wo_ref[D_MODEL:2 * D_MODEL, :], preferred_element_type=F32))
    h = _rms(x2, g_ref[...]).astype(BF16)
    x3 = x2 + 0.5 * _swiglu(h, wg_ref, wu_ref, wd_ref, acc_ref, ff_chunk)
    return _rms(x3, gf_ref[...])


def _out_ffn_kernel(x1p_ref, y5p_ref, ymp_ref, x1s_ref, y5s_ref, yms_ref, wo_ref, wg_ref, wu_ref, wd_ref, g_ref, gf_ref,
                    op_ref, os_ref, wo_b_ref, wg_b_ref, wu_b_ref, wd_b_ref, acc_p_ref, acc_s_ref, *,
                    ff_chunk, seg_layout, steps_p):
    i = pl.program_id(0)
    consts = (wo_b_ref, g_ref, wg_b_ref, wu_b_ref, wd_b_ref, gf_ref)

    @pl.when(i < CAST_STEPS)
    def _():
        for src_ref, dst_ref in ((wo_ref, wo_b_ref), (wg_ref, wg_b_ref), (wu_ref, wu_b_ref), (wd_ref, wd_b_ref)):
            rows = src_ref.shape[0]
            dst_ref[pl.ds(pl.multiple_of(i * rows, rows), rows), :] = src_ref[...].astype(BF16)

    @pl.when((i >= CAST_STEPS) & (i < CAST_STEPS + steps_p))
    def _():
        op_ref[...] = _out_ffn_tile(x1p_ref[...], y5p_ref[...], ymp_ref[...], consts, acc_p_ref, ff_chunk, seg_layout)

    @pl.when(i >= CAST_STEPS + steps_p)
    def _():
        os_ref[...] = _out_ffn_tile(x1s_ref[...], y5s_ref[...], yms_ref[...], consts, acc_s_ref, ff_chunk, ())


def _out_ffn(x1p, y5p, ymp, x1s, y5s, yms, wo, g, wg, wu, wd, gf, *, tm, tm_s, seg_layout, ff_chunk=256):
    steps_p, steps_s = x1p.shape[0] // tm, y5s.shape[0] // tm_s
    spec_p = pl.BlockSpec((tm, D_MODEL), lambda i: (jnp.clip(i - CAST_STEPS, 0, steps_p - 1), 0))
    spec_s = pl.BlockSpec((tm_s, D_MODEL), lambda i: (jnp.clip(i - CAST_STEPS - steps_p, 0, steps_s - 1), 0))

    def chunk_spec(w):
        return pl.BlockSpec((w.shape[0] // CAST_STEPS, w.shape[1]), lambda i: (jnp.minimum(i, CAST_STEPS - 1), 0))

    return pl.pallas_call(
        functools.partial(_out_ffn_kernel, ff_chunk=ff_chunk, seg_layout=seg_layout, steps_p=steps_p),
        grid=(CAST_STEPS + steps_p + steps_s,),
        in_specs=[spec_p] * 3 + [spec_s] * 3 + [chunk_spec(w) for w in (wo, wg, wu, wd)]
                 + [_const_spec(g), _const_spec(gf)],
        out_specs=[spec_p, spec_s],
        out_shape=[jax.ShapeDtypeStruct(x1p.shape, F32), jax.ShapeDtypeStruct(y5s.shape, F32)],
        scratch_shapes=[pltpu.VMEM(w.shape, BF16) for w in (wo, wg, wu, wd)]
                       + [pltpu.VMEM((tm, D_MODEL), F32), pltpu.VMEM((tm_s, D_MODEL), F32)],
        compiler_params=pltpu.CompilerParams(dimension_semantics=("arbitrary",), vmem_limit_bytes=VMEM_LIMIT),
        name="out_ffn",
    )(x1p, y5p, ymp, x1s, y5s, yms, wo, wg, wu, wd, g, gf)


def _lam_bar(lr, li, ldt):
    dt = jnp.exp(ldt)
    mag = jnp.exp(lr * dt)
    th = li * dt
    return mag * jnp.cos(th), mag * jnp.sin(th), dt


def _cmul(ar, ai, br, bi):
    return ar * br - ai * bi, ar * bi + ai * br


def _dot_split(a, b):
    a_hi = a.astype(BF16)
    b_hi = b.astype(BF16)
    a_lo = (a - a_hi.astype(F32)).astype(BF16)
    b_lo = (b - b_hi.astype(F32)).astype(BF16)
    return (jnp.dot(a_hi, b_hi, preferred_element_type=F32) + jnp.dot(a_hi, b_lo, preferred_element_type=F32)
            + jnp.dot(a_lo, b_hi, preferred_element_type=F32))


def _s5_prep_kernel(lre_ref, lim_ref, ldt_ref, bre_ref, bim_ref, lre_t_ref, lim_t_ref, ldt_t_ref, cre_ref, cim_ref,
                    pre_ref, pim_ref, bcat_ref, wout_ref):
    lr = lre_ref[...]
    li = lim_ref[...]
    ar, ai, _ = _lam_bar(lr, li, ldt_ref[...])
    nr = ar - 1.0
    den = lr * lr + li * li
    cr = (nr * lr + ai * li) / den
    ci = (ai * lr - nr * li) / den
    bs = [_cmul(cr, ci, bre_ref[...], bim_ref[...])]
    for _ in range(1, S5_T):
        bs.append(_cmul(ar, ai, *bs[-1]))

    pr, pi = ar, ai
    pre_ref[0:1, :] = pr
    pim_ref[0:1, :] = pi
    for j in range(1, POW_ROWS):
        pr, pi = _cmul(pr, pi, ar, ai)
        pre_ref[j:j + 1, :] = pr
        pim_ref[j:j + 1, :] = pi

    tr, ti, _ = _lam_bar(lre_t_ref[...], lim_t_ref[...], ldt_t_ref[...])
    cs = [(cre_ref[...], cim_ref[...])]
    for _ in range(S5_T):
        cs.append(_cmul(tr, ti, *cs[-1]))

    g_shift, s_shift = S5_GROUP.bit_length() - 1, S5_STATE.bit_length() - 1
    b_shape = (S5_BLOCK_IN, S5_BLOCK_ST)
    b_mask = (lax.broadcasted_iota(jnp.int32, b_shape, 0) >> g_shift) == (
        lax.broadcasted_iota(jnp.int32, b_shape, 1) >> s_shift)
    c_shape = (S5_BLOCK_ST, S5_BLOCK_IN)
    c_mask = (lax.broadcasted_iota(jnp.int32, c_shape, 0) >> s_shift) == (
        lax.broadcasted_iota(jnp.int32, c_shape, 1) >> g_shift)

    def b_block(br, bi, blk):
        parts = []
        for x in (br, bi):
            x = jnp.concatenate([x[:, blk * S5_BLOCK_ST:(blk + 1) * S5_BLOCK_ST]] * S5_BLOCK_GROUPS, axis=0)
            parts.append(jnp.where(b_mask, x, 0.0))
        return jnp.concatenate(parts, axis=1)

    def c_block(xr, xi, blk):
        parts = []
        for x in (xr, -xi):
            x = jnp.concatenate([x[:, blk * S5_BLOCK_IN:(blk + 1) * S5_BLOCK_IN]] * S5_BLOCK_GROUPS, axis=0)
            parts.append(jnp.where(c_mask, x, 0.0))
        return jnp.concatenate(parts, axis=0)

    zero = jnp.zeros((S5_BLOCK_IN, S5_BLOCK_IN), F32)
    z_rows = 2 * S5_BLOCK_ST
    for blk in range(S5_BLOCKS):
        bms = [b_block(br, bi, blk) for br, bi in bs]
        cms = [c_block(xr, xi, blk) for xr, xi in cs]
        kcat = _dot_split(bms[0], jnp.concatenate(cms[:S5_T], axis=1))
        ks = [kcat[:, j * S5_BLOCK_IN:(j + 1) * S5_BLOCK_IN] for j in range(S5_T)]
        for f in range(S5_T):
            bcat_ref[blk, f * S5_BLOCK_IN:(f + 1) * S5_BLOCK_IN, :] = bms[S5_T - 1 - f].astype(BF16)
            row0 = z_rows + f * S5_BLOCK_IN
            wout_ref[blk, row0:row0 + S5_BLOCK_IN, :] = jnp.concatenate(
                [ks[g - f] if g >= f else zero for g in range(S5_T)], axis=1).astype(BF16)
        wout_ref[blk, 0:z_rows, :] = jnp.concatenate(cms[1:], axis=1).astype(BF16)


def _s5_prep(lre, lim, ldt, bre_t, bim_t, lre_t, lim_t, ldt_t, cre_t, cim_t):
    pow_shape = jax.ShapeDtypeStruct((POW_ROWS, S5_LANES), F32)
    return pl.pallas_call(
        _s5_prep_kernel,
        out_shape=[pow_shape, pow_shape,
                   jax.ShapeDtypeStruct((S5_BLOCKS, S5_T * S5_BLOCK_IN, 2 * S5_BLOCK_ST), BF16),
                   jax.ShapeDtypeStruct((S5_BLOCKS, 2 * S5_BLOCK_ST + S5_T * S5_BLOCK_IN, S5_T * S5_BLOCK_IN), BF16)],
        compiler_params=pltpu.CompilerParams(vmem_limit_bytes=VMEM_LIMIT),
        name="s5_prep",
    )(lre, lim, ldt, bre_t, bim_t, lre_t, lim_t, ldt_t, cre_t, cim_t)


def _ml_prep_kernel(wq_ref, wk_ref, wv_ref, oq_ref, ok_ref, ov_ref):
    shape = (MXU_TILE, MXU_TILE)
    shift = ML_QKV_BLOCK.bit_length() - 1
    row = lax.broadcasted_iota(jnp.int32, shape, 0)
    col = lax.broadcasted_iota(jnp.int32, shape, 1)
    same = (row >> shift) == (col >> shift)
    sel = col & (ML_QKV_BLOCK - 1)
    for w_ref, o_ref in ((wq_ref, oq_ref), (wk_ref, ok_ref), (wv_ref, ov_ref)):
        for c in range(D_MODEL // MXU_TILE):
            w = w_ref[c * MXU_TILE:(c + 1) * MXU_TILE, :]
            acc = jnp.zeros(shape, F32)
            for o in range(ML_QKV_BLOCK):
                acc = jnp.where(sel == o, w[:, o:o + 1], acc)
            o_ref[c] = jnp.where(same, acc, 0.0).astype(BF16)


def _ml_prep(wq, wk, wv):
    out = jax.ShapeDtypeStruct((D_MODEL // MXU_TILE, MXU_TILE, MXU_TILE), BF16)
    return pl.pallas_call(_ml_prep_kernel, out_shape=[out] * 3, name="ml_prep")(wq, wk, wv)


def _block_diag_dot(x, w_ref):
    return jnp.concatenate(
        [jnp.dot(x[:, c * MXU_TILE:(c + 1) * MXU_TILE], w_ref[c], preferred_element_type=F32)
         for c in range(w_ref.shape[0])], axis=1)


def _s5_kernel(u_ref, sre0_ref, sim0_ref, pre_ref, pim_ref, bcat_ref, wout_ref, d_ref, gw_ref, gb_ref, on_ref,
               y_ref, sre_ref, sim_ref, s_ref, *, tm, independent, n_tiles, frame_order_io):
    r = tm // SUBLANES
    io_layout = ((0, r),) if frame_order_io else ()
    nc = r // S5_T
    rows_c = nc * SUBLANES
    t = pl.program_id(1)

    @pl.when(t == 0)
    def _():
        sre_ref[...] = jnp.broadcast_to(sre0_ref[...], sre_ref.shape)
        sim_ref[...] = jnp.broadcast_to(sim0_ref[...], sim_ref.shape)

    ups = [_segment_rows(u_ref[g], io_layout) for g in range(n_tiles)]
    us = []
    for f in range(S5_T):
        us.append(jnp.concatenate(
            [up.reshape(nc, S5_T * SUBLANES, D_MODEL)[:, f * SUBLANES:(f + 1) * SUBLANES, :].reshape(rows_c, D_MODEL)
             for up in ups], axis=0).astype(BF16))

    ys = [[] for _ in range(S5_T)]
    for b in range(S5_BLOCKS):
        lanes = slice(b * S5_BLOCK_ST, (b + 1) * S5_BLOCK_ST)
        chans = slice(b * S5_BLOCK_IN, (b + 1) * S5_BLOCK_IN)
        re = slice(0, S5_BLOCK_ST)
        im = slice(S5_BLOCK_ST, 2 * S5_BLOCK_ST)
        ucat = jnp.concatenate([u[:, chans] for u in us], axis=1)
        s_ref[...] = jnp.dot(ucat, bcat_ref[b], preferred_element_type=F32)
        ltr = jnp.broadcast_to(pre_ref[S5_T - 1:S5_T, lanes], (SUBLANES, S5_BLOCK_ST))
        lti = jnp.broadcast_to(pim_ref[S5_T - 1:S5_T, lanes], (SUBLANES, S5_BLOCK_ST))

        def scan(cr, ci, row0):
            for c in range(nc):
                rows = slice(row0 + c * SUBLANES, row0 + (c + 1) * SUBLANES)
                nr = ltr * cr - lti * ci + s_ref[rows, re]
                ni = ltr * ci + lti * cr + s_ref[rows, im]
                s_ref[rows, re] = cr
                s_ref[rows, im] = ci
                cr, ci = nr, ni
            return cr, ci

        for g in range(n_tiles):
            row0 = g * rows_c
            if independent:
                fr, fi = scan(sre_ref[g, :, lanes], sim_ref[g, :, lanes], row0)
                sre_ref[g, :, lanes] = fr
                sim_ref[g, :, lanes] = fi
            else:
                zero = jnp.zeros((SUBLANES, S5_BLOCK_ST), F32)
                fr, fi = scan(zero, zero, row0)
                rr = pre_ref[r - 1:r, lanes]
                ri = pim_ref[r - 1:r, lanes]
                cr = sre_ref[g, :, lanes]
                ci = sim_ref[g, :, lanes]
                rows_r, rows_i = [], []
                for a in range(SUBLANES):
                    rows_r.append(cr)
                    rows_i.append(ci)
                    cr, ci = rr * cr - ri * ci + fr[a:a + 1], rr * ci + ri * cr + fi[a:a + 1]
                sre_ref[g, :, lanes] = cr
                sim_ref[g, :, lanes] = ci
                cin_r = jnp.concatenate(rows_r, axis=0)
                cin_i = jnp.concatenate(rows_i, axis=0)
                s_ref[row0:row0 + SUBLANES, re] = cin_r
                s_ref[row0:row0 + SUBLANES, im] = cin_i
                for c in range(1, nc):
                    rows = slice(row0 + c * SUBLANES, row0 + (c + 1) * SUBLANES)
                    pr = pre_ref[S5_T * c - 1:S5_T * c, lanes]
                    pi = pim_ref[S5_T * c - 1:S5_T * c, lanes]
                    s_ref[rows, re] += pr * cin_r - pi * cin_i
                    s_ref[rows, im] += pr * cin_i + pi * cin_r
        yb = jnp.dot(jnp.concatenate([s_ref[...].astype(BF16), ucat], axis=1), wout_ref[b],
                     preferred_element_type=F32)
        for f in range(S5_T):
            ys[f].append(yb[:, f * S5_BLOCK_IN:(f + 1) * S5_BLOCK_IN])

    yf = [jnp.concatenate(ys[f], axis=1) for f in range(S5_T)]
    for g in range(n_tiles):
        rows = slice(g * rows_c, (g + 1) * rows_c)
        y = jnp.concatenate([y[rows].reshape(nc, SUBLANES, D_MODEL) for y in yf], axis=1).reshape(tm, D_MODEL)
        y = y + d_ref[...] * ups[g]
        gl = jax.nn.gelu(y)
        o = gl * jax.nn.sigmoid(jnp.dot(gl.astype(BF16), gw_ref[...], preferred_element_type=F32) + gb_ref[...])
        y_ref[g] = _segment_rows(_rms(o, on_ref[...]), io_layout, inverse=True).astype(BF16)


def _s5(u, sre0, sim0, consts, *, n_outer, n_tiles, steps, tm, independent=False, shared_init=False,
        frame_order_io=False):
    srows = SUBLANES if independent else 1
    u5 = u.reshape(n_outer, n_tiles, steps, tm, D_MODEL)
    seq_spec = pl.BlockSpec((None, n_tiles, None, tm, D_MODEL), lambda o, t: (o, 0, t, 0, 0))
    if shared_init:
        st_in = pl.BlockSpec((None, 1, srows, S5_LANES), lambda o, t: (0, 0, 0, 0))
    else:
        st_in = pl.BlockSpec((None, n_tiles, srows, S5_LANES), lambda o, t: (o, 0, 0, 0))
    st_out = pl.BlockSpec((None, n_tiles, srows, S5_LANES), lambda o, t: (o, 0, 0, 0))
    st_shape = jax.ShapeDtypeStruct((n_outer, n_tiles, srows, S5_LANES), F32)
    y, sre, sim = pl.pallas_call(
        functools.partial(_s5_kernel, tm=tm, independent=independent, n_tiles=n_tiles,
                          frame_order_io=frame_order_io),
        grid=(n_outer, steps),
        in_specs=[seq_spec, st_in, st_in] + [_const_spec(a) for a in consts],
        out_specs=[seq_spec, st_out, st_out],
        out_shape=[jax.ShapeDtypeStruct(u5.shape, BF16), st_shape, st_shape],
        scratch_shapes=[pltpu.VMEM((n_tiles * tm // S5_T, 2 * S5_BLOCK_ST), F32)],
        compiler_params=pltpu.CompilerParams(dimension_semantics=("parallel", "arbitrary"),
                                             vmem_limit_bytes=VMEM_LIMIT),
        name="s5_mix",
    )(u5, sre0, sim0, *consts)
    return y.reshape(u.shape), sre, sim


def _ml_front(xm_ref, cv_ref, xbuf_ref, consts, bufs, tm, n_tiles):
    cw_ref, cb_ref, bdq_ref, bdk_ref, bdv_ref, gw_ref, gb_ref = consts
    xc_ref, q_ref, k_ref, v_ref, vf_ref, g_ref = bufs
    pre = ML_CONV - 1
    cw = cw_ref[...]
    xms, xcs = [], []
    for g in range(n_tiles):
        xm = xm_ref[g]
        xbuf_ref[g, SUBLANES - pre:SUBLANES, :] = cv_ref[g]
        xbuf_ref[g, SUBLANES:SUBLANES + tm, :] = xm
        xc = cb_ref[...] + cw[pre:pre + 1] * xm
        for j in range(pre):
            xc = xc + cw[j:j + 1] * xbuf_ref[g, SUBLANES - pre + j:SUBLANES - pre + j + tm, :]
        cv_ref[g] = xbuf_ref[g, SUBLANES + tm - pre:SUBLANES + tm, :]
        xms.append(xm)
        xcs.append(_silu(xc))
    xm = xms[0] if n_tiles == 1 else jnp.concatenate(xms, axis=0)
    xc = xcs[0] if n_tiles == 1 else jnp.concatenate(xcs, axis=0)
    xc_ref[...] = xc

    xcb = xc.astype(BF16)
    q = _block_diag_dot(xcb, bdq_ref)
    k = _block_diag_dot(xcb, bdk_ref)
    v = _block_diag_dot(xm.astype(BF16), bdv_ref)
    qb = q.astype(BF16)
    vb = v.astype(BF16)
    g_ref[...] = jnp.dot(jnp.concatenate([qb, k.astype(BF16), vb], axis=1), gw_ref[...],
                         preferred_element_type=F32) + gb_ref[...]
    q_ref[...] = qb
    k_ref[...] = (k * (ML_HEAD_DIM ** -0.5)).astype(BF16)
    v_ref[...] = vb
    vf_ref[...] = v


def _ml_back(bufs, z_ref, state, consts, y_ref, h_ref, tm, tc, n_tiles):
    xc_ref, q_ref, k_ref, v_ref, vf_ref, g_ref = bufs
    c_ref, n_ref, m_ref = state
    nw_ref, sk_ref, on_ref = consts
    lane = lax.broadcasted_iota(jnp.int32, (tc, GATE_LANES), 1)
    row = lax.broadcasted_iota(jnp.int32, (tc, tc), 0)
    col = lax.broadcasted_iota(jnp.int32, (tc, tc), 1)
    causal = row >= col
    tril = causal.astype(F32)

    def chunk(rows, g):
        gates = g_ref[rows, :]
        lf = jnp.minimum(gates, 0.0) - jnp.log1p(jnp.exp(-jnp.abs(gates)))
        lf = jnp.where((lane >= ML_HEADS) & (lane < 2 * ML_HEADS), lf, 0.0)
        cum = jnp.dot(tril, lf, preferred_element_type=F32, precision=lax.Precision.HIGHEST)
        arr = jnp.where(lane < ML_HEADS, gates, cum)
        arr_t = arr.T
        for h in range(ML_HEADS):
            hs = slice(h * ML_HEAD_DIM, (h + 1) * ML_HEAD_DIM)
            qh = q_ref[rows, hs]
            kh = k_ref[rows, hs]
            vh = v_ref[rows, hs]
            ig_col = arr[:, h:h + 1]
            b_col = arr[:, ML_HEADS + h:ML_HEADS + h + 1]
            ig_row = arr_t[h:h + 1, :]
            b_row = arr_t[ML_HEADS + h:ML_HEADS + h + 1, :]
            m_prev = m_ref[g, :, h:h + 1]
            c_prev = c_ref[g, h]
            n_prev = n_ref[g, h:h + 1, :]

            logw = jnp.where(causal, b_col - b_row + ig_row, -jnp.inf)
            log_inter = b_col + m_prev
            m_t = jnp.maximum(log_inter, jnp.max(logw, axis=-1, keepdims=True))
            w = jnp.exp(logw - m_t)
            a_inter = jnp.exp(log_inter - m_t)
            s = lax.dot_general(qh, kh, (((1,), (1,)), ((), ())), preferred_element_type=F32) * w
            inter = lax.dot_general(qh, c_prev.astype(BF16), (((1,), (1,)), ((), ())),
                                    preferred_element_type=F32)
            num = a_inter * inter + jnp.dot(s.astype(BF16), vh, preferred_element_type=F32)
            den = (a_inter * jnp.sum(qh.astype(F32) * n_prev, axis=-1, keepdims=True)
                   + jnp.sum(s, axis=-1, keepdims=True))
            hh = num / jnp.maximum(jnp.abs(den), jnp.exp(-m_t))
            mu = jnp.mean(hh, axis=-1, keepdims=True)
            hc = hh - mu
            var = jnp.mean(hc * hc, axis=-1, keepdims=True)
            h_ref[rows, hs] = hc * lax.rsqrt(var + EPS)

            b_last = b_col[tc - 1:tc, :]
            m_new = m_t[tc - 1:tc, :]
            g_state = jnp.exp(b_last + m_prev - m_new)
            g_src = jnp.exp(b_last - b_col + ig_col - m_new)
            vs = (vf_ref[rows, hs] * g_src).astype(BF16)
            c_ref[g, h] = g_state * c_prev + lax.dot_general(vs, kh, (((0,), (0,)), ((), ())),
                                                             preferred_element_type=F32)
            n_ref[g, h:h + 1, :] = g_state * n_prev + jnp.sum(g_src * kh.astype(F32), axis=0, keepdims=True)
            m_ref[g, :, h:h + 1] = m_new

    for j in range(tm // tc):
        for g in range(n_tiles):
            chunk(slice(g * tm + j * tc, g * tm + (j + 1) * tc), g)

    for g in range(n_tiles):
        rows = slice(g * tm, (g + 1) * tm)
        out = (h_ref[rows, :] * nw_ref[...] + sk_ref[...] * xc_ref[rows, :]) * _silu(z_ref[g])
        y_ref[g] = _rms(out, on_ref[...]).astype(BF16)


def _mlstm_kernel(xm_ref, z_ref, c0_ref, n0_ref, m0_ref, cv0_ref, cw_ref, cb_ref, bdq_ref, bdk_ref, bdv_ref,
                  gw_ref, gb_ref, nw_ref, sk_ref, on_ref,
                  y_ref, c_ref, n_ref, m_ref, cv_ref, xbuf_ref, h_ref, *bufs, tm, tc, n_tiles, pipelined):
    t = pl.program_id(1)
    front_consts = (cw_ref, cb_ref, bdq_ref, bdk_ref, bdv_ref, gw_ref, gb_ref)
    back_consts = (nw_ref, sk_ref, on_ref)
    state = (c_ref, n_ref, m_ref)
    n_buf = len(bufs) // 2 if pipelined else len(bufs)

    @pl.when(t == 0)
    def _():
        cv_ref[...] = jnp.broadcast_to(cv0_ref[...], cv_ref.shape)
        if pipelined:
            for ref in bufs[n_buf:]:
                ref[...] = jnp.zeros(ref.shape, ref.dtype)

    @pl.when(t <= (1 if pipelined else 0))
    def _():
        c_ref[...] = jnp.broadcast_to(c0_ref[...], c_ref.shape)
        n_ref[...] = jnp.broadcast_to(n0_ref[...], n_ref.shape)
        m_ref[...] = jnp.broadcast_to(m0_ref[...], m_ref.shape)

    def step(front_set, back_set):
        _ml_front(xm_ref, cv_ref, xbuf_ref, front_consts, front_set, tm, n_tiles)
        _ml_back(back_set, z_ref, state, back_consts, y_ref, h_ref, tm, tc, n_tiles)

    if pipelined:
        pl.when(t % 2 == 0)(lambda: step(bufs[:n_buf], bufs[n_buf:]))
        pl.when(t % 2 == 1)(lambda: step(bufs[n_buf:], bufs[:n_buf]))
    else:
        step(bufs, bufs)


def _mlstm(xm, z, c0, n0, m0, cv0, consts, *, n_outer, n_tiles, steps, tm, tc, shared_init=False, pipelined=False):
    shape5 = (n_outer, n_tiles, steps, tm, D_MODEL)
    block = (None, n_tiles, None, tm, D_MODEL)
    if pipelined:
        grid_steps = steps + 1
        front_spec = pl.BlockSpec(block, lambda o, t: (o, 0, jnp.minimum(t, steps - 1), 0, 0))
        back_spec = pl.BlockSpec(block, lambda o, t: (o, 0, jnp.maximum(t - 1, 0), 0, 0))
    else:
        grid_steps = steps
        front_spec = back_spec = pl.BlockSpec(block, lambda o, t: (o, 0, t, 0, 0))

    def st(shape, shared):
        zeros = (0,) * len(shape)
        if shared:
            return pl.BlockSpec((None, 1) + shape, lambda o, t: (0, 0) + zeros)
        return pl.BlockSpec((None, n_tiles) + shape, lambda o, t: (o, 0) + zeros)

    st_shapes = ((ML_HEADS, ML_HEAD_DIM, ML_HEAD_DIM), (ML_HEADS, ML_HEAD_DIM), (1, ML_HEADS),
                 (ML_CONV - 1, D_MODEL))
    rows = n_tiles * tm
    buf_set = [pltpu.VMEM((rows, D_MODEL), F32), pltpu.VMEM((rows, D_MODEL), BF16), pltpu.VMEM((rows, D_MODEL), BF16),
               pltpu.VMEM((rows, D_MODEL), BF16), pltpu.VMEM((rows, D_MODEL), F32), pltpu.VMEM((rows, GATE_LANES), F32)]
    outs = pl.pallas_call(
        functools.partial(_mlstm_kernel, tm=tm, tc=tc, n_tiles=n_tiles, pipelined=pipelined),
        grid=(n_outer, grid_steps),
        in_specs=[front_spec, back_spec] + [st(s, shared_init) for s in st_shapes]
                 + [_const_spec(a) for a in consts],
        out_specs=[back_spec] + [st(s, False) for s in st_shapes],
        out_shape=[jax.ShapeDtypeStruct(shape5, BF16)]
                  + [jax.ShapeDtypeStruct((n_outer, n_tiles) + s, F32) for s in st_shapes],
        scratch_shapes=[pltpu.VMEM((n_tiles, tm + 2 * SUBLANES, D_MODEL), F32), pltpu.VMEM((rows, D_MODEL), F32)]
                       + buf_set * (2 if pipelined else 1),
        compiler_params=pltpu.CompilerParams(dimension_semantics=("parallel", "arbitrary"),
                                             vmem_limit_bytes=VMEM_LIMIT),
        name="mlstm_mix",
    )(xm.reshape(shape5), z.reshape(shape5), c0, n0, m0, cv0, *consts)
    n_streams = n_outer * n_tiles
    return (outs[0].reshape(xm.shape),) + tuple(o.reshape((n_streams,) + o.shape[2:]) for o in outs[1:])


def _row(v):
    return v.reshape(1, -1).astype(F32)


def _pad_lanes(v, width):
    return jnp.pad(v, [(0, 0)] * (v.ndim - 1) + [(0, width - v.shape[-1])])


def kernel(x_prompt, x_sample, state_s5_re, state_s5_im, state_mlstm_c, state_mlstm_n, state_mlstm_m,
           state_mlstm_conv, meta_tokens, norm_ffn1, ffn1_gate, ffn1_up, ffn1_down, norm_mix, w_in,
           s5_lambda_re, s5_lambda_im, s5_log_dt, s5_b_re, s5_b_im, s5_c_re, s5_c_im, s5_d, s5_glu_w, s5_glu_b,
           ml_conv_w, ml_conv_b, ml_wq, ml_wk, ml_wv, ml_igate_w, ml_igate_b, ml_fgate_w, ml_fgate_b,
           ml_norm_w, ml_skip, out_norm_s5, out_norm_ml, w_out, norm_ffn2, ffn2_gate, ffn2_up, ffn2_down,
           norm_final):
    nb, seq, _ = x_prompt.shape
    ns, dseq, _ = x_sample.shape
    n_meta = meta_tokens.shape[0]
    n_p, n_s = nb * seq, ns * dseq
    tile_p = min(TOKEN_TILE, seq)

    glu_w = s5_glu_w[0].astype(BF16)

    ldt_gs = jnp.broadcast_to(s5_log_dt[0][:, None], (S5_GROUPS, S5_STATE))

    def state_major(a):
        return jnp.repeat(a.T, S5_GROUP, axis=1)

    pre, pim, bcat, wout = _s5_prep(
        s5_lambda_re[0].reshape(1, S5_LANES), s5_lambda_im[0].reshape(1, S5_LANES), ldt_gs.reshape(1, S5_LANES),
        s5_b_re[0].reshape(S5_LANES, S5_GROUP).T, s5_b_im[0].reshape(S5_LANES, S5_GROUP).T,
        state_major(s5_lambda_re[0]), state_major(s5_lambda_im[0]), state_major(ldt_gs),
        s5_c_re[0].reshape(D_MODEL, S5_STATE).T, s5_c_im[0].reshape(D_MODEL, S5_STATE).T)
    s5_consts = (pre, pim, bcat, wout, _row(s5_d[0]), glu_w, _row(s5_glu_b[0]), _row(out_norm_s5[0]))

    bdq, bdk, bdv = _ml_prep(ml_wq[0].reshape(D_MODEL, ML_QKV_BLOCK), ml_wk[0].reshape(D_MODEL, ML_QKV_BLOCK),
                             ml_wv[0].reshape(D_MODEL, ML_QKV_BLOCK))
    gate_w = _pad_lanes(jnp.concatenate([ml_igate_w[0], ml_fgate_w[0]], axis=1), GATE_LANES).astype(BF16)
    gate_b = _pad_lanes(jnp.concatenate([ml_igate_b[0], ml_fgate_b[0]])[None, :], GATE_LANES)
    ml_consts = (ml_conv_w[0], _row(ml_conv_b[0]), bdq, bdk, bdv, gate_w, gate_b, _row(ml_norm_w[0]),
                 _row(ml_skip[0]), _row(out_norm_ml[0]))

    seg_p = ((0, tile_p // SUBLANES),)
    pad_m = max(SUBLANES * S5_T - n_meta, 0)
    tm_m = n_meta + pad_m
    n_small = -(-(n_s + tm_m) // SMALL_TILE_ALIGN) * SMALL_TILE_ALIGN
    small = jnp.concatenate([x_sample.reshape(n_s, D_MODEL), jnp.zeros((pad_m, D_MODEL), F32), meta_tokens,
                             jnp.zeros((n_small - n_s - tm_m, D_MODEL), F32)], axis=0)
    g_ffn1, g_mix = _row(norm_ffn1[0]), _row(norm_mix[0])
    (x1_p, u_p, xm_p, z_p), (wg1, wu1, wd1, win) = _ffn_in_cast(
        x_prompt.reshape(n_p, D_MODEL), g_ffn1, ffn1_gate[0], ffn1_up[0], ffn1_down[0], g_mix, w_in[0],
        tm=tile_p, seg_layout=seg_p)
    x1_s, u_s, xm_s, z_s = _ffn_in(small, g_ffn1, wg1, wu1, wd1, g_mix, win, tm=n_small, seg_layout=())

    zs5 = jnp.zeros((1, 1, 1, S5_LANES), F32)
    _, mre, mim = _s5(u_s[n_s:n_s + tm_m], zs5, zs5, s5_consts, n_outer=1, n_tiles=1, steps=1, tm=tm_m,
                      frame_order_io=True)
    y5_p, pre_s, pim_s = _s5(u_p, mre, mim, s5_consts, n_outer=1, n_tiles=nb, steps=seq // tile_p, tm=tile_p,
                             shared_init=True)
    n_grp = ns // SUBLANES
    y5_s, sre_s, sim_s = _s5(u_s[:n_s], state_s5_re[0].reshape(1, n_grp, SUBLANES, S5_LANES),
                             state_s5_im[0].reshape(1, n_grp, SUBLANES, S5_LANES), s5_consts,
                             n_outer=1, n_tiles=n_grp, steps=1, tm=SUBLANES * dseq, independent=True,
                             frame_order_io=True)

    ml_tile_p = min(ML_TILE, seq)
    zc = jnp.zeros((1, 1, ML_HEADS, ML_HEAD_DIM, ML_HEAD_DIM), F32)
    zn = jnp.zeros((1, 1, ML_HEADS, ML_HEAD_DIM), F32)
    zm = jnp.zeros((1, 1, 1, ML_HEADS), F32)
    zcv = jnp.zeros((1, 1, ML_CONV - 1, D_MODEL), F32)
    m_rows = slice(n_s + pad_m, n_s + pad_m + n_meta)
    _, c_m, n_m, m_m, cv_m = _mlstm(xm_s[m_rows], z_s[m_rows], zc, zn, zm, zcv, ml_consts, n_outer=1, n_tiles=1,
                                    steps=1, tm=n_meta, tc=n_meta)
    ym_p, c_p, nn_p, m_p, cv_p = _mlstm(xm_p, z_p, c_m[None], n_m[None], m_m[None], cv_m[None], ml_consts,
                                        n_outer=1, n_tiles=nb, steps=seq // ml_tile_p, tm=ml_tile_p,
                                        tc=min(ML_CHUNK, ml_tile_p), shared_init=True, pipelined=True)
    grp = ML_SAMPLE_GROUP
    st5 = lambda a: a.reshape((ns // grp, grp) + a.shape[1:])
    ym_s, c_s, nn_s, m_s, cv_s = _mlstm(xm_s[:n_s], z_s[:n_s], st5(state_mlstm_c[0]), st5(state_mlstm_n[0]),
                                        st5(state_mlstm_m[0][:, None, :]), st5(state_mlstm_conv[0]), ml_consts,
                                        n_outer=ns // grp, n_tiles=grp, steps=1, tm=dseq, tc=dseq)

    y_p, y_s = _out_ffn(x1_p, y5_p, ym_p, x1_s, y5_s, ym_s, w_out[0], _row(norm_ffn2[0]), ffn2_gate[0], ffn2_up[0],
                        ffn2_down[0], _row(norm_final), tm=tile_p, tm_s=n_s // 2, seg_layout=seg_p)

    def s5_state(s, n):
        return s.reshape(1, n, S5_GROUPS, S5_STATE)

    return (y_p.reshape(nb, seq, D_MODEL), y_s.reshape(ns, dseq, D_MODEL),
            s5_state(pre_s, nb), s5_state(pim_s, nb), c_p[None], nn_p[None], m_p[:, 0][None], cv_p[None],
            s5_state(sre_s, ns), s5_state(sim_s, ns), c_s[None], nn_s[None], m_s[:, 0][None], cv_s[None])
```

```python
import functools

import jax
import jax.numpy as jnp
from jax import lax
from jax.experimental import pallas as pl
from jax.experimental.pallas import tpu as pltpu

F32 = jnp.float32
BF16 = jnp.bfloat16

D_MODEL = 1024
D_FF = 2816
S5_GROUPS = 64
S5_GROUP = 16
S5_STATE = 64
S5_LANES = S5_GROUPS * S5_STATE
MXU_TILE = 256
S5_BLOCKS = 8
S5_BLOCK_GROUPS = S5_GROUPS // S5_BLOCKS
S5_BLOCK_IN = D_MODEL // S5_BLOCKS
S5_BLOCK_ST = S5_LANES // S5_BLOCKS
ML_HEADS = 4
ML_HEAD_DIM = 256
ML_CONV = 4
ML_QKV_BLOCK = 4
EPS = 1e-6
SUBLANES = 8
GATE_LANES = 128
POW_ROWS = 64
S5_T = 2
VMEM_LIMIT = 56 * 1024 * 1024
SINGLE_BUFFER_BYTES = 2 * 1024 * 1024

TOKEN_TILE = 512
CAST_STEPS = 16
ML_CHUNK = 256
ML_TILE = 256
FFN_TAIL_ROWS = 64
S5_TAIL_ROWS = 32
ML_TAIL_ROWS = 32
ML_SAMPLE_GROUP = 4
SMALL_TILE_ALIGN = 16


def _rms(x, g):
    return x * lax.rsqrt(jnp.mean(x * x, axis=-1, keepdims=True) + EPS) * g


def _silu(x):
    h = 0.5 * x
    return h + h * jnp.tanh(h)


def _swiglu(h, wg_ref, wu_ref, wd_ref, acc_ref, ff_chunk):
    for c in range(D_FF // ff_chunk):
        sl = slice(c * ff_chunk, (c + 1) * ff_chunk)
        g = jnp.dot(h, wg_ref[:, sl], preferred_element_type=F32)
        u = jnp.dot(h, wu_ref[:, sl], preferred_element_type=F32)
        a = (_silu(g) * u).astype(BF16)
        d = jnp.dot(a, wd_ref[sl, :], preferred_element_type=F32)
        if c == 0:
            acc_ref[...] = d
        else:
            acc_ref[...] += d
    return acc_ref[...]


def _const_spec(a):
    nd = a.ndim
    if a.size * a.dtype.itemsize >= SINGLE_BUFFER_BYTES:
        return pl.BlockSpec(a.shape, lambda *_: (0,) * nd, pipeline_mode=pl.Buffered(1))
    return pl.BlockSpec(a.shape, lambda *_: (0,) * nd)


def _segment_rows(x, layout, inverse=False):
    if not layout:
        return x
    parts = []
    for row0, r in layout:
        n = SUBLANES * r
        g = x[row0:row0 + n]
        if r % SUBLANES == 0:
            shape = (r, SUBLANES) if inverse else (SUBLANES, r)
            g = jnp.swapaxes(g.reshape(shape + g.shape[1:]), 0, 1).reshape(g.shape)
        elif inverse:
            g = jnp.concatenate([g[i * SUBLANES + a:i * SUBLANES + a + 1]
                                 for a in range(SUBLANES) for i in range(r)], axis=0)
        else:
            g = jnp.concatenate([g[a * r + i:a * r + i + 1]
                                 for i in range(r) for a in range(SUBLANES)], axis=0)
        parts.append(g)
    return parts[0] if len(parts) == 1 else jnp.concatenate(parts, axis=0)


def _ffn_in_tile(x, consts, acc_ref, ff_chunk, seg_layout):
    g1_ref, wg_ref, wu_ref, wd_ref, g2_ref, win_ref = consts
    h = _rms(x, g1_ref[...]).astype(BF16)
    x1 = x + 0.5 * _swiglu(h, wg_ref, wu_ref, wd_ref, acc_ref, ff_chunk)
    h2f = _rms(x1, g2_ref[...])
    h2 = h2f.astype(BF16)
    h2s = _segment_rows(h2f, seg_layout).astype(BF16)
    u = jnp.dot(h2s, win_ref[:, 0:D_MODEL], preferred_element_type=F32)
    xm = jnp.dot(h2, win_ref[:, D_MODEL:2 * D_MODEL], preferred_element_type=F32)
    z = jnp.dot(h2, win_ref[:, 2 * D_MODEL:3 * D_MODEL], preferred_element_type=F32)
    return x1, u, xm, z


def _ffn_in_kernel(x_ref, *refs, ff_chunk, seg_layout):
    consts, outs, acc_ref = refs[:6], refs[6:10], refs[10]
    for o_ref, v in zip(outs, _ffn_in_tile(x_ref[...], consts, acc_ref, ff_chunk, seg_layout)):
        o_ref[...] = v


def _ffn_in(x, g1, wg, wu, wd, g2, win, *, tm, seg_layout, ff_chunk=256):
    rows = x.shape[0]
    row_spec = pl.BlockSpec((tm, D_MODEL), lambda i: (i, 0))
    consts = (g1, wg, wu, wd, g2, win)
    return pl.pallas_call(
        functools.partial(_ffn_in_kernel, ff_chunk=ff_chunk, seg_layout=seg_layout),
        grid=(rows // tm,),
        in_specs=[row_spec] + [_const_spec(a) for a in consts],
        out_specs=[row_spec] * 4,
        out_shape=[jax.ShapeDtypeStruct(x.shape, F32)] * 4,
        scratch_shapes=[pltpu.VMEM((tm, D_MODEL), F32)],
        compiler_params=pltpu.CompilerParams(dimension_semantics=("parallel",), vmem_limit_bytes=VMEM_LIMIT),
        name="ffn_in",
    )(x, *consts)


def _ffn_in_cast_kernel(x_ref, wg_ref, wu_ref, wd_ref, win_ref, g1_ref, g2_ref,
                        x1_ref, u_ref, xm_ref, z_ref, wg_o_ref, wu_o_ref, wd_o_ref, win_o_ref,
                        wg_b_ref, wu_b_ref, wd_b_ref, win_b_ref, acc_ref, *, ff_chunk, seg_layout):
    i = pl.program_id(0)

    @pl.when(i < CAST_STEPS)
    def _():
        for src_ref, dst_ref, out_ref in ((wg_ref, wg_b_ref, wg_o_ref), (wu_ref, wu_b_ref, wu_o_ref),
                                          (wd_ref, wd_b_ref, wd_o_ref), (win_ref, win_b_ref, win_o_ref)):
            rows = src_ref.shape[0]
            w = src_ref[...].astype(BF16)
            dst_ref[pl.ds(pl.multiple_of(i * rows, rows), rows), :] = w
            out_ref[...] = w

    @pl.when(i >= CAST_STEPS)
    def _():
        consts = (g1_ref, wg_b_ref, wu_b_ref, wd_b_ref, g2_ref, win_b_ref)
        for o_ref, v in zip((x1_ref, u_ref, xm_ref, z_ref),
                            _ffn_in_tile(x_ref[...], consts, acc_ref, ff_chunk, seg_layout)):
            o_ref[...] = v


def _ffn_in_cast(x, g1, wg, wu, wd, g2, win, *, tm, seg_layout, ff_chunk=256):
    steps = x.shape[0] // tm
    row_spec = pl.BlockSpec((tm, D_MODEL), lambda i: (jnp.clip(i - CAST_STEPS, 0, steps - 1), 0))
    weights = (wg, wu, wd, win)

    def chunk_spec(w):
        return pl.BlockSpec((w.shape[0] // CAST_STEPS, w.shape[1]), lambda i: (jnp.minimum(i, CAST_STEPS - 1), 0))

    outs = pl.pallas_call(
        functools.partial(_ffn_in_cast_kernel, ff_chunk=ff_chunk, seg_layout=seg_layout),
        grid=(CAST_STEPS + steps,),
        in_specs=[row_spec] + [chunk_spec(w) for w in weights] + [_const_spec(g1), _const_spec(g2)],
        out_specs=[row_spec] * 4 + [chunk_spec(w) for w in weights],
        out_shape=[jax.ShapeDtypeStruct(x.shape, F32)] * 4 + [jax.ShapeDtypeStruct(w.shape, BF16) for w in weights],
        scratch_shapes=[pltpu.VMEM(w.shape, BF16) for w in weights] + [pltpu.VMEM((tm, D_MODEL), F32)],
        compiler_params=pltpu.CompilerParams(dimension_semantics=("arbitrary",), vmem_limit_bytes=VMEM_LIMIT),
        name="ffn_in",
    )(x, *weights, g1, g2)
    return outs[:4], outs[4:]


def _out_ffn_tile(x1, y5, ym, consts, acc_ref, ff_chunk, seg_layout):
    wo_ref, g_ref, wg_ref, wu_ref, wd_ref, gf_ref = consts
    p5 = jnp.dot(y5, wo_ref[0:D_MODEL, :], preferred_element_type=F32)
    x2 = (x1 + _segment_rows(p5, seg_layout, inverse=True)
          + jnp.dot(ym, wo_ref[D_MODEL:2 * D_MODEL, :], preferred_element_type=F32))
    h = _rms(x2, g_ref[...]).astype(BF16)
    _swiglu(h, wg_ref, wu_ref, wd_ref, acc_ref, ff_chunk)
    n = x2.shape[0]
    rb = FFN_TAIL_ROWS if n % FFN_TAIL_ROWS == 0 else n
    parts = [_rms(x2[r0:r0 + rb] + 0.5 * acc_ref[r0:r0 + rb, :], gf_ref[...]) for r0 in range(0, n, rb)]
    return parts[0] if len(parts) == 1 else jnp.concatenate(parts, axis=0)


def _out_ffn_kernel(x1p_ref, y5p_ref, ymp_ref, x1s_ref, y5s_ref, yms_ref, wo_ref, wg_ref, wu_ref, wd_ref, g_ref, gf_ref,
                    op_ref, os_ref, wo_b_ref, wg_b_ref, wu_b_ref, wd_b_ref, acc_p_ref, acc_s_ref, *,
                    ff_chunk, seg_layout, steps_p):
    i = pl.program_id(0)
    consts = (wo_b_ref, g_ref, wg_b_ref, wu_b_ref, wd_b_ref, gf_ref)

    @pl.when(i < CAST_STEPS)
    def _():
        for src_ref, dst_ref in ((wo_ref, wo_b_ref), (wg_ref, wg_b_ref), (wu_ref, wu_b_ref), (wd_ref, wd_b_ref)):
            rows = src_ref.shape[0]
            dst_ref[pl.ds(pl.multiple_of(i * rows, rows), rows), :] = src_ref[...].astype(BF16)

    @pl.when((i >= CAST_STEPS) & (i < CAST_STEPS + steps_p))
    def _():
        op_ref[...] = _out_ffn_tile(x1p_ref[...], y5p_ref[...], ymp_ref[...], consts, acc_p_ref, ff_chunk, seg_layout)

    @pl.when(i >= CAST_STEPS + steps_p)
    def _():
        os_ref[...] = _out_ffn_tile(x1s_ref[...], y5s_ref[...], yms_ref[...], consts, acc_s_ref, ff_chunk, ())


def _out_ffn(x1p, y5p, ymp, x1s, y5s, yms, wo, g, wg, wu, wd, gf, *, tm, tm_s, seg_layout, ff_chunk=256):
    steps_p, steps_s = x1p.shape[0] // tm, y5s.shape[0] // tm_s
    spec_p = pl.BlockSpec((tm, D_MODEL), lambda i: (jnp.clip(i - CAST_STEPS, 0, steps_p - 1), 0))
    spec_s = pl.BlockSpec((tm_s, D_MODEL), lambda i: (jnp.clip(i - CAST_STEPS - steps_p, 0, steps_s - 1), 0))

    def chunk_spec(w):
        return pl.BlockSpec((w.shape[0] // CAST_STEPS, w.shape[1]), lambda i: (jnp.minimum(i, CAST_STEPS - 1), 0))

    return pl.pallas_call(
        functools.partial(_out_ffn_kernel, ff_chunk=ff_chunk, seg_layout=seg_layout, steps_p=steps_p),
        grid=(CAST_STEPS + steps_p + steps_s,),
        in_specs=[spec_p] * 3 + [spec_s] * 3 + [chunk_spec(w) for w in (wo, wg, wu, wd)]
                 + [_const_spec(g), _const_spec(gf)],
        out_specs=[spec_p, spec_s],
        out_shape=[jax.ShapeDtypeStruct(x1p.shape, F32), jax.ShapeDtypeStruct(y5s.shape, F32)],
        scratch_shapes=[pltpu.VMEM(w.shape, BF16) for w in (wo, wg, wu, wd)]
                       + [pltpu.VMEM((tm, D_MODEL), F32), pltpu.VMEM((tm_s, D_MODEL), F32)],
        compiler_params=pltpu.CompilerParams(dimension_semantics=("arbitrary",), vmem_limit_bytes=VMEM_LIMIT),
        name="out_ffn",
    )(x1p, y5p, ymp, x1s, y5s, yms, wo, wg, wu, wd, g, gf)


def _lam_bar(lr, li, ldt):
    dt = jnp.exp(ldt)
    mag = jnp.exp(lr * dt)
    th = li * dt
    return mag * jnp.cos(th), mag * jnp.sin(th), dt


def _cmul(ar, ai, br, bi):
    return ar * br - ai * bi, ar * bi + ai * br


def _dot_split(a, b):
    a_hi = a.astype(BF16)
    b_hi = b.astype(BF16)
    a_lo = (a - a_hi.astype(F32)).astype(BF16)
    b_lo = (b - b_hi.astype(F32)).astype(BF16)
    return (jnp.dot(a_hi, b_hi, preferred_element_type=F32) + jnp.dot(a_hi, b_lo, preferred_element_type=F32)
            + jnp.dot(a_lo, b_hi, preferred_element_type=F32))


def _s5_prep_kernel(lre_ref, lim_ref, ldt_ref, bre_ref, bim_ref, lre_t_ref, lim_t_ref, ldt_t_ref, cre_ref, cim_ref,
                    pre_ref, pim_ref, bcat_ref, wout_ref):
    lr = lre_ref[...]
    li = lim_ref[...]
    ar, ai, _ = _lam_bar(lr, li, ldt_ref[...])
    nr = ar - 1.0
    den = lr * lr + li * li
    cr = (nr * lr + ai * li) / den
    ci = (ai * lr - nr * li) / den
    bs = [_cmul(cr, ci, bre_ref[...], bim_ref[...])]
    for _ in range(1, S5_T):
        bs.append(_cmul(ar, ai, *bs[-1]))

    pr, pi = ar, ai
    pre_ref[0:1, :] = pr
    pim_ref[0:1, :] = pi
    for j in range(1, POW_ROWS):
        pr, pi = _cmul(pr, pi, ar, ai)
        pre_ref[j:j + 1, :] = pr
        pim_ref[j:j + 1, :] = pi

    tr, ti, _ = _lam_bar(lre_t_ref[...], lim_t_ref[...], ldt_t_ref[...])
    cs = [(cre_ref[...], cim_ref[...])]
    for _ in range(S5_T):
        cs.append(_cmul(tr, ti, *cs[-1]))

    g_shift, s_shift = S5_GROUP.bit_length() - 1, S5_STATE.bit_length() - 1
    b_shape = (S5_BLOCK_IN, S5_BLOCK_ST)
    b_mask = (lax.broadcasted_iota(jnp.int32, b_shape, 0) >> g_shift) == (
        lax.broadcasted_iota(jnp.int32, b_shape, 1) >> s_shift)
    c_shape = (S5_BLOCK_ST, S5_BLOCK_IN)
    c_mask = (lax.broadcasted_iota(jnp.int32, c_shape, 0) >> s_shift) == (
        lax.broadcasted_iota(jnp.int32, c_shape, 1) >> g_shift)

    def b_block(br, bi, blk):
        parts = []
        for x in (br, bi):
            x = jnp.concatenate([x[:, blk * S5_BLOCK_ST:(blk + 1) * S5_BLOCK_ST]] * S5_BLOCK_GROUPS, axis=0)
            parts.append(jnp.where(b_mask, x, 0.0))
        return jnp.concatenate(parts, axis=1)

    def c_block(xr, xi, blk):
        parts = []
        for x in (xr, -xi):
            x = jnp.concatenate([x[:, blk * S5_BLOCK_IN:(blk + 1) * S5_BLOCK_IN]] * S5_BLOCK_GROUPS, axis=0)
            parts.append(jnp.where(c_mask, x, 0.0))
        return jnp.concatenate(parts, axis=0)

    zero = jnp.zeros((S5_BLOCK_IN, S5_BLOCK_IN), F32)
    z_rows = 2 * S5_BLOCK_ST
    for blk in range(S5_BLOCKS):
        bms = [b_block(br, bi, blk) for br, bi in bs]
        cms = [c_block(xr, xi, blk) for xr, xi in cs]
        kcat = _dot_split(bms[0], jnp.concatenate(cms[:S5_T], axis=1))
        ks = [kcat[:, j * S5_BLOCK_IN:(j + 1) * S5_BLOCK_IN] for j in range(S5_T)]
        for f in range(S5_T):
            bcat_ref[blk, f * S5_BLOCK_IN:(f + 1) * S5_BLOCK_IN, :] = bms[S5_T - 1 - f].astype(BF16)
            row0 = z_rows + f * S5_BLOCK_IN
            wout_ref[blk, row0:row0 + S5_BLOCK_IN, :] = jnp.concatenate(
                [ks[g - f] if g >= f else zero for g in range(S5_T)], axis=1).astype(BF16)
        wout_ref[blk, 0:z_rows, :] = jnp.concatenate(cms[1:], axis=1).astype(BF16)


def _s5_prep(lre, lim, ldt, bre_t, bim_t, lre_t, lim_t, ldt_t, cre_t, cim_t):
    pow_shape = jax.ShapeDtypeStruct((POW_ROWS, S5_LANES), F32)
    return pl.pallas_call(
        _s5_prep_kernel,
        out_shape=[pow_shape, pow_shape,
                   jax.ShapeDtypeStruct((S5_BLOCKS, S5_T * S5_BLOCK_IN, 2 * S5_BLOCK_ST), BF16),
                   jax.ShapeDtypeStruct((S5_BLOCKS, 2 * S5_BLOCK_ST + S5_T * S5_BLOCK_IN, S5_T * S5_BLOCK_IN), BF16)],
        compiler_params=pltpu.CompilerParams(vmem_limit_bytes=VMEM_LIMIT),
        name="s5_prep",
    )(lre, lim, ldt, bre_t, bim_t, lre_t, lim_t, ldt_t, cre_t, cim_t)


def _ml_prep_kernel(wq_ref, wk_ref, wv_ref, oq_ref, ok_ref, ov_ref):
    shape = (MXU_TILE, MXU_TILE)
    shift = ML_QKV_BLOCK.bit_length() - 1
    row = lax.broadcasted_iota(jnp.int32, shape, 0)
    col = lax.broadcasted_iota(jnp.int32, shape, 1)
    same = (row >> shift) == (col >> shift)
    sel = col & (ML_QKV_BLOCK - 1)
    for w_ref, o_ref in ((wq_ref, oq_ref), (wk_ref, ok_ref), (wv_ref, ov_ref)):
        for c in range(D_MODEL // MXU_TILE):
            w = w_ref[c * MXU_TILE:(c + 1) * MXU_TILE, :]
            acc = jnp.zeros(shape, F32)
            for o in range(ML_QKV_BLOCK):
                acc = jnp.where(sel == o, w[:, o:o + 1], acc)
            o_ref[c] = jnp.where(same, acc, 0.0).astype(BF16)


def _ml_prep(wq, wk, wv):
    out = jax.ShapeDtypeStruct((D_MODEL // MXU_TILE, MXU_TILE, MXU_TILE), BF16)
    return pl.pallas_call(_ml_prep_kernel, out_shape=[out] * 3, name="ml_prep")(wq, wk, wv)


def _block_diag_dot(x, w_ref):
    return jnp.concatenate(
        [jnp.dot(x[:, c * MXU_TILE:(c + 1) * MXU_TILE], w_ref[c], preferred_element_type=F32)
         for c in range(w_ref.shape[0])], axis=1)


def _s5_kernel(u_ref, sre0_ref, sim0_ref, pre_ref, pim_ref, bcat_ref, wout_ref, d_ref, gw_ref, gb_ref, on_ref,
               y_ref, sre_ref, sim_ref, s_ref, *, tm, independent, n_tiles, frame_order_io):
    r = tm // SUBLANES
    io_layout = ((0, r),) if frame_order_io else ()
    nc = r // S5_T
    rows_c = nc * SUBLANES
    t = pl.program_id(1)

    @pl.when(t == 0)
    def _():
        sre_ref[...] = jnp.broadcast_to(sre0_ref[...], sre_ref.shape)
        sim_ref[...] = jnp.broadcast_to(sim0_ref[...], sim_ref.shape)

    ups = [_segment_rows(u_ref[g], io_layout) for g in range(n_tiles)]
    us = []
    for f in range(S5_T):
        us.append(jnp.concatenate(
            [up.reshape(nc, S5_T * SUBLANES, D_MODEL)[:, f * SUBLANES:(f + 1) * SUBLANES, :].reshape(rows_c, D_MODEL)
             for up in ups], axis=0).astype(BF16))

    ys = [[] for _ in range(S5_T)]
    for b in range(S5_BLOCKS):
        lanes = slice(b * S5_BLOCK_ST, (b + 1) * S5_BLOCK_ST)
        chans = slice(b * S5_BLOCK_IN, (b + 1) * S5_BLOCK_IN)
        re = slice(0, S5_BLOCK_ST)
        im = slice(S5_BLOCK_ST, 2 * S5_BLOCK_ST)
        ucat = jnp.concatenate([u[:, chans] for u in us], axis=1)
        s_ref[...] = jnp.dot(ucat, bcat_ref[b], preferred_element_type=F32)
        ltr = jnp.broadcast_to(pre_ref[S5_T - 1:S5_T, lanes], (SUBLANES, S5_BLOCK_ST))
        lti = jnp.broadcast_to(pim_ref[S5_T - 1:S5_T, lanes], (SUBLANES, S5_BLOCK_ST))

        def scan(cr, ci, row0):
            for c in range(nc):
                rows = slice(row0 + c * SUBLANES, row0 + (c + 1) * SUBLANES)
                nr = ltr * cr - lti * ci + s_ref[rows, re]
                ni = ltr * ci + lti * cr + s_ref[rows, im]
                s_ref[rows, re] = cr
                s_ref[rows, im] = ci
                cr, ci = nr, ni
            return cr, ci

        for g in range(n_tiles):
            row0 = g * rows_c
            if independent:
                fr, fi = scan(sre_ref[g, :, lanes], sim_ref[g, :, lanes], row0)
                sre_ref[g, :, lanes] = fr
                sim_ref[g, :, lanes] = fi
            else:
                zero = jnp.zeros((SUBLANES, S5_BLOCK_ST), F32)
                fr, fi = scan(zero, zero, row0)
                rr = pre_ref[r - 1:r, lanes]
                ri = pim_ref[r - 1:r, lanes]
                cr = sre_ref[g, :, lanes]
                ci = sim_ref[g, :, lanes]
                rows_r, rows_i = [], []
                for a in range(SUBLANES):
                    rows_r.append(cr)
                    rows_i.append(ci)
                    cr, ci = rr * cr - ri * ci + fr[a:a + 1], rr * ci + ri * cr + fi[a:a + 1]
                sre_ref[g, :, lanes] = cr
                sim_ref[g, :, lanes] = ci
                cin_r = jnp.concatenate(rows_r, axis=0)
                cin_i = jnp.concatenate(rows_i, axis=0)
                s_ref[row0:row0 + SUBLANES, re] = cin_r
                s_ref[row0:row0 + SUBLANES, im] = cin_i
                for c in range(1, nc):
                    rows = slice(row0 + c * SUBLANES, row0 + (c + 1) * SUBLANES)
                    pr = pre_ref[S5_T * c - 1:S5_T * c, lanes]
                    pi = pim_ref[S5_T * c - 1:S5_T * c, lanes]
                    s_ref[rows, re] += pr * cin_r - pi * cin_i
                    s_ref[rows, im] += pr * cin_i + pi * cin_r
        yb = jnp.dot(jnp.concatenate([s_ref[...].astype(BF16), ucat], axis=1), wout_ref[b],
                     preferred_element_type=F32)
        for f in range(S5_T):
            ys[f].append(yb[:, f * S5_BLOCK_IN:(f + 1) * S5_BLOCK_IN])

    yf = [jnp.concatenate(ys[f], axis=1) for f in range(S5_T)]
    for g in range(n_tiles):
        rows = slice(g * rows_c, (g + 1) * rows_c)
        y = jnp.concatenate([y[rows].reshape(nc, SUBLANES, D_MODEL) for y in yf], axis=1).reshape(tm, D_MODEL)
        y = y + d_ref[...] * ups[g]
        gl = jax.nn.gelu(y)
        gate = jnp.dot(gl.astype(BF16), gw_ref[...], preferred_element_type=F32)
        if io_layout:
            o = gl * jax.nn.sigmoid(gate + gb_ref[...])
            y_ref[g] = _segment_rows(_rms(o, on_ref[...]), io_layout, inverse=True).astype(BF16)
        else:
            rb = S5_TAIL_ROWS if tm % S5_TAIL_ROWS == 0 else tm
            for r0 in range(0, tm, rb):
                rs = slice(r0, r0 + rb)
                o = gl[rs] * jax.nn.sigmoid(gate[rs] + gb_ref[...])
                y_ref[g, rs, :] = _rms(o, on_ref[...]).astype(BF16)


def _s5(u, sre0, sim0, consts, *, n_outer, n_tiles, steps, tm, independent=False, shared_init=False,
        frame_order_io=False):
    srows = SUBLANES if independent else 1
    u5 = u.reshape(n_outer, n_tiles, steps, tm, D_MODEL)
    seq_spec = pl.BlockSpec((None, n_tiles, None, tm, D_MODEL), lambda o, t: (o, 0, t, 0, 0))
    if shared_init:
        st_in = pl.BlockSpec((None, 1, srows, S5_LANES), lambda o, t: (0, 0, 0, 0))
    else:
        st_in = pl.BlockSpec((None, n_tiles, srows, S5_LANES), lambda o, t: (o, 0, 0, 0))
    st_out = pl.BlockSpec((None, n_tiles, srows, S5_LANES), lambda o, t: (o, 0, 0, 0))
    st_shape = jax.ShapeDtypeStruct((n_outer, n_tiles, srows, S5_LANES), F32)
    y, sre, sim = pl.pallas_call(
        functools.partial(_s5_kernel, tm=tm, independent=independent, n_tiles=n_tiles,
                          frame_order_io=frame_order_io),
        grid=(n_outer, steps),
        in_specs=[seq_spec, st_in, st_in] + [_const_spec(a) for a in consts],
        out_specs=[seq_spec, st_out, st_out],
        out_shape=[jax.ShapeDtypeStruct(u5.shape, BF16), st_shape, st_shape],
        scratch_shapes=[pltpu.VMEM((n_tiles * tm // S5_T, 2 * S5_BLOCK_ST), F32)],
        compiler_params=pltpu.CompilerParams(dimension_semantics=("parallel", "arbitrary"),
                                             vmem_limit_bytes=VMEM_LIMIT),
        name="s5_mix",
    )(u5, sre0, sim0, *consts)
    return y.reshape(u.shape), sre, sim


def _ml_front(xm_ref, cv_ref, xbuf_ref, consts, bufs, tm, n_tiles):
    cw_ref, cb_ref, bdq_ref, bdk_ref, bdv_ref, gw_ref, gb_ref = consts
    xc_ref, q_ref, k_ref, v_ref, vf_ref, g_ref = bufs
    pre = ML_CONV - 1
    cw = cw_ref[...]
    xms, xcs = [], []
    for g in range(n_tiles):
        xm = xm_ref[g]
        xbuf_ref[g, SUBLANES - pre:SUBLANES, :] = cv_ref[g]
        xbuf_ref[g, SUBLANES:SUBLANES + tm, :] = xm
        xc = cb_ref[...] + cw[pre:pre + 1] * xm
        for j in range(pre):
            xc = xc + cw[j:j + 1] * xbuf_ref[g, SUBLANES - pre + j:SUBLANES - pre + j + tm, :]
        cv_ref[g] = xbuf_ref[g, SUBLANES + tm - pre:SUBLANES + tm, :]
        xms.append(xm)
        xcs.append(_silu(xc))
    xm = xms[0] if n_tiles == 1 else jnp.concatenate(xms, axis=0)
    xc = xcs[0] if n_tiles == 1 else jnp.concatenate(xcs, axis=0)
    xc_ref[...] = xc

    xcb = xc.astype(BF16)
    q = _block_diag_dot(xcb, bdq_ref)
    k = _block_diag_dot(xcb, bdk_ref)
    v = _block_diag_dot(xm.astype(BF16), bdv_ref)
    qb = q.astype(BF16)
    vb = v.astype(BF16)
    g_ref[...] = jnp.dot(jnp.concatenate([qb, k.astype(BF16), vb], axis=1), gw_ref[...],
                         preferred_element_type=F32) + gb_ref[...]
    q_ref[...] = qb
    k_ref[...] = (k * (ML_HEAD_DIM ** -0.5)).astype(BF16)
    v_ref[...] = vb
    vf_ref[...] = v


def _ml_back(bufs, z_ref, state, consts, y_ref, h_ref, tm, tc, n_tiles):
    xc_ref, q_ref, k_ref, v_ref, vf_ref, g_ref = bufs
    c_ref, n_ref, m_ref = state
    nw_ref, sk_ref, on_ref = consts
    lane = lax.broadcasted_iota(jnp.int32, (tc, GATE_LANES), 1)
    row = lax.broadcasted_iota(jnp.int32, (tc, tc), 0)
    col = lax.broadcasted_iota(jnp.int32, (tc, tc), 1)
    causal = row >= col
    tril = causal.astype(F32)

    def chunk(rows, g):
        gates = g_ref[rows, :]
        lf = jnp.minimum(gates, 0.0) - jnp.log1p(jnp.exp(-jnp.abs(gates)))
        lf = jnp.where((lane >= ML_HEADS) & (lane < 2 * ML_HEADS), lf, 0.0)
        cum = jnp.dot(tril, lf, preferred_element_type=F32, precision=lax.Precision.HIGHEST)
        arr = jnp.where(lane < ML_HEADS, gates, cum)
        arr_t = arr.T
        for h in range(ML_HEADS):
            hs = slice(h * ML_HEAD_DIM, (h + 1) * ML_HEAD_DIM)
            qh = q_ref[rows, hs]
            kh = k_ref[rows, hs]
            vh = v_ref[rows, hs]
            ig_col = arr[:, h:h + 1]
            b_col = arr[:, ML_HEADS + h:ML_HEADS + h + 1]
            ig_row = arr_t[h:h + 1, :]
            b_row = arr_t[ML_HEADS + h:ML_HEADS + h + 1, :]
            m_prev = m_ref[g, :, h:h + 1]
            c_prev = c_ref[g, h]
            n_prev = n_ref[g, h:h + 1, :]

            logw = jnp.where(causal, b_col - b_row + ig_row, -jnp.inf)
            log_inter = b_col + m_prev
            m_t = jnp.maximum(log_inter, jnp.max(logw, axis=-1, keepdims=True))
            w = jnp.exp(logw - m_t)
            a_inter = jnp.exp(log_inter - m_t)
            s = lax.dot_general(qh, kh, (((1,), (1,)), ((), ())), preferred_element_type=F32) * w
            inter = lax.dot_general(qh, c_prev.astype(BF16), (((1,), (1,)), ((), ())),
                                    preferred_element_type=F32)
            num = a_inter * inter + jnp.dot(s.astype(BF16), vh, preferred_element_type=F32)
            den = (a_inter * jnp.sum(qh.astype(F32) * n_prev, axis=-1, keepdims=True)
                   + jnp.sum(s, axis=-1, keepdims=True))
            hh = num / jnp.maximum(jnp.abs(den), jnp.exp(-m_t))
            mu = jnp.mean(hh, axis=-1, keepdims=True)
            hc = hh - mu
            var = jnp.mean(hc * hc, axis=-1, keepdims=True)
            h_ref[rows, hs] = hc * lax.rsqrt(var + EPS)

            b_last = b_col[tc - 1:tc, :]
            m_new = m_t[tc - 1:tc, :]
            g_state = jnp.exp(b_last + m_prev - m_new)
            g_src = jnp.exp(b_last - b_col + ig_col - m_new)
            vs = (vf_ref[rows, hs] * g_src).astype(BF16)
            c_ref[g, h] = g_state * c_prev + lax.dot_general(vs, kh, (((0,), (0,)), ((), ())),
                                                             preferred_element_type=F32)
            n_ref[g, h:h + 1, :] = g_state * n_prev + jnp.sum(g_src * kh.astype(F32), axis=0, keepdims=True)
            m_ref[g, :, h:h + 1] = m_new

    for j in range(tm // tc):
        for g in range(n_tiles):
            chunk(slice(g * tm + j * tc, g * tm + (j + 1) * tc), g)

    rb = ML_TAIL_ROWS if tm % ML_TAIL_ROWS == 0 else tm
    for g in range(n_tiles):
        for r0 in range(0, tm, rb):
            rows = slice(g * tm + r0, g * tm + r0 + rb)
            out = (h_ref[rows, :] * nw_ref[...] + sk_ref[...] * xc_ref[rows, :]) * _silu(z_ref[g, r0:r0 + rb, :])
            y_ref[g, r0:r0 + rb, :] = _rms(out, on_ref[...]).astype(BF16)


def _mlstm_kernel(xm_ref, z_ref, c0_ref, n0_ref, m0_ref, cv0_ref, cw_ref, cb_ref, bdq_ref, bdk_ref, bdv_ref,
                  gw_ref, gb_ref, nw_ref, sk_ref, on_ref,
                  y_ref, c_ref, n_ref, m_ref, cv_ref, xbuf_ref, h_ref, *bufs, tm, tc, n_tiles, pipelined):
    t = pl.program_id(1)
    front_consts = (cw_ref, cb_ref, bdq_ref, bdk_ref, bdv_ref, gw_ref, gb_ref)
    back_consts = (nw_ref, sk_ref, on_ref)
    state = (c_ref, n_ref, m_ref)
    n_buf = len(bufs) // 2 if pipelined else len(bufs)

    @pl.when(t == 0)
    def _():
        cv_ref[...] = jnp.broadcast_to(cv0_ref[...], cv_ref.shape)
        if pipelined:
            for ref in bufs[n_buf:]:
                ref[...] = jnp.zeros(ref.shape, ref.dtype)

    @pl.when(t <= (1 if pipelined else 0))
    def _():
        c_ref[...] = jnp.broadcast_to(c0_ref[...], c_ref.shape)
        n_ref[...] = jnp.broadcast_to(n0_ref[...], n_ref.shape)
        m_ref[...] = jnp.broadcast_to(m0_ref[...], m_ref.shape)

    def step(front_set, back_set):
        _ml_front(xm_ref, cv_ref, xbuf_ref, front_consts, front_set, tm, n_tiles)
        _ml_back(back_set, z_ref, state, back_consts, y_ref, h_ref, tm, tc, n_tiles)

    if pipelined:
        pl.when(t % 2 == 0)(lambda: step(bufs[:n_buf], bufs[n_buf:]))
        pl.when(t % 2 == 1)(lambda: step(bufs[n_buf:], bufs[:n_buf]))
    else:
        step(bufs, bufs)


def _mlstm(xm, z, c0, n0, m0, cv0, consts, *, n_outer, n_tiles, steps, tm, tc, shared_init=False, pipelined=False):
    shape5 = (n_outer, n_tiles, steps, tm, D_MODEL)
    block = (None, n_tiles, None, tm, D_MODEL)
    if pipelined:
        grid_steps = steps + 1
        front_spec = pl.BlockSpec(block, lambda o, t: (o, 0, jnp.minimum(t, steps - 1), 0, 0))
        back_spec = pl.BlockSpec(block, lambda o, t: (o, 0, jnp.maximum(t - 1, 0), 0, 0))
    else:
        grid_steps = steps
        front_spec = back_spec = pl.BlockSpec(block, lambda o, t: (o, 0, t, 0, 0))

    def st(shape, shared):
        zeros = (0,) * len(shape)
        if shared:
            return pl.BlockSpec((None, 1) + shape, lambda o, t: (0, 0) + zeros)
        return pl.BlockSpec((None, n_tiles) + shape, lambda o, t: (o, 0) + zeros)

    st_shapes = ((ML_HEADS, ML_HEAD_DIM, ML_HEAD_DIM), (ML_HEADS, ML_HEAD_DIM), (1, ML_HEADS),
                 (ML_CONV - 1, D_MODEL))
    rows = n_tiles * tm
    buf_set = [pltpu.VMEM((rows, D_MODEL), F32), pltpu.VMEM((rows, D_MODEL), BF16), pltpu.VMEM((rows, D_MODEL), BF16),
               pltpu.VMEM((rows, D_MODEL), BF16), pltpu.VMEM((rows, D_MODEL), F32), pltpu.VMEM((rows, GATE_LANES), F32)]
    outs = pl.pallas_call(
        functools.partial(_mlstm_kernel, tm=tm, tc=tc, n_tiles=n_tiles, pipelined=pipelined),
        grid=(n_outer, grid_steps),
        in_specs=[front_spec, back_spec] + [st(s, shared_init) for s in st_shapes]
                 + [_const_spec(a) for a in consts],
        out_specs=[back_spec] + [st(s, False) for s in st_shapes],
        out_shape=[jax.ShapeDtypeStruct(shape5, BF16)]
                  + [jax.ShapeDtypeStruct((n_outer, n_tiles) + s, F32) for s in st_shapes],
        scratch_shapes=[pltpu.VMEM((n_tiles, tm + 2 * SUBLANES, D_MODEL), F32), pltpu.VMEM((rows, D_MODEL), F32)]
                       + buf_set * (2 if pipelined else 1),
        compiler_params=pltpu.CompilerParams(dimension_semantics=("parallel", "arbitrary"),
                                             vmem_limit_bytes=VMEM_LIMIT),
        name="mlstm_mix",
    )(xm.reshape(shape5), z.reshape(shape5), c0, n0, m0, cv0, *consts)
    n_streams = n_outer * n_tiles
    return (outs[0].reshape(xm.shape),) + tuple(o.reshape((n_streams,) + o.shape[2:]) for o in outs[1:])


def _row(v):
    return v.reshape(1, -1).astype(F32)


def _pad_lanes(v, width):
    return jnp.pad(v, [(0, 0)] * (v.ndim - 1) + [(0, width - v.shape[-1])])


def kernel(x_prompt, x_sample, state_s5_re, state_s5_im, state_mlstm_c, state_mlstm_n, state_mlstm_m,
           state_mlstm_conv, meta_tokens, norm_ffn1, ffn1_gate, ffn1_up, ffn1_down, norm_mix, w_in,
           s5_lambda_re, s5_lambda_im, s5_log_dt, s5_b_re, s5_b_im, s5_c_re, s5_c_im, s5_d, s5_glu_w, s5_glu_b,
           ml_conv_w, ml_conv_b, ml_wq, ml_wk, ml_wv, ml_igate_w, ml_igate_b, ml_fgate_w, ml_fgate_b,
           ml_norm_w, ml_skip, out_norm_s5, out_norm_ml, w_out, norm_ffn2, ffn2_gate, ffn2_up, ffn2_down,
           norm_final):
    nb, seq, _ = x_prompt.shape
    ns, dseq, _ = x_sample.shape
    n_meta = meta_tokens.shape[0]
    n_p, n_s = nb * seq, ns * dseq
    tile_p = min(TOKEN_TILE, seq)

    glu_w = s5_glu_w[0].astype(BF16)

    ldt_gs = jnp.broadcast_to(s5_log_dt[0][:, None], (S5_GROUPS, S5_STATE))

    def state_major(a):
        return jnp.repeat(a.T, S5_GROUP, axis=1)

    pre, pim, bcat, wout = _s5_prep(
        s5_lambda_re[0].reshape(1, S5_LANES), s5_lambda_im[0].reshape(1, S5_LANES), ldt_gs.reshape(1, S5_LANES),
        s5_b_re[0].reshape(S5_LANES, S5_GROUP).T, s5_b_im[0].reshape(S5_LANES, S5_GROUP).T,
        state_major(s5_lambda_re[0]), state_major(s5_lambda_im[0]), state_major(ldt_gs),
        s5_c_re[0].reshape(D_MODEL, S5_STATE).T, s5_c_im[0].reshape(D_MODEL, S5_STATE).T)
    s5_consts = (pre, pim, bcat, wout, _row(s5_d[0]), glu_w, _row(s5_glu_b[0]), _row(out_norm_s5[0]))

    bdq, bdk, bdv = _ml_prep(ml_wq[0].reshape(D_MODEL, ML_QKV_BLOCK), ml_wk[0].reshape(D_MODEL, ML_QKV_BLOCK),
                             ml_wv[0].reshape(D_MODEL, ML_QKV_BLOCK))
    gate_w = _pad_lanes(jnp.concatenate([ml_igate_w[0], ml_fgate_w[0]], axis=1), GATE_LANES).astype(BF16)
    gate_b = _pad_lanes(jnp.concatenate([ml_igate_b[0], ml_fgate_b[0]])[None, :], GATE_LANES)
    ml_consts = (ml_conv_w[0], _row(ml_conv_b[0]), bdq, bdk, bdv, gate_w, gate_b, _row(ml_norm_w[0]),
                 _row(ml_skip[0]), _row(out_norm_ml[0]))

    seg_p = ((0, tile_p // SUBLANES),)
    pad_m = max(SUBLANES * S5_T - n_meta, 0)
    tm_m = n_meta + pad_m
    n_small = -(-(n_s + tm_m) // SMALL_TILE_ALIGN) * SMALL_TILE_ALIGN
    small = jnp.concatenate([x_sample.reshape(n_s, D_MODEL), jnp.zeros((pad_m, D_MODEL), F32), meta_tokens,
                             jnp.zeros((n_small - n_s - tm_m, D_MODEL), F32)], axis=0)
    g_ffn1, g_mix = _row(norm_ffn1[0]), _row(norm_mix[0])
    (x1_p, u_p, xm_p, z_p), (wg1, wu1, wd1, win) = _ffn_in_cast(
        x_prompt.reshape(n_p, D_MODEL), g_ffn1, ffn1_gate[0], ffn1_up[0], ffn1_down[0], g_mix, w_in[0],
        tm=tile_p, seg_layout=seg_p)
    x1_s, u_s, xm_s, z_s = _ffn_in(small, g_ffn1, wg1, wu1, wd1, g_mix, win, tm=n_small, seg_layout=())

    zs5 = jnp.zeros((1, 1, 1, S5_LANES), F32)
    _, mre, mim = _s5(u_s[n_s:n_s + tm_m], zs5, zs5, s5_consts, n_outer=1, n_tiles=1, steps=1, tm=tm_m,
                      frame_order_io=True)
    y5_p, pre_s, pim_s = _s5(u_p, mre, mim, s5_consts, n_outer=1, n_tiles=nb, steps=seq // tile_p, tm=tile_p,
                             shared_init=True)
    n_grp = ns // SUBLANES
    y5_s, sre_s, sim_s = _s5(u_s[:n_s], state_s5_re[0].reshape(1, n_grp, SUBLANES, S5_LANES),
                             state_s5_im[0].reshape(1, n_grp, SUBLANES, S5_LANES), s5_consts,
                             n_outer=1, n_tiles=n_grp, steps=1, tm=SUBLANES * dseq, independent=True,
                             frame_order_io=True)

    ml_tile_p = min(ML_TILE, seq)
    zc = jnp.zeros((1, 1, ML_HEADS, ML_HEAD_DIM, ML_HEAD_DIM), F32)
    zn = jnp.zeros((1, 1, ML_HEADS, ML_HEAD_DIM), F32)
    zm = jnp.zeros((1, 1, 1, ML_HEADS), F32)
    zcv = jnp.zeros((1, 1, ML_CONV - 1, D_MODEL), F32)
    m_rows = slice(n_s + pad_m, n_s + pad_m + n_meta)
    _, c_m, n_m, m_m, cv_m = _mlstm(xm_s[m_rows], z_s[m_rows], zc, zn, zm, zcv, ml_consts, n_outer=1, n_tiles=1,
                                    steps=1, tm=n_meta, tc=n_meta)
    ym_p, c_p, nn_p, m_p, cv_p = _mlstm(xm_p, z_p, c_m[None], n_m[None], m_m[None], cv_m[None], ml_consts,
                                        n_outer=1, n_tiles=nb, steps=seq // ml_tile_p, tm=ml_tile_p,
                                        tc=min(ML_CHUNK, ml_tile_p), shared_init=True, pipelined=True)
    grp = ML_SAMPLE_GROUP
    st5 = lambda a: a.reshape((ns // grp, grp) + a.shape[1:])
    ym_s, c_s, nn_s, m_s, cv_s = _mlstm(xm_s[:n_s], z_s[:n_s], st5(state_mlstm_c[0]), st5(state_mlstm_n[0]),
                                        st5(state_mlstm_m[0][:, None, :]), st5(state_mlstm_conv[0]), ml_consts,
                                        n_outer=ns // grp, n_tiles=grp, steps=1, tm=dseq, tc=dseq)

    y_p, y_s = _out_ffn(x1_p, y5_p, ym_p, x1_s, y5_s, ym_s, w_out[0], _row(norm_ffn2[0]), ffn2_gate[0], ffn2_up[0],
                        ffn2_down[0], _row(norm_final), tm=tile_p, tm_s=n_s // 2, seg_layout=seg_p)

    def s5_state(s, n):
        return s.reshape(1, n, S5_GROUPS, S5_STATE)

    return (y_p.reshape(nb, seq, D_MODEL), y_s.reshape(ns, dseq, D_MODEL),
            s5_state(pre_s, nb), s5_state(pim_s, nb), c_p[None], nn_p[None], m_p[:, 0][None], cv_p[None],
            s5_state(sre_s, ns), s5_state(sim_s, ns), c_s[None], nn_s[None], m_s[:, 0][None], cv_s[None])
```
